```python
import math
import jax, jax.numpy as jnp
from jax import lax
import numpy as np

D_MODEL = 1024
BATCH = 8
SEQ = 2048
DEPTH = 1

MEM_LEN = 256
ROPE_THETA = 500000.0
EPS = 1e-6
NEG = -1e30
Q_BLOCK = 128
N_BRANCH = 3
BRANCH_WIDTH = 512
A_HEADS = 8
A_HEAD_DIM = 64
A_ROT = A_HEAD_DIM // 4
IDX_HEADS = 8
IDX_DIM = 64
IDX_ROT = IDX_DIM // 4
TOPK_MAX = 256
B_HEADS = 8
B_NOPE = 64
B_ROPE = 32
B_VDIM = 64
B_QK = B_NOPE + B_ROPE
B_Q_RANK = 384
B_KV_RANK = 256
M_HEADS = 4
M_HEAD_DIM = 128

SPLITS = (
    A_HEADS * A_HEAD_DIM,
    A_HEADS * A_HEAD_DIM,
    A_HEADS * A_HEAD_DIM,
    IDX_HEADS * IDX_DIM,
    IDX_DIM,
    IDX_HEADS,
    BRANCH_WIDTH,
    B_Q_RANK,
    B_KV_RANK,
    B_ROPE,
    BRANCH_WIDTH,
    M_HEADS * M_HEAD_DIM,
    BRANCH_WIDTH,
    N_BRANCH * D_MODEL,
)
D_IN = (4 * A_HEADS * A_HEAD_DIM + IDX_DIM + IDX_HEADS + BRANCH_WIDTH
        + B_Q_RANK + B_KV_RANK + B_ROPE + BRANCH_WIDTH
        + M_HEADS * M_HEAD_DIM + BRANCH_WIDTH + N_BRANCH * D_MODEL)

kernel_name = "hybrid_dsa_mla_memory_gated_block"


def rms_norm(x, g):
    xf = x.astype(jnp.float32)
    y = xf * lax.rsqrt(jnp.mean(xf * xf, axis=-1, keepdims=True) + EPS)
    return (y * g.astype(jnp.float32)).astype(x.dtype)


def rope_tables(positions, rot_dim):
    inv_freq = ROPE_THETA ** (-(jnp.arange(0, rot_dim, 2, dtype=jnp.float32) / rot_dim))
    ang = positions.astype(jnp.float32)[..., None] * inv_freq
    return jnp.cos(ang), jnp.sin(ang)


def rotate(x, cos, sin):
    half = x.shape[-1] // 2
    x1 = x[..., :half].astype(jnp.float32)
    x2 = x[..., half:].astype(jnp.float32)
    return jnp.concatenate([x1 * cos - x2 * sin, x2 * cos + x1 * sin], axis=-1).astype(x.dtype)


def partial_rope(x, cos, sin):
    rd = 2 * cos.shape[-1]
    return jnp.concatenate([rotate(x[..., :rd], cos, sin), x[..., rd:]], axis=-1)


def to_blocks(t, nb):
    b = t.shape[0]
    return jnp.moveaxis(t.reshape((b, nb, Q_BLOCK) + t.shape[2:]), 1, 0)


def from_blocks(t):
    t = jnp.moveaxis(t, 0, 1)
    return t.reshape((t.shape[0], t.shape[1] * t.shape[2]) + t.shape[3:])


def dsa_attention(q, k, v, qi, ki, wi):
    b, s, h, d = q.shape
    nb = s // Q_BLOCK
    topk = min(TOPK_MAX, s // 4)
    key_pos = jnp.arange(s)
    ki_f = ki.astype(jnp.float32)
    scale = d ** -0.5

    def one_block(args):
        qb, qib, wib, start = args
        q_pos = start + jnp.arange(Q_BLOCK)
        causal = key_pos[None, :] <= q_pos[:, None]
        dots = jnp.einsum('bqhc,bsc->bhqs', qib.astype(jnp.float32), ki_f) * (IDX_DIM ** -0.5)
        index = jnp.einsum('bhqs,bqh->bqs', jax.nn.relu(dots),
                           wib.astype(jnp.float32) * (IDX_HEADS ** -0.5))
        index = jnp.where(causal[None], index, NEG)
        _, idx = lax.top_k(index, topk)
        valid = idx <= q_pos[None, :, None]
        k_sel = jax.vmap(lambda kb, ib: kb[ib])(k, idx)
        v_sel = jax.vmap(lambda vb, ib: vb[ib])(v, idx)
        sc = jnp.einsum('bqhd,bqkhd->bhqk', qb, k_sel).astype(jnp.float32) * scale
        sc = jnp.where(valid[:, None], sc, NEG)
        p = jax.nn.softmax(sc, axis=-1).astype(v.dtype)
        return jnp.einsum('bhqk,bqkhd->bqhd', p, v_sel)

    starts = jnp.arange(nb) * Q_BLOCK
    out = lax.map(one_block, (to_blocks(q, nb), to_blocks(qi, nb), to_blocks(wi, nb), starts))
    return from_blocks(out)


def causal_attention(q, k, v):
    b, s, h, dq = q.shape
    nb = s // Q_BLOCK
    key_pos = jnp.arange(s)
    scale = dq ** -0.5

    def one_block(args):
        qb, start = args
        q_pos = start + jnp.arange(Q_BLOCK)
        sc = jnp.einsum('bqhd,bkhd->bhqk', qb, k).astype(jnp.float32) * scale
        sc = jnp.where((key_pos[None, :] <= q_pos[:, None])[None, None], sc, NEG)
        p = jax.nn.softmax(sc, axis=-1).astype(v.dtype)
        return jnp.einsum('bhqk,bkhd->bqhd', p, v)

    starts = jnp.arange(nb) * Q_BLOCK
    return from_blocks(lax.map(one_block, (to_blocks(q, nb), starts)))


def setup_inputs(seed: int = 0) -> dict:
    key = jax.random.key(seed)
    ks = jax.random.split(key, 24)
    f32 = jnp.float32

    def w(k, shape, fan_in):
        return jax.random.normal(k, shape, f32) * (fan_in ** -0.5)

    def gain(k, shape):
        return 1.0 + 0.02 * jax.random.normal(k, shape, f32)

    x = jax.random.normal(ks[0], (BATCH, SEQ, D_MODEL), f32)
    mem = jax.random.normal(ks[1], (BATCH, MEM_LEN, D_MODEL), f32)
    offsets = jax.random.randint(ks[2], (BATCH, 1), 0, 4096, dtype=jnp.int32)
    positions = offsets + jnp.arange(SEQ, dtype=jnp.int32)[None, :]
    return {
        "x": x,
        "mem": mem,
        "positions": positions,
        "g_norm": gain(ks[3], (DEPTH, D_MODEL)),
        "w_in": w(ks[4], (DEPTH, D_MODEL, D_IN), D_MODEL),
        "g_qn_a": gain(ks[5], (DEPTH, A_HEAD_DIM)),
        "g_kn_a": gain(ks[6], (DEPTH, A_HEAD_DIM)),
        "g_cq": gain(ks[7], (DEPTH, B_Q_RANK)),
        "g_ckv": gain(ks[8], (DEPTH, B_KV_RANK)),
        "w_uq": w(ks[9], (DEPTH, B_Q_RANK, B_HEADS * B_QK), B_Q_RANK),
        "w_ukv": w(ks[10], (DEPTH, B_KV_RANK, B_HEADS * (B_NOPE + B_VDIM)), B_KV_RANK),
        "g_qn_b": gain(ks[11], (DEPTH, B_QK)),
        "g_kn_b": gain(ks[12], (DEPTH, B_QK)),
        "g_mem": gain(ks[13], (DEPTH, D_MODEL)),
        "w_mem_kv": w(ks[14], (DEPTH, D_MODEL, 2 * M_HEADS * M_HEAD_DIM), D_MODEL),
        "g_qn_m": gain(ks[15], (DEPTH, M_HEAD_DIM)),
        "g_kn_m": gain(ks[16], (DEPTH, M_HEAD_DIM)),
        "w_branch": w(ks[17], (DEPTH, N_BRANCH, BRANCH_WIDTH, D_MODEL), BRANCH_WIDTH),
        "w_out": w(ks[18], (DEPTH, D_MODEL, D_MODEL), D_MODEL),
    }


def reference(x, mem, positions, g_norm, w_in, g_qn_a, g_kn_a, g_cq, g_ckv, w_uq, w_ukv,
              g_qn_b, g_kn_b, g_mem, w_mem_kv, g_qn_m, g_kn_m, w_branch, w_out):
    b, s, _ = x.shape
    m_len = mem.shape[1]
    cos_a, sin_a = rope_tables(positions, A_ROT)
    cos_b, sin_b = rope_tables(positions, B_ROPE)
    split_at = [int(o) for o in np.cumsum(SPLITS)[:-1]]

    for layer in range(DEPTH):
        h = rms_norm(x, g_norm[layer])
        proj = h @ w_in[layer]
        (q_a, k_a, v_a, q_i, k_i, w_i, z_a, c_q, c_kv, k_rope, z_b, q_m, z_m,
         gate_logits) = jnp.split(proj, split_at, axis=-1)

        q_a = partial_rope(rms_norm(q_a.reshape(b, s, A_HEADS, A_HEAD_DIM), g_qn_a[layer]),
                           cos_a[:, :, None], sin_a[:, :, None])
        k_a = partial_rope(rms_norm(k_a.reshape(b, s, A_HEADS, A_HEAD_DIM), g_kn_a[layer]),
                           cos_a[:, :, None], sin_a[:, :, None])
        v_a = v_a.reshape(b, s, A_HEADS, A_HEAD_DIM)
        q_i = partial_rope(q_i.reshape(b, s, IDX_HEADS, IDX_DIM), cos_a[:, :, None], sin_a[:, :, None])
        k_i = partial_rope(k_i, cos_a, sin_a)
        o_a = dsa_attention(q_a, k_a, v_a, q_i, k_i, w_i)

        q_b = (rms_norm(c_q, g_cq[layer]) @ w_uq[layer]).reshape(b, s, B_HEADS, B_QK)
        kv_b = (rms_norm(c_kv, g_ckv[layer]) @ w_ukv[layer]).reshape(b, s, B_HEADS, B_NOPE + B_VDIM)
        k_nope, v_b = kv_b[..., :B_NOPE], kv_b[..., B_NOPE:]
        k_b = jnp.concatenate(
            [k_nope, jnp.broadcast_to(k_rope[:, :, None, :], (b, s, B_HEADS, B_ROPE))], axis=-1)
        q_b = rms_norm(q_b, g_qn_b[layer])
        k_b = rms_norm(k_b, g_kn_b[layer])
        q_b = jnp.concatenate(
            [q_b[..., :B_NOPE], rotate(q_b[..., B_NOPE:], cos_b[:, :, None], sin_b[:, :, None])], axis=-1)
        k_b = jnp.concatenate(
            [k_b[..., :B_NOPE], rotate(k_b[..., B_NOPE:], cos_b[:, :, None], sin_b[:, :, None])], axis=-1)
        o_b = causal_attention(q_b, k_b, v_b)

        kv_m = (rms_norm(mem, g_mem[layer]) @ w_mem_kv[layer]).reshape(b, m_len, 2, M_HEADS, M_HEAD_DIM)
        k_m = rms_norm(kv_m[:, :, 0], g_kn_m[layer])
        v_m = kv_m[:, :, 1]
        q_m = rms_norm(q_m.reshape(b, s, M_HEADS, M_HEAD_DIM), g_qn_m[layer])
        sc_m = jnp.einsum('bqhd,bmhd->bhqm', q_m, k_m).astype(jnp.float32) * (M_HEAD_DIM ** -0.5)
        p_m = jax.nn.softmax(sc_m, axis=-1).astype(v_m.dtype)
        o_m = jnp.einsum('bhqm,bmhd->bqhd', p_m, v_m)

        ys = jnp.stack([
            o_a.reshape(b, s, BRANCH_WIDTH) * jax.nn.silu(z_a),
            o_b.reshape(b, s, BRANCH_WIDTH) * jax.nn.silu(z_b),
            o_m.reshape(b, s, BRANCH_WIDTH) * jax.nn.silu(z_m),
        ], axis=0)
        branch = jnp.einsum('nbsw,nwd->nbsd', ys, w_branch[layer])
        gates = jax.nn.sigmoid(
            gate_logits.reshape(b, s, N_BRANCH, D_MODEL).astype(jnp.float32)).astype(x.dtype)
        merged = jnp.einsum('bsnd,nbsd->bsd', gates, branch)
        x = x + merged @ w_out[layer]
    return x
```

```python
import functools

import numpy as np
import jax
import jax.numpy as jnp
from jax import lax
from jax.experimental import pallas as pl
from jax.experimental.pallas import tpu as pltpu

F32 = jnp.float32
BF16 = jnp.bfloat16
I32 = jnp.int32

D_MODEL = 1024
ROPE_THETA = 500000.0
EPS = 1e-6
NEG = -1e30
Q_BLOCK = 128
N_BRANCH = 3
BRANCH_WIDTH = 512
A_HEADS = 8
A_HEAD_DIM = 64
A_ROT = A_HEAD_DIM // 4
IDX_HEADS = 8
IDX_DIM = 64
TOPK_MAX = 256
B_HEADS = 8
B_NOPE = 64
B_ROPE = 32
B_VDIM = 64
B_QK = B_NOPE + B_ROPE
B_Q_RANK = 384
B_KV_RANK = 256
M_HEADS = 4
M_HEAD_DIM = 128

LANES = 128
TM = 256
KC = 256
KEY_CLASS = 512
VMEM_LIMIT = 56 * 1024 * 1024
INT_MIN = -2 ** 31


def _nt(a, b):
    return lax.dot_general(a, b, (((1,), (1,)), ((), ())), preferred_element_type=F32)


def _mm(a, b):
    return jnp.dot(a, b, preferred_element_type=F32)


def _rms_lanes(xf, g_row, n=None):
    n = xf.shape[-1] if n is None else n
    ms = jnp.sum(xf * xf, axis=-1, keepdims=True) / n
    return xf * lax.rsqrt(ms + EPS) * g_row


def _rms_rows(blk, g_col, n=None):
    n = blk.shape[0] if n is None else n
    ms = jnp.sum(blk * blk, axis=0, keepdims=True) / n
    return blk * lax.rsqrt(ms + EPS) * g_col


def _rope_rows(blk, lo, half, cos_t, sin_t):
    x1 = blk[lo:lo + half]
    x2 = blk[lo + half:lo + 2 * half]
    parts = []
    if lo:
        parts.append(blk[:lo])
    parts += [x1 * cos_t - x2 * sin_t, x2 * cos_t + x1 * sin_t]
    if lo + 2 * half < blk.shape[0]:
        parts.append(blk[lo + 2 * half:])
    return jnp.concatenate(parts, axis=0)


def _rope_lanes(yc, half, cos_l, sin_first, sin_second):
    return (yc * cos_l + pltpu.roll(yc, LANES - half, 1) * sin_first
            + pltpu.roll(yc, half, 1) * sin_second)


def _proj_a_kernel(x_ref, posr_ref, posc_ref, gn_ref, wqa_ref, wqi_ref, wwi_ref, wka_ref, wva_ref,
                   wki_ref, gqa_ref, gka_ref, invc_ref, invr_ref, mfirst_ref, msecond_ref, grp_ref,
                   qat_ref, qib_ref, wt_ref, ka_ref, vat_ref, ki_ref):
    h = _rms_lanes(x_ref[...], gn_ref[...]).astype(BF16)
    ang_t = invc_ref[...] * posr_ref[...].astype(F32)
    cos_t, sin_t = jnp.cos(ang_t), jnp.sin(ang_t)
    half = A_ROT // 2

    qa = _nt(wqa_ref[...], h)
    gq = gqa_ref[...]
    for hh in range(A_HEADS):
        blk = _rms_rows(qa[hh * A_HEAD_DIM:(hh + 1) * A_HEAD_DIM], gq)
        blk = _rope_rows(blk, 0, half, cos_t, sin_t) * (A_HEAD_DIM ** -0.5)
        qat_ref[hh * A_HEAD_DIM:(hh + 1) * A_HEAD_DIM, :] = blk.astype(BF16)

    qi = _nt(wqi_ref[...], h)
    for hh in range(IDX_HEADS):
        blk = _rope_rows(qi[hh * IDX_DIM:(hh + 1) * IDX_DIM], 0, half, cos_t, sin_t)
        blk = (blk * (IDX_DIM ** -0.5)).astype(BF16)
        for j in range(TM // Q_BLOCK):
            qib_ref[j, 0:IDX_DIM, hh * LANES:(hh + 1) * LANES] = blk[:, j * Q_BLOCK:(j + 1) * Q_BLOCK]
    qib_ref[:, IDX_DIM:, :] = jnp.zeros((TM // Q_BLOCK, LANES - IDX_DIM, IDX_HEADS * LANES), BF16)

    wt_ref[...] = _nt(wwi_ref[...], h)[0:IDX_HEADS] * (IDX_HEADS ** -0.5)

    vat_ref[...] = _nt(wva_ref[...], h).astype(BF16)

    ang_l = posc_ref[...].astype(F32) * invr_ref[...]
    cos_l, sin_l = jnp.cos(ang_l), jnp.sin(ang_l)
    s_first = sin_l * mfirst_ref[...]
    s_second = sin_l * msecond_ref[...]

    ka = _mm(h, wka_ref[...])
    grp = grp_ref[...]
    gk = gka_ref[...]
    for c in range(A_HEADS * A_HEAD_DIM // LANES):
        kc = ka[:, c * LANES:(c + 1) * LANES]
        sq = kc * kc
        s_lo = jnp.sum(sq * grp, axis=-1, keepdims=True)
        s_hi = jnp.sum(sq, axis=-1, keepdims=True) - s_lo
        ms = jnp.where(grp > 0.5, s_lo, s_hi) / A_HEAD_DIM
        y = kc * lax.rsqrt(ms + EPS) * gk
        ka_ref[:, c * LANES:(c + 1) * LANES] = _rope_lanes(y, half, cos_l, s_first, s_second).astype(BF16)

    ki = _mm(h, wki_ref[...])
    ki_ref[...] = _rope_lanes(ki, half, cos_l, s_first, s_second).astype(BF16)


def _proj_b_kernel(x_ref, posr_ref, posc_ref, gn_ref, wcq_ref, wckv_ref, wkr_ref, wqm_ref, wuq_ref,
                   wuk_ref, wuv_ref, gcq_ref, gckv_ref, gqb_ref, gkb_ref, gqm_ref, invc_ref, invr_ref,
                   mfirst_ref, msecond_ref,
                   qbt_ref, kb_ref, vbt_ref, qmt_ref):
    h = _rms_lanes(x_ref[...], gn_ref[...]).astype(BF16)
    half = B_ROPE // 2
    ang_t = invc_ref[...] * posr_ref[...].astype(F32)
    cos_t, sin_t = jnp.cos(ang_t), jnp.sin(ang_t)
    ang_l = posc_ref[...].astype(F32) * invr_ref[...]
    cos_l, sin_l = jnp.cos(ang_l), jnp.sin(ang_l)
    s_first = sin_l * mfirst_ref[...]
    s_second = sin_l * msecond_ref[...]

    cq = _rms_lanes(_mm(h, wcq_ref[...]), gcq_ref[...]).astype(BF16)
    qb = _nt(wuq_ref[...], cq)
    gq = gqb_ref[...]
    for hh in range(B_HEADS):
        blk = _rms_rows(qb[hh * LANES:(hh + 1) * LANES], gq, n=B_QK)
        blk = _rope_rows(blk, B_NOPE, half, cos_t, sin_t) * (B_QK ** -0.5)
        qbt_ref[hh * LANES:(hh + 1) * LANES, :] = blk.astype(BF16)

    ckv = _rms_lanes(_mm(h, wckv_ref[...]), gckv_ref[...]).astype(BF16)
    kn = _mm(ckv, wuk_ref[...])
    kr = _mm(h, wkr_ref[...])
    gk = gkb_ref[...]
    for hh in range(B_HEADS):
        kc = kn[:, hh * LANES:(hh + 1) * LANES] + kr
        y = _rms_lanes(kc, gk, n=B_QK)
        kb_ref[:, hh * LANES:(hh + 1) * LANES] = _rope_lanes(y, half, cos_l, s_first, s_second).astype(BF16)
    vbt_ref[...] = _nt(wuv_ref[...], ckv).astype(BF16)

    qm = _nt(wqm_ref[...], h)
    gm = gqm_ref[...]
    for hh in range(M_HEADS):
        blk = _rms_rows(qm[hh * M_HEAD_DIM:(hh + 1) * M_HEAD_DIM], gm) * (M_HEAD_DIM ** -0.5)
        qmt_ref[hh * M_HEAD_DIM:(hh + 1) * M_HEAD_DIM, :] = blk.astype(BF16)


def _mem_kv_kernel(mem_ref, gmem_ref, wk_ref, wvt_ref, gkm_ref, km_ref, vmt_ref):
    hm = _rms_lanes(mem_ref[...], gmem_ref[...]).astype(BF16)
    k = _mm(hm, wk_ref[...])
    gk = gkm_ref[...]
    for hh in range(M_HEADS):
        kc = _rms_lanes(k[:, hh * M_HEAD_DIM:(hh + 1) * M_HEAD_DIM], gk)
        km_ref[:, hh * M_HEAD_DIM:(hh + 1) * M_HEAD_DIM] = kc.astype(BF16)
    vmt_ref[...] = _nt(wvt_ref[...], hm).astype(BF16)


def _attend(nk, n_heads, dv, q_of, k_of, v_of, bias_ref, s_ref, p_ref, ot_ref):
    nq = s_ref.shape[1]
    for hh in range(n_heads):
        q = q_of(hh)
        m = jnp.full((1, nq), -jnp.inf, F32)
        for c in range(nk // KC):
            sl = slice(c * KC, (c + 1) * KC)
            s = _mm(k_of(hh, sl), q)
            if bias_ref is not None:
                s = s + bias_ref[sl, :]
            s_ref[sl, :] = s
            m = jnp.maximum(m, jnp.max(s, axis=0, keepdims=True))
        l = jnp.zeros((1, nq), F32)
        for c in range(nk // KC):
            sl = slice(c * KC, (c + 1) * KC)
            p = jnp.exp(s_ref[sl, :] - m)
            l = l + jnp.sum(p, axis=0, keepdims=True)
            p_ref[sl, :] = p.astype(BF16)
        o = _mm(v_of(hh, nk), p_ref[0:nk, :])
        ot_ref[hh * dv:(hh + 1) * dv, :] = o / l


def _count(key_ref, nk, pred):
    cnt = jnp.zeros((1, LANES), I32)
    for c in range(nk // KC):
        hit = pred(key_ref[c * KC:(c + 1) * KC, :])
        cnt = cnt + jnp.sum(jnp.where(hit, 1, 0).astype(I32), axis=0, keepdims=True)
    return cnt


def _dsa_body(nk, start, qat_ref, qib_ref, wt_ref, ka_ref, vat_ref, ki_ref, oa_ref,
              key_ref, bias_ref, s_ref, p_ref, ot_ref):
    q_pos = start + lax.broadcasted_iota(I32, (1, LANES), 1)
    row = lax.broadcasted_iota(I32, (KC, LANES), 0)

    for c in range(nk // KC):
        sl = slice(c * KC, (c + 1) * KC)
        ki_c = ki_ref[sl, :]
        acc = jnp.zeros((KC, LANES), F32)
        for hh in range(IDX_HEADS):
            d = _mm(ki_c, qib_ref[0, :, hh * LANES:(hh + 1) * LANES])
            acc = acc + jnp.maximum(d, 0.0) * wt_ref[hh:hh + 1, :]
        acc = jnp.where(row + c * KC <= q_pos, acc, NEG)
        bits = pltpu.bitcast(acc, I32)
        key_ref[sl, :] = bits ^ ((bits >> 31) & 0x7FFFFFFF)

    def step(i, t_u):
        cand_u = t_u | jnp.left_shift(jnp.int32(1), 31 - i)
        cand_s = cand_u ^ INT_MIN
        cnt = _count(key_ref, nk, lambda k: k >= cand_s)
        return jnp.where(cnt >= TOPK_MAX, cand_u, t_u)

    t_u = lax.fori_loop(0, 32, step, jnp.zeros((1, LANES), I32))
    t_s = t_u ^ INT_MIN
    room = (TOPK_MAX - _count(key_ref, nk, lambda k: k > t_s)).astype(F32)

    tri = (lax.broadcasted_iota(I32, (KC, KC), 0) >= lax.broadcasted_iota(I32, (KC, KC), 1))
    tri = jnp.where(tri, 1.0, 0.0).astype(BF16)
    take_all = q_pos < TOPK_MAX
    running = jnp.zeros((1, LANES), F32)
    for c in range(nk // KC):
        sl = slice(c * KC, (c + 1) * KC)
        k = key_ref[sl, :]
        tie = k == t_s
        rank = _mm(tri, jnp.where(tie, 1.0, 0.0).astype(BF16)) + running
        running = rank[KC - 1:KC, :]
        sel = (k > t_s) | (tie & (rank <= room)) | take_all
        bias_ref[sl, :] = jnp.where(sel & (row + c * KC <= q_pos), 0.0, NEG)

    odd = (lax.broadcasted_iota(I32, (LANES, LANES), 0) >= A_HEAD_DIM)

    def q_of(hh):
        blk = qat_ref[(hh // 2) * LANES:(hh // 2 + 1) * LANES, :]
        keep = odd if hh % 2 else jnp.logical_not(odd)
        return jnp.where(keep, blk, jnp.zeros_like(blk))

    def k_of(hh, sl):
        return ka_ref[sl, (hh // 2) * LANES:(hh // 2 + 1) * LANES]

    def v_of(hh, n):
        return vat_ref[hh * A_HEAD_DIM:(hh + 1) * A_HEAD_DIM, 0:n]

    _attend(nk, A_HEADS, A_HEAD_DIM, q_of, k_of, v_of, bias_ref, s_ref, p_ref, ot_ref)
    oa_ref[...] = ot_ref[...].T


def _dsa_kernel(qat_ref, qib_ref, wt_ref, ka_ref, vat_ref, ki_ref, oa_ref,
                key_ref, bias_ref, s_ref, p_ref, ot_ref, *, seq):
    qb = pl.program_id(1)
    start = qb * Q_BLOCK
    per_class = KEY_CLASS // Q_BLOCK
    for cls in range(seq // KEY_CLASS):
        @pl.when(qb // per_class == cls)
        def _():
            _dsa_body(KEY_CLASS * (cls + 1), start, qat_ref, qib_ref, wt_ref, ka_ref, vat_ref, ki_ref,
                      oa_ref, key_ref, bias_ref, s_ref, p_ref, ot_ref)


def _mla_body(nk, start, qbt_ref, kb_ref, vbt_ref, ob_ref, bias_ref, s_ref, p_ref, ot_ref):
    q_pos = start + lax.broadcasted_iota(I32, (1, LANES), 1)
    row = lax.broadcasted_iota(I32, (KC, LANES), 0)
    for c in range(nk // KC):
        bias_ref[c * KC:(c + 1) * KC, :] = jnp.where(row + c * KC <= q_pos, 0.0, NEG)

    def q_of(hh):
        return qbt_ref[hh * LANES:(hh + 1) * LANES, :]

    def k_of(hh, sl):
        return kb_ref[sl, hh * LANES:(hh + 1) * LANES]

    def v_of(hh, n):
        return vbt_ref[hh * B_VDIM:(hh + 1) * B_VDIM, 0:n]

    _attend(nk, B_HEADS, B_VDIM, q_of, k_of, v_of, bias_ref, s_ref, p_ref, ot_ref)
    ob_ref[...] = ot_ref[...].T


def _mla_kernel(qbt_ref, kb_ref, vbt_ref, ob_ref, bias_ref, s_ref, p_ref, ot_ref, *, seq):
    qb = pl.program_id(1)
    start = qb * Q_BLOCK
    per_class = KEY_CLASS // Q_BLOCK
    for cls in range(seq // KEY_CLASS):
        @pl.when(qb // per_class == cls)
        def _():
            _mla_body(KEY_CLASS * (cls + 1), start, qbt_ref, kb_ref, vbt_ref, ob_ref,
                      bias_ref, s_ref, p_ref, ot_ref)


def _mem_attn_kernel(qmt_ref, km_ref, vmt_ref, om_ref, s_ref, p_ref, ot_ref, *, mem_len):
    def q_of(hh):
        return qmt_ref[hh * M_HEAD_DIM:(hh + 1) * M_HEAD_DIM, :]

    def k_of(hh, sl):
        return km_ref[sl, hh * M_HEAD_DIM:(hh + 1) * M_HEAD_DIM]

    def v_of(hh, n):
        return vmt_ref[hh * M_HEAD_DIM:(hh + 1) * M_HEAD_DIM, 0:n]

    _attend(mem_len, M_HEADS, M_HEAD_DIM, q_of, k_of, v_of, None, s_ref, p_ref, ot_ref)
    om_ref[...] = ot_ref[...].T


def _final_kernel(x_ref, oa_ref, ob_ref, om_ref, gn_ref, wz_ref, wg_ref, wb_ref, wo_ref, out_ref):
    x = x_ref[...]
    h = _rms_lanes(x, gn_ref[...]).astype(BF16)
    merged = jnp.zeros((TM, D_MODEL), F32)
    for n, o_ref in enumerate((oa_ref, ob_ref, om_ref)):
        z = _mm(h, wz_ref[n])
        y = (o_ref[...] * (z * jax.nn.sigmoid(z))).astype(BF16)
        branch = _mm(y, wb_ref[n])
        gate = jax.nn.sigmoid(_mm(h, wg_ref[n]))
        merged = merged + gate * branch
    out_ref[...] = x + _mm(merged.astype(BF16), wo_ref[...])


def _full(shape):
    return pl.BlockSpec(shape, lambda *_: (0,) * len(shape))


def _params(n_axes):
    return pltpu.CompilerParams(dimension_semantics=("arbitrary",) * n_axes,
                                vmem_limit_bytes=VMEM_LIMIT)


def _lane_pattern(inv, period, lo):
    half = inv.shape[0]
    j = np.arange(LANES) % period
    first = (j >= lo) & (j < lo + half)
    second = (j >= lo + half) & (j < lo + 2 * half)
    idx = np.where(first, j - lo, np.where(second, j - lo - half, 0))
    inv_l = jnp.where(jnp.asarray(first | second), inv[idx], 0.0).reshape(1, LANES)
    m_first = jnp.asarray(np.where(first, -1.0, 0.0), F32).reshape(1, LANES)
    m_second = jnp.asarray(np.where(second, 1.0, 0.0), F32).reshape(1, LANES)
    return inv_l, m_first, m_second


def kernel(x, mem, positions, g_norm, w_in, g_qn_a, g_kn_a, g_cq, g_ckv, w_uq, w_ukv, g_qn_b, g_kn_b,
           g_mem, w_mem_kv, g_qn_m, g_kn_m, w_branch, w_out):
    b, s, d = x.shape
    m_len = mem.shape[1]
    n = b * s
    nq = s // Q_BLOCK
    assert d == D_MODEL and s % KEY_CLASS == 0 and n % TM == 0 and m_len % KC == 0
    assert g_norm.shape[0] == 1, "single-layer block"

    w = w_in[0]
    off = np.cumsum([0, 512, 512, 512, 512, IDX_DIM, IDX_HEADS, BRANCH_WIDTH, B_Q_RANK, B_KV_RANK, B_ROPE,
                     BRANCH_WIDTH, M_HEADS * M_HEAD_DIM, BRANCH_WIDTH, N_BRANCH * D_MODEL])
    seg = [w[:, off[i]:off[i + 1]] for i in range(14)]
    (w_qa, w_ka, w_va, w_qi, w_ki, w_wi, w_za, w_cq, w_ckv, w_kr, w_zb, w_qm, w_zm, w_gate) = seg
    bf = lambda a: a.astype(BF16)
    wqa_t, wqi_t, wva_t, wqm_t = bf(w_qa.T), bf(w_qi.T), bf(w_va.T), bf(w_qm.T)
    wwi_t = bf(jnp.pad(w_wi.T, ((0, 16 - IDX_HEADS), (0, 0))))
    wki_p = bf(jnp.pad(w_ki, ((0, 0), (0, LANES - IDX_DIM))))
    wkr_p = bf(jnp.pad(w_kr, ((0, 0), (B_NOPE, LANES - B_QK))))
    wuq_t = bf(jnp.pad(w_uq[0].reshape(B_Q_RANK, B_HEADS, B_QK), ((0, 0), (0, 0), (0, LANES - B_QK)))
               .reshape(B_Q_RANK, B_HEADS * LANES).T)
    ukv = w_ukv[0].reshape(B_KV_RANK, B_HEADS, B_NOPE + B_VDIM)
    wuk_p = bf(jnp.pad(ukv[:, :, :B_NOPE], ((0, 0), (0, 0), (0, LANES - B_NOPE)))
               .reshape(B_KV_RANK, B_HEADS * LANES))
    wuv_t = bf(ukv[:, :, B_NOPE:].reshape(B_KV_RANK, B_HEADS * B_VDIM).T)
    wmk = bf(w_mem_kv[0][:, :M_HEADS * M_HEAD_DIM])
    wmv_t = bf(w_mem_kv[0][:, M_HEADS * M_HEAD_DIM:].T)
    wz = bf(jnp.stack([w_za, w_zb, w_zm]))
    wg = bf(w_gate.reshape(D_MODEL, N_BRANCH, D_MODEL).transpose(1, 0, 2))
    wb = bf(w_branch[0])
    wo = bf(w_out[0])

    gn = g_norm[0].reshape(1, D_MODEL)
    gqa_c = g_qn_a[0].reshape(A_HEAD_DIM, 1)
    gka_r = jnp.tile(g_kn_a[0], LANES // A_HEAD_DIM).reshape(1, LANES)
    gcq_r = g_cq[0].reshape(1, B_Q_RANK)
    gckv_r = g_ckv[0].reshape(1, B_KV_RANK)
    gqb_c = jnp.pad(g_qn_b[0], (0, LANES - B_QK)).reshape(LANES, 1)
    gkb_r = jnp.pad(g_kn_b[0], (0, LANES - B_QK)).reshape(1, LANES)
    gqm_c = g_qn_m[0].reshape(M_HEAD_DIM, 1)
    gkm_r = g_kn_m[0].reshape(1, M_HEAD_DIM)
    gmem_r = g_mem[0].reshape(1, D_MODEL)

    inv_a = ROPE_THETA ** (-(jnp.arange(0, A_ROT, 2, dtype=F32) / A_ROT))
    inv_b = ROPE_THETA ** (-(jnp.arange(0, B_ROPE, 2, dtype=F32) / B_ROPE))
    inva_l, mfa, msa = _lane_pattern(inv_a, A_HEAD_DIM, 0)
    invb_l, mfb, msb = _lane_pattern(inv_b, LANES, B_NOPE)
    grp = jnp.asarray(np.where(np.arange(LANES) < A_HEAD_DIM, 1.0, 0.0), F32).reshape(1, LANES)

    x2 = x.reshape(n, d)
    pos_r = positions.reshape(1, n)
    pos_c = positions.reshape(n, 1)
    tile = lambda width: pl.BlockSpec((TM, width), lambda i: (i, 0))
    tile_t = lambda rows: pl.BlockSpec((rows, TM), lambda i: (0, i))
    pos_specs = [pl.BlockSpec((1, TM), lambda i: (0, i)), pl.BlockSpec((TM, 1), lambda i: (i, 0))]

    a_in = [x2, pos_r, pos_c, gn, wqa_t, wqi_t, wwi_t, bf(w_ka), wva_t, wki_p, gqa_c, gka_r,
            inv_a.reshape(-1, 1), inva_l, mfa, msa, grp]
    qat, qib, wt, ka, vat, ki = pl.pallas_call(
        _proj_a_kernel,
        grid=(n // TM,),
        in_specs=[tile(d)] + pos_specs + [_full(a.shape) for a in a_in[3:]],
        out_specs=[tile_t(512), pl.BlockSpec((TM // Q_BLOCK, LANES, IDX_HEADS * LANES), lambda i: (i, 0, 0)),
                   tile_t(IDX_HEADS), tile(512), tile_t(512), tile(LANES)],
        out_shape=[jax.ShapeDtypeStruct((512, n), BF16),
                   jax.ShapeDtypeStruct((n // Q_BLOCK, LANES, IDX_HEADS * LANES), BF16),
                   jax.ShapeDtypeStruct((IDX_HEADS, n), F32),
                   jax.ShapeDtypeStruct((n, 512), BF16),
                   jax.ShapeDtypeStruct((512, n), BF16),
                   jax.ShapeDtypeStruct((n, LANES), BF16)],
        compiler_params=_params(1), name="proj_a",
    )(*a_in)

    b_in = [x2, pos_r, pos_c, gn, bf(w_cq), bf(w_ckv), wkr_p, wqm_t, wuq_t, wuk_p, wuv_t,
            gcq_r, gckv_r, gqb_c, gkb_r, gqm_c, inv_b.reshape(-1, 1), invb_l, mfb, msb]
    qbt, kb, vbt, qmt = pl.pallas_call(
        _proj_b_kernel,
        grid=(n // TM,),
        in_specs=[tile(d)] + pos_specs + [_full(a.shape) for a in b_in[3:]],
        out_specs=[tile_t(B_HEADS * LANES), tile(B_HEADS * LANES), tile_t(512), tile_t(512)],
        out_shape=[jax.ShapeDtypeStruct((B_HEADS * LANES, n), BF16),
                   jax.ShapeDtypeStruct((n, B_HEADS * LANES), BF16),
                   jax.ShapeDtypeStruct((512, n), BF16),
                   jax.ShapeDtypeStruct((512, n), BF16)],
        compiler_params=_params(1), name="proj_b",
    )(*b_in)

    km, vmt = pl.pallas_call(
        _mem_kv_kernel,
        grid=(b,),
        in_specs=[pl.BlockSpec((m_len, d), lambda i: (i, 0)), _full(gmem_r.shape), _full(wmk.shape),
                  _full(wmv_t.shape), _full(gkm_r.shape)],
        out_specs=[pl.BlockSpec((m_len, 512), lambda i: (i, 0)), pl.BlockSpec((512, m_len), lambda i: (0, i))],
        out_shape=[jax.ShapeDtypeStruct((b * m_len, 512), BF16), jax.ShapeDtypeStruct((512, b * m_len), BF16)],
        compiler_params=_params(1), name="mem_kv",
    )(mem.reshape(b * m_len, d), gmem_r, wmk, wmv_t, gkm_r)

    qcol = lambda rows: pl.BlockSpec((rows, Q_BLOCK), lambda bi, qi: (0, bi * nq + qi))
    seq_rows = lambda width: pl.BlockSpec((s, width), lambda bi, qi: (bi, 0))
    seq_cols = lambda rows: pl.BlockSpec((rows, s), lambda bi, qi: (0, bi))
    o_spec = pl.BlockSpec((Q_BLOCK, 512), lambda bi, qi: (bi * nq + qi, 0))
    o_shape = jax.ShapeDtypeStruct((n, 512), F32)
    attn_scratch = [pltpu.VMEM((s, LANES), F32), pltpu.VMEM((s, LANES), F32), pltpu.VMEM((s, LANES), BF16),
                    pltpu.VMEM((512, LANES), F32)]

    oa = pl.pallas_call(
        functools.partial(_dsa_kernel, seq=s),
        grid=(b, nq),
        in_specs=[qcol(512), pl.BlockSpec((1, LANES, IDX_HEADS * LANES), lambda bi, qi: (bi * nq + qi, 0, 0)),
                  qcol(IDX_HEADS), seq_rows(512), seq_cols(512), seq_rows(LANES)],
        out_specs=o_spec, out_shape=o_shape,
        scratch_shapes=[pltpu.VMEM((s, LANES), I32)] + attn_scratch,
        compiler_params=_params(2), name="dsa",
    )(qat, qib, wt, ka, vat, ki)

    ob = pl.pallas_call(
        functools.partial(_mla_kernel, seq=s),
        grid=(b, nq),
        in_specs=[qcol(B_HEADS * LANES), seq_rows(B_HEADS * LANES), seq_cols(512)],
        out_specs=o_spec, out_shape=o_shape,
        scratch_shapes=attn_scratch,
        compiler_params=_params(2), name="mla",
    )(qbt, kb, vbt)

    per_b = s // TM
    om = pl.pallas_call(
        functools.partial(_mem_attn_kernel, mem_len=m_len),
        grid=(n // TM,),
        in_specs=[tile_t(512), pl.BlockSpec((m_len, 512), lambda i: (i // per_b, 0)),
                  pl.BlockSpec((512, m_len), lambda i: (0, i // per_b))],
        out_specs=tile(512), out_shape=o_shape,
        scratch_shapes=[pltpu.VMEM((m_len, TM), F32), pltpu.VMEM((m_len, TM), BF16), pltpu.VMEM((512, TM), F32)],
        compiler_params=_params(1), name="mem_attn",
    )(qmt, km, vmt)

    out = pl.pallas_call(
        _final_kernel,
        grid=(n // TM,),
        in_specs=[tile(d), tile(512), tile(512), tile(512), _full(gn.shape), _full(wz.shape), _full(wg.shape),
                  _full(wb.shape), _full(wo.shape)],
        out_specs=tile(d), out_shape=jax.ShapeDtypeStruct((n, d), x.dtype),
        compiler_params=_params(1), name="final",
    )(x2, oa, ob, om, gn, wz, wg, wb, wo)
    return out.reshape(b, s, d)
```

```python
import functools

import numpy as np
import jax
import jax.numpy as jnp
from jax import lax
from jax.experimental import pallas as pl
from jax.experimental.pallas import tpu as pltpu

F32 = jnp.float32
BF16 = jnp.bfloat16
I32 = jnp.int32

D_MODEL = 1024
ROPE_THETA = 500000.0
EPS = 1e-6
NEG = -1e30
Q_BLOCK = 128
N_BRANCH = 3
BRANCH_WIDTH = 512
A_HEADS = 8
A_HEAD_DIM = 64
A_ROT = A_HEAD_DIM // 4
IDX_HEADS = 8
IDX_DIM = 64
TOPK_MAX = 256
B_HEADS = 8
B_NOPE = 64
B_ROPE = 32
B_VDIM = 64
B_QK = B_NOPE + B_ROPE
B_Q_RANK = 384
B_KV_RANK = 256
M_HEADS = 4
M_HEAD_DIM = 128

LANES = 128
TM = 256
KC = 256
KEY_CLASS = 256
VMEM_LIMIT = 56 * 1024 * 1024
INT_MIN = -2 ** 31
LOG2E = 1.4426950408889634
BOUNDED_SCORE_LIMIT = 32.0


def _nt(a, b):
    return lax.dot_general(a, b, (((1,), (1,)), ((), ())), preferred_element_type=F32)


def _mm(a, b):
    return jnp.dot(a, b, preferred_element_type=F32)


def _rms_lanes(xf, g_row, n=None):
    n = xf.shape[-1] if n is None else n
    ms = jnp.sum(xf * xf, axis=-1, keepdims=True) / n
    return xf * lax.rsqrt(ms + EPS) * g_row


def _rms_rows(blk, g_col, n=None):
    n = blk.shape[0] if n is None else n
    ms = jnp.sum(blk * blk, axis=0, keepdims=True) / n
    return blk * lax.rsqrt(ms + EPS) * g_col


def _rope_rows(blk, lo, half, cos_t, sin_t):
    x1 = blk[lo:lo + half]
    x2 = blk[lo + half:lo + 2 * half]
    parts = []
    if lo:
        parts.append(blk[:lo])
    parts += [x1 * cos_t - x2 * sin_t, x2 * cos_t + x1 * sin_t]
    if lo + 2 * half < blk.shape[0]:
        parts.append(blk[lo + 2 * half:])
    return jnp.concatenate(parts, axis=0)


def _rope_lanes(yc, half, cos_l, sin_first, sin_second):
    return (yc * cos_l + pltpu.roll(yc, LANES - half, 1) * sin_first
            + pltpu.roll(yc, half, 1) * sin_second)


def _proj_a_kernel(x_ref, posr_ref, posc_ref, gn_ref, wqa_ref, wqi_ref, wwi_ref, wka_ref, wva_ref,
                   wki_ref, gqa_ref, gka_ref, invc_ref, invr_ref, mfirst_ref, msecond_ref, grp_ref,
                   qat_ref, qib_ref, wt_ref, ka_ref, vat_ref, ki_ref):
    h = _rms_lanes(x_ref[...], gn_ref[...]).astype(BF16)
    ang_t = invc_ref[...] * posr_ref[...].astype(F32)
    cos_t, sin_t = jnp.cos(ang_t), jnp.sin(ang_t)
    half = A_ROT // 2

    qa = _nt(wqa_ref[...], h)
    gq = gqa_ref[...]
    for hh in range(A_HEADS):
        blk = _rms_rows(qa[hh * A_HEAD_DIM:(hh + 1) * A_HEAD_DIM], gq)
        blk = _rope_rows(blk, 0, half, cos_t, sin_t) * (A_HEAD_DIM ** -0.5 * LOG2E)
        qat_ref[hh * A_HEAD_DIM:(hh + 1) * A_HEAD_DIM, :] = blk.astype(BF16)

    qi = _nt(wqi_ref[...], h)
    for hh in range(IDX_HEADS):
        blk = _rope_rows(qi[hh * IDX_DIM:(hh + 1) * IDX_DIM], 0, half, cos_t, sin_t)
        blk = (blk * (IDX_DIM ** -0.5)).astype(BF16)
        for j in range(TM // Q_BLOCK):
            qib_ref[j, 0:IDX_DIM, hh * LANES:(hh + 1) * LANES] = blk[:, j * Q_BLOCK:(j + 1) * Q_BLOCK]
    qib_ref[:, IDX_DIM:, :] = jnp.zeros((TM // Q_BLOCK, LANES - IDX_DIM, IDX_HEADS * LANES), BF16)

    wt_ref[...] = _nt(wwi_ref[...], h)[0:IDX_HEADS] * (IDX_HEADS ** -0.5)

    vat_ref[...] = _nt(wva_ref[...], h).astype(BF16)

    ang_l = posc_ref[...].astype(F32) * invr_ref[...]
    cos_l, sin_l = jnp.cos(ang_l), jnp.sin(ang_l)
    s_first = sin_l * mfirst_ref[...]
    s_second = sin_l * msecond_ref[...]

    ka = _mm(h, wka_ref[...])
    grp = grp_ref[...]
    gk = gka_ref[...]
    for c in range(A_HEADS * A_HEAD_DIM // LANES):
        kc = ka[:, c * LANES:(c + 1) * LANES]
        sq = kc * kc
        s_lo = jnp.sum(sq * grp, axis=-1, keepdims=True)
        s_hi = jnp.sum(sq, axis=-1, keepdims=True) - s_lo
        ms = jnp.where(grp > 0.5, s_lo, s_hi) / A_HEAD_DIM
        y = kc * lax.rsqrt(ms + EPS) * gk
        ka_ref[:, c * LANES:(c + 1) * LANES] = _rope_lanes(y, half, cos_l, s_first, s_second).astype(BF16)

    ki = _mm(h, wki_ref[...])
    ki_ref[...] = _rope_lanes(ki, half, cos_l, s_first, s_second).astype(BF16)


def _proj_b_kernel(x_ref, posr_ref, posc_ref, gn_ref, wcq_ref, wckv_ref, wkr_ref, wqm_ref, wuq_ref,
                   wuk_ref, wuv_ref, gcq_ref, gckv_ref, gqb_ref, gkb_ref, gqm_ref, invc_ref, invr_ref,
                   mfirst_ref, msecond_ref,
                   qbt_ref, kb_ref, vbt_ref, qmt_ref):
    h = _rms_lanes(x_ref[...], gn_ref[...]).astype(BF16)
    half = B_ROPE // 2
    ang_t = invc_ref[...] * posr_ref[...].astype(F32)
    cos_t, sin_t = jnp.cos(ang_t), jnp.sin(ang_t)
    ang_l = posc_ref[...].astype(F32) * invr_ref[...]
    cos_l, sin_l = jnp.cos(ang_l), jnp.sin(ang_l)
    s_first = sin_l * mfirst_ref[...]
    s_second = sin_l * msecond_ref[...]

    cq = _rms_lanes(_mm(h, wcq_ref[...]), gcq_ref[...]).astype(BF16)
    qb = _nt(wuq_ref[...], cq)
    gq = gqb_ref[...]
    for hh in range(B_HEADS):
        blk = _rms_rows(qb[hh * LANES:(hh + 1) * LANES], gq, n=B_QK)
        blk = _rope_rows(blk, B_NOPE, half, cos_t, sin_t) * (B_QK ** -0.5 * LOG2E)
        qbt_ref[hh * LANES:(hh + 1) * LANES, :] = blk.astype(BF16)

    ckv = _rms_lanes(_mm(h, wckv_ref[...]), gckv_ref[...]).astype(BF16)
    kn = _mm(ckv, wuk_ref[...])
    kr = _mm(h, wkr_ref[...])
    gk = gkb_ref[...]
    for hh in range(B_HEADS):
        kc = kn[:, hh * LANES:(hh + 1) * LANES] + kr
        y = _rms_lanes(kc, gk, n=B_QK)
        kb_ref[:, hh * LANES:(hh + 1) * LANES] = _rope_lanes(y, half, cos_l, s_first, s_second).astype(BF16)
    vbt_ref[...] = _nt(wuv_ref[...], ckv).astype(BF16)

    qm = _nt(wqm_ref[...], h)
    gm = gqm_ref[...]
    for hh in range(M_HEADS):
        blk = _rms_rows(qm[hh * M_HEAD_DIM:(hh + 1) * M_HEAD_DIM], gm) * (M_HEAD_DIM ** -0.5 * LOG2E)
        qmt_ref[hh * M_HEAD_DIM:(hh + 1) * M_HEAD_DIM, :] = blk.astype(BF16)


def _mem_kv_kernel(mem_ref, gmem_ref, wk_ref, wvt_ref, gkm_ref, km_ref, vmt_ref):
    hm = _rms_lanes(mem_ref[...], gmem_ref[...]).astype(BF16)
    k = _mm(hm, wk_ref[...])
    gk = gkm_ref[...]
    for hh in range(M_HEADS):
        kc = _rms_lanes(k[:, hh * M_HEAD_DIM:(hh + 1) * M_HEAD_DIM], gk)
        km_ref[:, hh * M_HEAD_DIM:(hh + 1) * M_HEAD_DIM] = kc.astype(BF16)
    vmt_ref[...] = _nt(wvt_ref[...], hm).astype(BF16)


def _attend(nk, n_heads, dv, q_of, k_of, v_of, bias_of, bounded, s_ref, p_ref, ot_ref):
    nq = ot_ref.shape[1]
    chunks = [slice(c * KC, (c + 1) * KC) for c in range(nk // KC)]

    def scores(hh, q, c):
        s = _mm(k_of(hh, chunks[c]), q)
        b = bias_of(c)
        return s if b is None else s + b

    def probabilities(hh):
        q = q_of(hh)
        if bounded:
            l8 = jnp.zeros((8, nq), F32)
            for c in range(len(chunks)):
                p = jnp.exp2(scores(hh, q, c))
                l8 = l8 + p.reshape(KC // 8, 8, nq).sum(axis=0)
                p_ref[hh % 2, chunks[c], :] = p.astype(BF16)
            return jnp.sum(l8, axis=0, keepdims=True)
        m = jnp.full((1, nq), -jnp.inf, F32)
        for c in range(len(chunks)):
            s = scores(hh, q, c)
            s_ref[chunks[c], :] = s
            m = jnp.maximum(m, jnp.max(s, axis=0, keepdims=True))
        l = jnp.zeros((1, nq), F32)
        for c in range(len(chunks)):
            p = jnp.exp2(s_ref[chunks[c], :] - m)
            l = l + jnp.sum(p, axis=0, keepdims=True)
            p_ref[hh % 2, chunks[c], :] = p.astype(BF16)
        return l

    def weighted_values(hh, l):
        o = _mm(v_of(hh, slice(0, nk)), p_ref[hh % 2, 0:nk, :])
        ot_ref[hh * dv:(hh + 1) * dv, :] = o / l

    l_prev = probabilities(0)
    for hh in range(1, n_heads):
        l_cur = probabilities(hh)
        weighted_values(hh - 1, l_prev)
        l_prev = l_cur
    weighted_values(n_heads - 1, l_prev)


def _count(key_ref, nk, pred):
    cnt = jnp.zeros((8, LANES), I32)
    for c in range(nk // KC):
        hit = pred(key_ref[c * KC:(c + 1) * KC, :])
        cnt = cnt + jnp.where(hit, 1, 0).astype(I32).reshape(KC // 8, 8, LANES).sum(axis=0)
    return jnp.sum(cnt, axis=0, keepdims=True)


def _select_topk(nk, q_pos, row, chunks, qib_ref, wt_ref, ki_ref, key_ref, bias_ref):
    for c, sl in enumerate(chunks):
        ki_c = ki_ref[sl, :]
        acc = jnp.zeros((KC, LANES), F32)
        for hh in range(IDX_HEADS):
            d = _mm(ki_c, qib_ref[0, :, hh * LANES:(hh + 1) * LANES])
            acc = acc + jnp.maximum(d, 0.0) * wt_ref[hh:hh + 1, :]
        acc = jnp.where(row + c * KC <= q_pos, acc, NEG)
        bits = pltpu.bitcast(acc, I32)
        key_ref[sl, :] = bits ^ ((bits >> 31) & 0x7FFFFFFF)

    def step(i, carry):
        t_u, cnt_t = carry
        cand_u = t_u | jnp.left_shift(jnp.int32(1), 31 - i)
        cand_s = cand_u ^ INT_MIN
        cnt = _count(key_ref, nk, lambda k: k >= cand_s)
        ok = cnt >= TOPK_MAX
        return jnp.where(ok, cand_u, t_u), jnp.where(ok, cnt, cnt_t)

    t_u, cnt_t = lax.fori_loop(0, 32, step, (jnp.zeros((1, LANES), I32), jnp.full((1, LANES), nk, I32)))
    t_s = t_u ^ INT_MIN
    take_all = q_pos < TOPK_MAX
    over = (cnt_t > TOPK_MAX) & jnp.logical_not(take_all)
    split_ties = jnp.max(jnp.where(over, 1, 0)) > 0

    @pl.when(jnp.logical_not(split_ties))
    def _():
        for c, sl in enumerate(chunks):
            sel = (key_ref[sl, :] >= t_s) | take_all
            bias_ref[sl, :] = jnp.where(sel & (row + c * KC <= q_pos), 0.0, NEG)

    @pl.when(split_ties)
    def _():
        room = (TOPK_MAX - _count(key_ref, nk, lambda k: k > t_s)).astype(F32)
        tri = lax.broadcasted_iota(I32, (KC, KC), 0) >= lax.broadcasted_iota(I32, (KC, KC), 1)
        tri = jnp.where(tri, 1.0, 0.0).astype(BF16)
        running = jnp.zeros((1, LANES), F32)
        for c, sl in enumerate(chunks):
            k = key_ref[sl, :]
            tie = k == t_s
            rank = _mm(tri, jnp.where(tie, 1.0, 0.0).astype(BF16)) + running
            running = rank[KC - 1:KC, :]
            sel = (k > t_s) | (tie & (rank <= room)) | take_all
            bias_ref[sl, :] = jnp.where(sel & (row + c * KC <= q_pos), 0.0, NEG)


def _dsa_body(nk, start, bounded, qat_ref, qib_ref, wt_ref, ka_ref, vat_ref, ki_ref, oa_ref,
              key_ref, bias_ref, s_ref, p_ref, ot_ref):
    q_pos = start + lax.broadcasted_iota(I32, (1, LANES), 1)
    row = lax.broadcasted_iota(I32, (KC, LANES), 0)
    chunks = [slice(c * KC, (c + 1) * KC) for c in range(nk // KC)]

    if nk <= TOPK_MAX:
        for c, sl in enumerate(chunks):
            bias_ref[sl, :] = jnp.where(row + c * KC <= q_pos, 0.0, NEG)
    else:
        _select_topk(nk, q_pos, row, chunks, qib_ref, wt_ref, ki_ref, key_ref, bias_ref)

    odd = (lax.broadcasted_iota(I32, (LANES, LANES), 0) >= A_HEAD_DIM)

    def q_of(hh):
        blk = qat_ref[(hh // 2) * LANES:(hh // 2 + 1) * LANES, :]
        keep = odd if hh % 2 else jnp.logical_not(odd)
        return jnp.where(keep, blk, jnp.zeros_like(blk))

    def k_of(hh, sl):
        return ka_ref[sl, (hh // 2) * LANES:(hh // 2 + 1) * LANES]

    def v_of(hh, sl):
        return vat_ref[hh * A_HEAD_DIM:(hh + 1) * A_HEAD_DIM, sl]

    _attend(nk, A_HEADS, A_HEAD_DIM, q_of, k_of, v_of, lambda c: bias_ref[chunks[c], :], bounded,
            s_ref, p_ref, ot_ref)
    oa_ref[...] = ot_ref[...].T


def _dsa_kernel(qat_ref, qib_ref, wt_ref, ka_ref, vat_ref, ki_ref, oa_ref,
                key_ref, bias_ref, s_ref, p_ref, ot_ref, *, seq, bounded):
    qb = pl.program_id(1)
    start = qb * Q_BLOCK
    per_class = KEY_CLASS // Q_BLOCK
    for cls in range(seq // KEY_CLASS):
        @pl.when(qb // per_class == cls)
        def _():
            _dsa_body(KEY_CLASS * (cls + 1), start, bounded, qat_ref, qib_ref, wt_ref, ka_ref, vat_ref,
                      ki_ref, oa_ref, key_ref, bias_ref, s_ref, p_ref, ot_ref)


def _mla_body(nk, start, bounded, qbt_ref, kb_ref, vbt_ref, ob_ref, bias_ref, s_ref, p_ref, ot_ref):
    last = nk // KC - 1
    q_pos = start + lax.broadcasted_iota(I32, (1, LANES), 1)
    row = lax.broadcasted_iota(I32, (KC, LANES), 0)
    bias_ref[0:KC, :] = jnp.where(row + last * KC <= q_pos, 0.0, NEG)

    def q_of(hh):
        return qbt_ref[hh * LANES:(hh + 1) * LANES, :]

    def k_of(hh, sl):
        return kb_ref[sl, hh * LANES:(hh + 1) * LANES]

    def v_of(hh, sl):
        return vbt_ref[hh * B_VDIM:(hh + 1) * B_VDIM, sl]

    _attend(nk, B_HEADS, B_VDIM, q_of, k_of, v_of, lambda c: bias_ref[0:KC, :] if c == last else None,
            bounded, s_ref, p_ref, ot_ref)
    ob_ref[...] = ot_ref[...].T


def _mla_kernel(qbt_ref, kb_ref, vbt_ref, ob_ref, bias_ref, s_ref, p_ref, ot_ref, *, seq, bounded):
    qb = pl.program_id(1)
    start = qb * Q_BLOCK
    per_class = KEY_CLASS // Q_BLOCK
    for cls in range(seq // KEY_CLASS):
        @pl.when(qb // per_class == cls)
        def _():
            _mla_body(KEY_CLASS * (cls + 1), start, bounded, qbt_ref, kb_ref, vbt_ref, ob_ref,
                      bias_ref, s_ref, p_ref, ot_ref)


def _mem_attn_kernel(qmt_ref, km_ref, vmt_ref, om_ref, s_ref, p_ref, ot_ref, *, mem_len, bounded):
    def q_of(hh):
        return qmt_ref[hh * M_HEAD_DIM:(hh + 1) * M_HEAD_DIM, :]

    def k_of(hh, sl):
        return km_ref[sl, hh * M_HEAD_DIM:(hh + 1) * M_HEAD_DIM]

    def v_of(hh, sl):
        return vmt_ref[hh * M_HEAD_DIM:(hh + 1) * M_HEAD_DIM, sl]

    _attend(mem_len, M_HEADS, M_HEAD_DIM, q_of, k_of, v_of, lambda c: None, bounded, s_ref, p_ref, ot_ref)
    om_ref[...] = ot_ref[...].T


def _final_kernel(x_ref, oa_ref, ob_ref, om_ref, gn_ref, wz_ref, wg_ref, wb_ref, wo_ref, out_ref):
    x = x_ref[...]
    h = _rms_lanes(x, gn_ref[...]).astype(BF16)
    merged = jnp.zeros((TM, D_MODEL), F32)
    for n, o_ref in enumerate((oa_ref, ob_ref, om_ref)):
        z = _mm(h, wz_ref[n])
        y = (o_ref[...] * (z * jax.nn.sigmoid(z))).astype(BF16)
        branch = _mm(y, wb_ref[n])
        gate = jax.nn.sigmoid(_mm(h, wg_ref[n]))
        merged = merged + gate * branch
    out_ref[...] = x + _mm(merged.astype(BF16), wo_ref[...])


def _full(shape):
    return pl.BlockSpec(shape, lambda *_: (0,) * len(shape))


def _params(n_axes):
    return pltpu.CompilerParams(dimension_semantics=("arbitrary",) * n_axes,
                                vmem_limit_bytes=VMEM_LIMIT)


def _lane_pattern(inv, period, lo):
    half = inv.shape[0]
    j = np.arange(LANES) % period
    first = (j >= lo) & (j < lo + half)
    second = (j >= lo + half) & (j < lo + 2 * half)
    idx = np.where(first, j - lo, np.where(second, j - lo - half, 0))
    inv_l = jnp.where(jnp.asarray(first | second), inv[idx], 0.0).reshape(1, LANES)
    m_first = jnp.asarray(np.where(first, -1.0, 0.0), F32).reshape(1, LANES)
    m_second = jnp.asarray(np.where(second, 1.0, 0.0), F32).reshape(1, LANES)
    return inv_l, m_first, m_second


def kernel(x, mem, positions, g_norm, w_in, g_qn_a, g_kn_a, g_cq, g_ckv, w_uq, w_ukv, g_qn_b, g_kn_b,
           g_mem, w_mem_kv, g_qn_m, g_kn_m, w_branch, w_out):
    b, s, d = x.shape
    m_len = mem.shape[1]
    n = b * s
    nq = s // Q_BLOCK
    assert d == D_MODEL and s % KEY_CLASS == 0 and n % TM == 0 and m_len % KC == 0
    assert g_norm.shape[0] == 1, "single-layer block"

    w = w_in[0]
    off = np.cumsum([0, 512, 512, 512, 512, IDX_DIM, IDX_HEADS, BRANCH_WIDTH, B_Q_RANK, B_KV_RANK, B_ROPE,
                     BRANCH_WIDTH, M_HEADS * M_HEAD_DIM, BRANCH_WIDTH, N_BRANCH * D_MODEL])
    seg = [w[:, off[i]:off[i + 1]] for i in range(14)]
    (w_qa, w_ka, w_va, w_qi, w_ki, w_wi, w_za, w_cq, w_ckv, w_kr, w_zb, w_qm, w_zm, w_gate) = seg
    bf = lambda a: a.astype(BF16)
    wqa_t, wqi_t, wva_t, wqm_t = bf(w_qa.T), bf(w_qi.T), bf(w_va.T), bf(w_qm.T)
    wwi_t = bf(jnp.pad(w_wi.T, ((0, 16 - IDX_HEADS), (0, 0))))
    wki_p = bf(jnp.pad(w_ki, ((0, 0), (0, LANES - IDX_DIM))))
    wkr_p = bf(jnp.pad(w_kr, ((0, 0), (B_NOPE, LANES - B_QK))))
    wuq_t = bf(jnp.pad(w_uq[0].reshape(B_Q_RANK, B_HEADS, B_QK), ((0, 0), (0, 0), (0, LANES - B_QK)))
               .reshape(B_Q_RANK, B_HEADS * LANES).T)
    ukv = w_ukv[0].reshape(B_KV_RANK, B_HEADS, B_NOPE + B_VDIM)
    wuk_p = bf(jnp.pad(ukv[:, :, :B_NOPE], ((0, 0), (0, 0), (0, LANES - B_NOPE)))
               .reshape(B_KV_RANK, B_HEADS * LANES))
    wuv_t = bf(ukv[:, :, B_NOPE:].reshape(B_KV_RANK, B_HEADS * B_VDIM).T)
    wmk = bf(w_mem_kv[0][:, :M_HEADS * M_HEAD_DIM])
    wmv_t = bf(w_mem_kv[0][:, M_HEADS * M_HEAD_DIM:].T)
    wz = bf(jnp.stack([w_za, w_zb, w_zm]))
    wg = bf(w_gate.reshape(D_MODEL, N_BRANCH, D_MODEL).transpose(1, 0, 2))
    wb = bf(w_branch[0])
    wo = bf(w_out[0])

    gn = g_norm[0].reshape(1, D_MODEL)
    gqa_c = g_qn_a[0].reshape(A_HEAD_DIM, 1)
    gka_r = jnp.tile(g_kn_a[0], LANES // A_HEAD_DIM).reshape(1, LANES)
    gcq_r = g_cq[0].reshape(1, B_Q_RANK)
    gckv_r = g_ckv[0].reshape(1, B_KV_RANK)
    gqb_c = jnp.pad(g_qn_b[0], (0, LANES - B_QK)).reshape(LANES, 1)
    gkb_r = jnp.pad(g_kn_b[0], (0, LANES - B_QK)).reshape(1, LANES)
    gqm_c = g_qn_m[0].reshape(M_HEAD_DIM, 1)
    gkm_r = g_kn_m[0].reshape(1, M_HEAD_DIM)
    gmem_r = g_mem[0].reshape(1, D_MODEL)

    inv_a = ROPE_THETA ** (-(jnp.arange(0, A_ROT, 2, dtype=F32) / A_ROT))
    inv_b = ROPE_THETA ** (-(jnp.arange(0, B_ROPE, 2, dtype=F32) / B_ROPE))
    inva_l, mfa, msa = _lane_pattern(inv_a, A_HEAD_DIM, 0)
    invb_l, mfb, msb = _lane_pattern(inv_b, LANES, B_NOPE)
    grp = jnp.asarray(np.where(np.arange(LANES) < A_HEAD_DIM, 1.0, 0.0), F32).reshape(1, LANES)

    x2 = x.reshape(n, d)
    pos_r = positions.reshape(1, n)
    pos_c = positions.reshape(n, 1)
    tile = lambda width: pl.BlockSpec((TM, width), lambda i: (i, 0))
    tile_t = lambda rows: pl.BlockSpec((rows, TM), lambda i: (0, i))
    pos_specs = [pl.BlockSpec((1, TM), lambda i: (0, i)), pl.BlockSpec((TM, 1), lambda i: (i, 0))]

    a_in = [x2, pos_r, pos_c, gn, wqa_t, wqi_t, wwi_t, bf(w_ka), wva_t, wki_p, gqa_c, gka_r,
            inv_a.reshape(-1, 1), inva_l, mfa, msa, grp]
    qat, qib, wt, ka, vat, ki = pl.pallas_call(
        _proj_a_kernel,
        grid=(n // TM,),
        in_specs=[tile(d)] + pos_specs + [_full(a.shape) for a in a_in[3:]],
        out_specs=[tile_t(512), pl.BlockSpec((TM // Q_BLOCK, LANES, IDX_HEADS * LANES), lambda i: (i, 0, 0)),
                   tile_t(IDX_HEADS), tile(512), tile_t(512), tile(LANES)],
        out_shape=[jax.ShapeDtypeStruct((512, n), BF16),
                   jax.ShapeDtypeStruct((n // Q_BLOCK, LANES, IDX_HEADS * LANES), BF16),
                   jax.ShapeDtypeStruct((IDX_HEADS, n), F32),
                   jax.ShapeDtypeStruct((n, 512), BF16),
                   jax.ShapeDtypeStruct((512, n), BF16),
                   jax.ShapeDtypeStruct((n, LANES), BF16)],
        compiler_params=_params(1), name="proj_a",
    )(*a_in)

    b_in = [x2, pos_r, pos_c, gn, bf(w_cq), bf(w_ckv), wkr_p, wqm_t, wuq_t, wuk_p, wuv_t,
            gcq_r, gckv_r, gqb_c, gkb_r, gqm_c, inv_b.reshape(-1, 1), invb_l, mfb, msb]
    qbt, kb, vbt, qmt = pl.pallas_call(
        _proj_b_kernel,
        grid=(n // TM,),
        in_specs=[tile(d)] + pos_specs + [_full(a.shape) for a in b_in[3:]],
        out_specs=[tile_t(B_HEADS * LANES), tile(B_HEADS * LANES), tile_t(512), tile_t(512)],
        out_shape=[jax.ShapeDtypeStruct((B_HEADS * LANES, n), BF16),
                   jax.ShapeDtypeStruct((n, B_HEADS * LANES), BF16),
                   jax.ShapeDtypeStruct((512, n), BF16),
                   jax.ShapeDtypeStruct((512, n), BF16)],
        compiler_params=_params(1), name="proj_b",
    )(*b_in)

    km, vmt = pl.pallas_call(
        _mem_kv_kernel,
        grid=(b,),
        in_specs=[pl.BlockSpec((m_len, d), lambda i: (i, 0)), _full(gmem_r.shape), _full(wmk.shape),
                  _full(wmv_t.shape), _full(gkm_r.shape)],
        out_specs=[pl.BlockSpec((m_len, 512), lambda i: (i, 0)), pl.BlockSpec((512, m_len), lambda i: (0, i))],
        out_shape=[jax.ShapeDtypeStruct((b * m_len, 512), BF16), jax.ShapeDtypeStruct((512, b * m_len), BF16)],
        compiler_params=_params(1), name="mem_kv",
    )(mem.reshape(b * m_len, d), gmem_r, wmk, wmv_t, gkm_r)

    qcol = lambda rows: pl.BlockSpec((rows, Q_BLOCK), lambda bi, qi: (0, bi * nq + qi))
    seq_rows = lambda width: pl.BlockSpec((s, width), lambda bi, qi: (bi, 0))
    seq_cols = lambda rows: pl.BlockSpec((rows, s), lambda bi, qi: (0, bi))
    o_spec = pl.BlockSpec((Q_BLOCK, 512), lambda bi, qi: (bi * nq + qi, 0))
    o_shape = jax.ShapeDtypeStruct((n, 512), F32)
    attn_scratch = [pltpu.VMEM((s, LANES), F32), pltpu.VMEM((s, LANES), F32), pltpu.VMEM((2, s, LANES), BF16),
                    pltpu.VMEM((512, LANES), F32)]

    def dsa(bounded):
        return pl.pallas_call(
            functools.partial(_dsa_kernel, seq=s, bounded=bounded),
            grid=(b, nq),
            in_specs=[qcol(512),
                      pl.BlockSpec((1, LANES, IDX_HEADS * LANES), lambda bi, qi: (bi * nq + qi, 0, 0)),
                      qcol(IDX_HEADS), seq_rows(512), seq_cols(512), seq_rows(LANES)],
            out_specs=o_spec, out_shape=o_shape,
            scratch_shapes=[pltpu.VMEM((s, LANES), I32)] + attn_scratch,
            compiler_params=_params(2), name="dsa" if bounded else "dsa_general")

    def mla(bounded):
        return pl.pallas_call(
            functools.partial(_mla_kernel, seq=s, bounded=bounded),
            grid=(b, nq),
            in_specs=[qcol(B_HEADS * LANES), seq_rows(B_HEADS * LANES), seq_cols(512)],
            out_specs=o_spec, out_shape=o_shape,
            scratch_shapes=attn_scratch,
            compiler_params=_params(2), name="mla" if bounded else "mla_general")

    per_b = s // TM

    def mem_attn(bounded):
        return pl.pallas_call(
            functools.partial(_mem_attn_kernel, mem_len=m_len, bounded=bounded),
            grid=(n // TM,),
            in_specs=[tile_t(512), pl.BlockSpec((m_len, 512), lambda i: (i // per_b, 0)),
                      pl.BlockSpec((512, m_len), lambda i: (0, i // per_b))],
            out_specs=tile(512), out_shape=o_shape,
            scratch_shapes=[pltpu.VMEM((m_len, TM), F32), pltpu.VMEM((2, m_len, TM), BF16),
                            pltpu.VMEM((512, TM), F32)],
            compiler_params=_params(1), name="mem_attn" if bounded else "mem_attn_general")

    def attend(call, gq, gk, dim, *operands):
        bound = dim ** 0.5 * LOG2E * 1.02 * jnp.max(jnp.abs(gq)) * jnp.max(jnp.abs(gk))
        return lax.cond(bound <= BOUNDED_SCORE_LIMIT, call(True), call(False), *operands)

    oa = attend(dsa, g_qn_a, g_kn_a, A_HEAD_DIM, qat, qib, wt, ka, vat, ki)
    ob = attend(mla, g_qn_b, g_kn_b, B_QK, qbt, kb, vbt)
    om = attend(mem_attn, g_qn_m, g_kn_m, M_HEAD_DIM, qmt, km, vmt)

    out = pl.pallas_call(
        _final_kernel,
        grid=(n // TM,),
        in_specs=[tile(d), tile(512), tile(512), tile(512), _full(gn.shape), _full(wz.shape), _full(wg.shape),
                  _full(wb.shape), _full(wo.shape)],
        out_specs=tile(d), out_shape=jax.ShapeDtypeStruct((n, d), x.dtype),
        compiler_params=_params(1), name="final",
    )(x2, oa, ob, om, gn, wz, wg, wb, wo)
    return out.reshape(b, s, d)
```

```python
import functools

import numpy as np
import jax
import jax.numpy as jnp
from jax import lax
from jax.experimental import pallas as pl
from jax.experimental.pallas import tpu as pltpu

F32 = jnp.float32
BF16 = jnp.bfloat16
I32 = jnp.int32

D_MODEL = 1024
ROPE_THETA = 500000.0
EPS = 1e-6
NEG = -1e30
N_BRANCH = 3
BRANCH_WIDTH = 512
A_HEADS = 8
A_HEAD_DIM = 64
A_ROT = A_HEAD_DIM // 4
IDX_HEADS = 8
IDX_DIM = 64
TOPK_MAX = 256
B_HEADS = 8
B_NOPE = 64
B_ROPE = 32
B_VDIM = 64
B_QK = B_NOPE + B_ROPE
B_Q_RANK = 384
B_KV_RANK = 256
M_HEADS = 4
M_HEAD_DIM = 128

LANES = 128
TM = 256
TQ = 256
KC = 256
COUNT_ROWS = 32
VMEM_LIMIT = 56 * 1024 * 1024
INT_MIN = -2 ** 31
LOG2E = 1.4426950408889634
BOUNDED_SCORE_LIMIT = 32.0


def _nt(a, b):
    return lax.dot_general(a, b, (((1,), (1,)), ((), ())), preferred_element_type=F32)


def _mm(a, b):
    return jnp.dot(a, b, preferred_element_type=F32)


def _rms_lanes(xf, g_row, n=None):
    n = xf.shape[-1] if n is None else n
    ms = jnp.sum(xf * xf, axis=-1, keepdims=True) / n
    return xf * lax.rsqrt(ms + EPS) * g_row


def _rms_rows(blk, g_col, n=None):
    n = blk.shape[0] if n is None else n
    ms = jnp.sum(blk * blk, axis=0, keepdims=True) / n
    return blk * lax.rsqrt(ms + EPS) * g_col


def _rope_rows(blk, lo, half, cos_t, sin_t):
    x1 = blk[lo:lo + half]
    x2 = blk[lo + half:lo + 2 * half]
    parts = []
    if lo:
        parts.append(blk[:lo])
    parts += [x1 * cos_t - x2 * sin_t, x2 * cos_t + x1 * sin_t]
    if lo + 2 * half < blk.shape[0]:
        parts.append(blk[lo + 2 * half:])
    return jnp.concatenate(parts, axis=0)


def _rope_lanes(yc, half, cos_l, sin_first, sin_second):
    return (yc * cos_l + pltpu.roll(yc, LANES - half, 1) * sin_first
            + pltpu.roll(yc, half, 1) * sin_second)


def _proj_a_kernel(x_ref, posr_ref, posc_ref, gn_ref, wqa_ref, wqi_ref, wwi_ref, wka_ref, wva_ref,
                   wki_ref, gqa_ref, gka_ref, invc_ref, invr_ref, mfirst_ref, msecond_ref, grp_ref,
                   qat_ref, qib_ref, wt_ref, ka_ref, vat_ref, ki_ref):
    h = _rms_lanes(x_ref[...], gn_ref[...]).astype(BF16)
    ang_t = invc_ref[...] * posr_ref[...].astype(F32)
    cos_t, sin_t = jnp.cos(ang_t), jnp.sin(ang_t)
    half = A_ROT // 2

    qa = _nt(wqa_ref[...], h)
    gq = gqa_ref[...]
    for hh in range(A_HEADS):
        blk = _rms_rows(qa[hh * A_HEAD_DIM:(hh + 1) * A_HEAD_DIM], gq)
        blk = _rope_rows(blk, 0, half, cos_t, sin_t) * (A_HEAD_DIM ** -0.5 * LOG2E)
        qat_ref[hh * A_HEAD_DIM:(hh + 1) * A_HEAD_DIM, :] = blk.astype(BF16)

    qi = _nt(wqi_ref[...], h)
    for hh in range(IDX_HEADS):
        blk = _rope_rows(qi[hh * IDX_DIM:(hh + 1) * IDX_DIM], 0, half, cos_t, sin_t)
        blk = (blk * (IDX_DIM ** -0.5)).astype(BF16)
        for j in range(TM // TQ):
            qib_ref[j, 0:IDX_DIM, hh * TQ:(hh + 1) * TQ] = blk[:, j * TQ:(j + 1) * TQ]
    qib_ref[:, IDX_DIM:, :] = jnp.zeros((TM // TQ, LANES - IDX_DIM, IDX_HEADS * TQ), BF16)

    wt_ref[...] = _nt(wwi_ref[...], h)[0:IDX_HEADS] * (IDX_HEADS ** -0.5)

    vat_ref[...] = _nt(wva_ref[...], h).astype(BF16)

    ang_l = posc_ref[...].astype(F32) * invr_ref[...]
    cos_l, sin_l = jnp.cos(ang_l), jnp.sin(ang_l)
    s_first = sin_l * mfirst_ref[...]
    s_second = sin_l * msecond_ref[...]

    ka = _mm(h, wka_ref[...])
    grp = grp_ref[...]
    gk = gka_ref[...]
    for c in range(A_HEADS * A_HEAD_DIM // LANES):
        kc = ka[:, c * LANES:(c + 1) * LANES]
        sq = kc * kc
        s_lo = jnp.sum(sq * grp, axis=-1, keepdims=True)
        s_hi = jnp.sum(sq, axis=-1, keepdims=True) - s_lo
        ms = jnp.where(grp > 0.5, s_lo, s_hi) / A_HEAD_DIM
        y = kc * lax.rsqrt(ms + EPS) * gk
        ka_ref[:, c * LANES:(c + 1) * LANES] = _rope_lanes(y, half, cos_l, s_first, s_second).astype(BF16)

    ki = _mm(h, wki_ref[...])
    ki_ref[...] = _rope_lanes(ki, half, cos_l, s_first, s_second).astype(BF16)


def _proj_b_kernel(x_ref, posr_ref, posc_ref, gn_ref, wcq_ref, wckv_ref, wkr_ref, wqm_ref, wuq_ref,
                   wuk_ref, wuv_ref, gcq_ref, gckv_ref, gqb_ref, gkb_ref, gqm_ref, invc_ref, invr_ref,
                   mfirst_ref, msecond_ref,
                   qbt_ref, kb_ref, vbt_ref, qmt_ref):
    h = _rms_lanes(x_ref[...], gn_ref[...]).astype(BF16)
    half = B_ROPE // 2
    ang_t = invc_ref[...] * posr_ref[...].astype(F32)
    cos_t, sin_t = jnp.cos(ang_t), jnp.sin(ang_t)
    ang_l = posc_ref[...].astype(F32) * invr_ref[...]
    cos_l, sin_l = jnp.cos(ang_l), jnp.sin(ang_l)
    s_first = sin_l * mfirst_ref[...]
    s_second = sin_l * msecond_ref[...]

    cq = _rms_lanes(_mm(h, wcq_ref[...]), gcq_ref[...]).astype(BF16)
    qb = _nt(wuq_ref[...], cq)
    gq = gqb_ref[...]
    for hh in range(B_HEADS):
        blk = _rms_rows(qb[hh * LANES:(hh + 1) * LANES], gq, n=B_QK)
        blk = _rope_rows(blk, B_NOPE, half, cos_t, sin_t) * (B_QK ** -0.5 * LOG2E)
        qbt_ref[hh * LANES:(hh + 1) * LANES, :] = blk.astype(BF16)

    ckv = _rms_lanes(_mm(h, wckv_ref[...]), gckv_ref[...]).astype(BF16)
    kn = _mm(ckv, wuk_ref[...])
    kr = _mm(h, wkr_ref[...])
    gk = gkb_ref[...]
    for hh in range(B_HEADS):
        kc = kn[:, hh * LANES:(hh + 1) * LANES] + kr
        y = _rms_lanes(kc, gk, n=B_QK)
        kb_ref[:, hh * LANES:(hh + 1) * LANES] = _rope_lanes(y, half, cos_l, s_first, s_second).astype(BF16)
    vbt_ref[...] = _nt(wuv_ref[...], ckv).astype(BF16)

    qm = _nt(wqm_ref[...], h)
    gm = gqm_ref[...]
    for hh in range(M_HEADS):
        blk = _rms_rows(qm[hh * M_HEAD_DIM:(hh + 1) * M_HEAD_DIM], gm) * (M_HEAD_DIM ** -0.5 * LOG2E)
        qmt_ref[hh * M_HEAD_DIM:(hh + 1) * M_HEAD_DIM, :] = blk.astype(BF16)


def _mem_kv_kernel(mem_ref, gmem_ref, wk_ref, wvt_ref, gkm_ref, km_ref, vmt_ref):
    hm = _rms_lanes(mem_ref[...], gmem_ref[...]).astype(BF16)
    k = _mm(hm, wk_ref[...])
    gk = gkm_ref[...]
    for hh in range(M_HEADS):
        kc = _rms_lanes(k[:, hh * M_HEAD_DIM:(hh + 1) * M_HEAD_DIM], gk)
        km_ref[:, hh * M_HEAD_DIM:(hh + 1) * M_HEAD_DIM] = kc.astype(BF16)
    vmt_ref[...] = _nt(wvt_ref[...], hm).astype(BF16)


def _attend(nk, n_heads, dv, q_of, k_of, v_of, bias_of, bounded, s_ref, p_ref, ot_ref):
    nq = ot_ref.shape[1]
    chunks = [slice(c * KC, (c + 1) * KC) for c in range(nk // KC)]

    def scores(hh, q, c):
        s = _mm(k_of(hh, chunks[c]), q)
        b = bias_of(c)
        return s if b is None else s + b

    depth = max(1, min(n_heads - 1, 8 // len(chunks)))
    slots = depth + 1

    def p_rows(hh, sl):
        base = (hh % slots) * nk
        return slice(base + sl.start, base + sl.stop)

    def probabilities(hh):
        q = q_of(hh)
        if bounded:
            l8 = jnp.zeros((8, nq), F32)
            for c in range(len(chunks)):
                p = jnp.exp2(scores(hh, q, c))
                l8 = l8 + p.reshape(KC // 8, 8, nq).sum(axis=0)
                p_ref[p_rows(hh, chunks[c]), :] = p.astype(BF16)
            return jnp.sum(l8, axis=0, keepdims=True)
        m = jnp.full((1, nq), -jnp.inf, F32)
        for c in range(len(chunks)):
            s = scores(hh, q, c)
            s_ref[chunks[c], :] = s
            m = jnp.maximum(m, jnp.max(s, axis=0, keepdims=True))
        l = jnp.zeros((1, nq), F32)
        for c in range(len(chunks)):
            p = jnp.exp2(s_ref[chunks[c], :] - m)
            l = l + jnp.sum(p, axis=0, keepdims=True)
            p_ref[p_rows(hh, chunks[c]), :] = p.astype(BF16)
        return l

    def weighted_values(hh, l):
        o = _mm(v_of(hh, slice(0, nk)), p_ref[p_rows(hh, slice(0, nk)), :])
        ot_ref[hh * dv:(hh + 1) * dv, :] = o / l

    sums = {}
    for step in range(n_heads + depth):
        if step < n_heads:
            sums[step] = probabilities(step)
        if step >= depth:
            weighted_values(step - depth, sums.pop(step - depth))


def _count(score_ref, nk, pred):
    cnt = jnp.zeros((COUNT_ROWS, TQ), I32)
    for c in range(nk // KC):
        hit = pred(score_ref[c * KC:(c + 1) * KC, :])
        cnt = cnt + jnp.where(hit, 1, 0).reshape(KC // COUNT_ROWS, COUNT_ROWS, TQ).sum(axis=0)
    return jnp.sum(cnt, axis=0, keepdims=True)


def _ordered_pattern_to_float(u):
    k = u ^ INT_MIN
    return pltpu.bitcast(k ^ ((k >> 31) & 0x7FFFFFFF), F32)


def _select_topk(nk, q_pos, row, chunks, qib_ref, wt_ref, ki_ref, score_ref, bias_ref):
    for c, sl in enumerate(chunks):
        ki_c = ki_ref[sl, :]
        acc = jnp.zeros((KC, TQ), F32)
        for hh in range(IDX_HEADS):
            d = _mm(ki_c, qib_ref[0, :, hh * TQ:(hh + 1) * TQ])
            acc = acc + jnp.maximum(d, 0.0) * wt_ref[hh:hh + 1, :]
        score_ref[sl, :] = jnp.where(row + c * KC <= q_pos, acc, NEG)

    def step(i, carry):
        t_u, cnt_t = carry
        cand_u = t_u | jnp.left_shift(jnp.int32(1), 31 - i)
        cand = _ordered_pattern_to_float(cand_u)
        cnt = _count(score_ref, nk, lambda x: x >= cand)
        ok = cnt >= TOPK_MAX
        return jnp.where(ok, cand_u, t_u), jnp.where(ok, cnt, cnt_t)

    t_u, cnt_t = lax.fori_loop(0, 32, step, (jnp.zeros((1, TQ), I32), jnp.full((1, TQ), nk, I32)))
    thr = _ordered_pattern_to_float(t_u)
    split_ties = jnp.max(jnp.where(cnt_t > TOPK_MAX, 1, 0)) > 0

    @pl.when(jnp.logical_not(split_ties))
    def _():
        for c, sl in enumerate(chunks):
            sel = score_ref[sl, :] >= thr
            bias_ref[sl, :] = jnp.where(sel & (row + c * KC <= q_pos), 0.0, NEG)

    @pl.when(split_ties)
    def _():
        room = (TOPK_MAX - _count(score_ref, nk, lambda x: x > thr)).astype(F32)
        tri = lax.broadcasted_iota(I32, (KC, KC), 0) >= lax.broadcasted_iota(I32, (KC, KC), 1)
        tri = jnp.where(tri, 1.0, 0.0).astype(BF16)
        running = jnp.zeros((1, TQ), F32)
        for c, sl in enumerate(chunks):
            x = score_ref[sl, :]
            tie = x == thr
            rank = _mm(tri, jnp.where(tie, 1.0, 0.0).astype(BF16)) + running
            running = rank[KC - 1:KC, :]
            sel = (x > thr) | (tie & (rank <= room))
            bias_ref[sl, :] = jnp.where(sel & (row + c * KC <= q_pos), 0.0, NEG)


def _dsa_body(nk, start, bounded, qat_ref, qib_ref, wt_ref, ka_ref, vat_ref, ki_ref, oa_ref,
              score_ref, bias_ref, s_ref, p_ref, ot_ref):
    q_pos = start + lax.broadcasted_iota(I32, (1, TQ), 1)
    row = lax.broadcasted_iota(I32, (KC, TQ), 0)
    chunks = [slice(c * KC, (c + 1) * KC) for c in range(nk // KC)]

    if nk <= TOPK_MAX:
        for c, sl in enumerate(chunks):
            bias_ref[sl, :] = jnp.where(row + c * KC <= q_pos, 0.0, NEG)
    else:
        _select_topk(nk, q_pos, row, chunks, qib_ref, wt_ref, ki_ref, score_ref, bias_ref)

    odd = (lax.broadcasted_iota(I32, (LANES, TQ), 0) >= A_HEAD_DIM)

    def q_of(hh):
        blk = qat_ref[(hh // 2) * LANES:(hh // 2 + 1) * LANES, :]
        keep = odd if hh % 2 else jnp.logical_not(odd)
        return jnp.where(keep, blk, jnp.zeros_like(blk))

    def k_of(hh, sl):
        return ka_ref[sl, (hh // 2) * LANES:(hh // 2 + 1) * LANES]

    def v_of(hh, sl):
        return vat_ref[hh * A_HEAD_DIM:(hh + 1) * A_HEAD_DIM, sl]

    _attend(nk, A_HEADS, A_HEAD_DIM, q_of, k_of, v_of, lambda c: bias_ref[chunks[c], :], bounded,
            s_ref, p_ref, ot_ref)
    oa_ref[...] = ot_ref[...].T


def _dsa_kernel(qat_ref, qib_ref, wt_ref, ka_ref, vat_ref, ki_ref, oa_ref,
                score_ref, bias_ref, s_ref, p_ref, ot_ref, *, seq, bounded):
    qb = pl.program_id(1)
    for cls in range(seq // TQ):
        @pl.when(qb == cls)
        def _():
            _dsa_body(TQ * (cls + 1), cls * TQ, bounded, qat_ref, qib_ref, wt_ref, ka_ref, vat_ref,
                      ki_ref, oa_ref, score_ref, bias_ref, s_ref, p_ref, ot_ref)


def _mla_body(nk, start, bounded, qbt_ref, kb_ref, vbt_ref, ob_ref, bias_ref, s_ref, p_ref, ot_ref):
    last = nk // KC - 1
    q_pos = start + lax.broadcasted_iota(I32, (1, TQ), 1)
    row = lax.broadcasted_iota(I32, (KC, TQ), 0)
    bias_ref[0:KC, :] = jnp.where(row + last * KC <= q_pos, 0.0, NEG)

    def q_of(hh):
        return qbt_ref[hh * LANES:(hh + 1) * LANES, :]

    def k_of(hh, sl):
        return kb_ref[sl, hh * LANES:(hh + 1) * LANES]

    def v_of(hh, sl):
        return vbt_ref[hh * B_VDIM:(hh + 1) * B_VDIM, sl]

    _attend(nk, B_HEADS, B_VDIM, q_of, k_of, v_of, lambda c: bias_ref[0:KC, :] if c == last else None,
            bounded, s_ref, p_ref, ot_ref)
    ob_ref[...] = ot_ref[...].T


def _mla_kernel(qbt_ref, kb_ref, vbt_ref, ob_ref, bias_ref, s_ref, p_ref, ot_ref, *, seq, bounded):
    qb = pl.program_id(1)
    for cls in range(seq // TQ):
        @pl.when(qb == cls)
        def _():
            _mla_body(TQ * (cls + 1), cls * TQ, bounded, qbt_ref, kb_ref, vbt_ref, ob_ref,
                      bias_ref, s_ref, p_ref, ot_ref)


def _mem_attn_kernel(qmt_ref, km_ref, vmt_ref, om_ref, s_ref, p_ref, ot_ref, *, mem_len, bounded):
    def q_of(hh):
        return qmt_ref[hh * M_HEAD_DIM:(hh + 1) * M_HEAD_DIM, :]

    def k_of(hh, sl):
        return km_ref[sl, hh * M_HEAD_DIM:(hh + 1) * M_HEAD_DIM]

    def v_of(hh, sl):
        return vmt_ref[hh * M_HEAD_DIM:(hh + 1) * M_HEAD_DIM, sl]

    _attend(mem_len, M_HEADS, M_HEAD_DIM, q_of, k_of, v_of, lambda c: None, bounded, s_ref, p_ref, ot_ref)
    om_ref[...] = ot_ref[...].T


def _final_kernel(x_ref, oa_ref, ob_ref, om_ref, gn_ref, wz_ref, wg_ref, wb_ref, wo_ref, out_ref):
    x = x_ref[...]
    h = _rms_lanes(x, gn_ref[...]).astype(BF16)
    merged = jnp.zeros((TM, D_MODEL), F32)
    for n, o_ref in enumerate((oa_ref, ob_ref, om_ref)):
        z = _mm(h, wz_ref[n])
        y = (o_ref[...] * (z * jax.nn.sigmoid(z))).astype(BF16)
        branch = _mm(y, wb_ref[n])
        gate = jax.nn.sigmoid(_mm(h, wg_ref[n]))
        merged = merged + gate * branch
    out_ref[...] = x + _mm(merged.astype(BF16), wo_ref[...])


def _full(shape):
    return pl.BlockSpec(shape, lambda *_: (0,) * len(shape))


def _params(n_axes):
    return pltpu.CompilerParams(dimension_semantics=("arbitrary",) * n_axes,
                                vmem_limit_bytes=VMEM_LIMIT)


def _lane_pattern(inv, period, lo):
    half = inv.shape[0]
    j = np.arange(LANES) % period
    first = (j >= lo) & (j < lo + half)
    second = (j >= lo + half) & (j < lo + 2 * half)
    idx = np.where(first, j - lo, np.where(second, j - lo - half, 0))
    inv_l = jnp.where(jnp.asarray(first | second), inv[idx], 0.0).reshape(1, LANES)
    m_first = jnp.asarray(np.where(first, -1.0, 0.0), F32).reshape(1, LANES)
    m_second = jnp.asarray(np.where(second, 1.0, 0.0), F32).reshape(1, LANES)
    return inv_l, m_first, m_second


def kernel(x, mem, positions, g_norm, w_in, g_qn_a, g_kn_a, g_cq, g_ckv, w_uq, w_ukv, g_qn_b, g_kn_b,
           g_mem, w_mem_kv, g_qn_m, g_kn_m, w_branch, w_out):
    b, s, d = x.shape
    m_len = mem.shape[1]
    n = b * s
    nq = s // TQ
    assert d == D_MODEL and s % TQ == 0 and TQ == KC and TM % TQ == 0 and n % TM == 0 and m_len % KC == 0
    assert g_norm.shape[0] == 1, "single-layer block"

    w = w_in[0]
    off = np.cumsum([0, 512, 512, 512, 512, IDX_DIM, IDX_HEADS, BRANCH_WIDTH, B_Q_RANK, B_KV_RANK, B_ROPE,
                     BRANCH_WIDTH, M_HEADS * M_HEAD_DIM, BRANCH_WIDTH, N_BRANCH * D_MODEL])
    seg = [w[:, off[i]:off[i + 1]] for i in range(14)]
    (w_qa, w_ka, w_va, w_qi, w_ki, w_wi, w_za, w_cq, w_ckv, w_kr, w_zb, w_qm, w_zm, w_gate) = seg
    bf = lambda a: a.astype(BF16)
    wqa_t, wqi_t, wva_t, wqm_t = bf(w_qa.T), bf(w_qi.T), bf(w_va.T), bf(w_qm.T)
    wwi_t = bf(jnp.pad(w_wi.T, ((0, 16 - IDX_HEADS), (0, 0))))
    wki_p = bf(jnp.pad(w_ki, ((0, 0), (0, LANES - IDX_DIM))))
    wkr_p = bf(jnp.pad(w_kr, ((0, 0), (B_NOPE, LANES - B_QK))))
    wuq_t = bf(jnp.pad(w_uq[0].reshape(B_Q_RANK, B_HEADS, B_QK), ((0, 0), (0, 0), (0, LANES - B_QK)))
               .reshape(B_Q_RANK, B_HEADS * LANES).T)
    ukv = w_ukv[0].reshape(B_KV_RANK, B_HEADS, B_NOPE + B_VDIM)
    wuk_p = bf(jnp.pad(ukv[:, :, :B_NOPE], ((0, 0), (0, 0), (0, LANES - B_NOPE)))
               .reshape(B_KV_RANK, B_HEADS * LANES))
    wuv_t = bf(ukv[:, :, B_NOPE:].reshape(B_KV_RANK, B_HEADS * B_VDIM).T)
    wmk = bf(w_mem_kv[0][:, :M_HEADS * M_HEAD_DIM])
    wmv_t = bf(w_mem_kv[0][:, M_HEADS * M_HEAD_DIM:].T)
    wz = bf(jnp.stack([w_za, w_zb, w_zm]))
    wg = bf(w_gate.reshape(D_MODEL, N_BRANCH, D_MODEL).transpose(1, 0, 2))
    wb = bf(w_branch[0])
    wo = bf(w_out[0])

    gn = g_norm[0].reshape(1, D_MODEL)
    gqa_c = g_qn_a[0].reshape(A_HEAD_DIM, 1)
    gka_r = jnp.tile(g_kn_a[0], LANES // A_HEAD_DIM).reshape(1, LANES)
    gcq_r = g_cq[0].reshape(1, B_Q_RANK)
    gckv_r = g_ckv[0].reshape(1, B_KV_RANK)
    gqb_c = jnp.pad(g_qn_b[0], (0, LANES - B_QK)).reshape(LANES, 1)
    gkb_r = jnp.pad(g_kn_b[0], (0, LANES - B_QK)).reshape(1, LANES)
    gqm_c = g_qn_m[0].reshape(M_HEAD_DIM, 1)
    gkm_r = g_kn_m[0].reshape(1, M_HEAD_DIM)
    gmem_r = g_mem[0].reshape(1, D_MODEL)

    inv_a = ROPE_THETA ** (-(jnp.arange(0, A_ROT, 2, dtype=F32) / A_ROT))
    inv_b = ROPE_THETA ** (-(jnp.arange(0, B_ROPE, 2, dtype=F32) / B_ROPE))
    inva_l, mfa, msa = _lane_pattern(inv_a, A_HEAD_DIM, 0)
    invb_l, mfb, msb = _lane_pattern(inv_b, LANES, B_NOPE)
    grp = jnp.asarray(np.where(np.arange(LANES) < A_HEAD_DIM, 1.0, 0.0), F32).reshape(1, LANES)

    x2 = x.reshape(n, d)
    pos_r = positions.reshape(1, n)
    pos_c = positions.reshape(n, 1)
    tile = lambda width: pl.BlockSpec((TM, width), lambda i: (i, 0))
    tile_t = lambda rows: pl.BlockSpec((rows, TM), lambda i: (0, i))
    pos_specs = [pl.BlockSpec((1, TM), lambda i: (0, i)), pl.BlockSpec((TM, 1), lambda i: (i, 0))]

    a_in = [x2, pos_r, pos_c, gn, wqa_t, wqi_t, wwi_t, bf(w_ka), wva_t, wki_p, gqa_c, gka_r,
            inv_a.reshape(-1, 1), inva_l, mfa, msa, grp]
    qat, qib, wt, ka, vat, ki = pl.pallas_call(
        _proj_a_kernel,
        grid=(n // TM,),
        in_specs=[tile(d)] + pos_specs + [_full(a.shape) for a in a_in[3:]],
        out_specs=[tile_t(512), pl.BlockSpec((TM // TQ, LANES, IDX_HEADS * TQ), lambda i: (i, 0, 0)),
                   tile_t(IDX_HEADS), tile(512), tile_t(512), tile(LANES)],
        out_shape=[jax.ShapeDtypeStruct((512, n), BF16),
                   jax.ShapeDtypeStruct((n // TQ, LANES, IDX_HEADS * TQ), BF16),
                   jax.ShapeDtypeStruct((IDX_HEADS, n), F32),
                   jax.ShapeDtypeStruct((n, 512), BF16),
                   jax.ShapeDtypeStruct((512, n), BF16),
                   jax.ShapeDtypeStruct((n, LANES), BF16)],
        compiler_params=_params(1), name="proj_a",
    )(*a_in)

    b_in = [x2, pos_r, pos_c, gn, bf(w_cq), bf(w_ckv), wkr_p, wqm_t, wuq_t, wuk_p, wuv_t,
            gcq_r, gckv_r, gqb_c, gkb_r, gqm_c, inv_b.reshape(-1, 1), invb_l, mfb, msb]
    qbt, kb, vbt, qmt = pl.pallas_call(
        _proj_b_kernel,
        grid=(n // TM,),
        in_specs=[tile(d)] + pos_specs + [_full(a.shape) for a in b_in[3:]],
        out_specs=[tile_t(B_HEADS * LANES), tile(B_HEADS * LANES), tile_t(512), tile_t(512)],
        out_shape=[jax.ShapeDtypeStruct((B_HEADS * LANES, n), BF16),
                   jax.ShapeDtypeStruct((n, B_HEADS * LANES), BF16),
                   jax.ShapeDtypeStruct((512, n), BF16),
                   jax.ShapeDtypeStruct((512, n), BF16)],
        compiler_params=_params(1), name="proj_b",
    )(*b_in)

    km, vmt = pl.pallas_call(
        _mem_kv_kernel,
        grid=(b,),
        in_specs=[pl.BlockSpec((m_len, d), lambda i: (i, 0)), _full(gmem_r.shape), _full(wmk.shape),
                  _full(wmv_t.shape), _full(gkm_r.shape)],
        out_specs=[pl.BlockSpec((m_len, 512), lambda i: (i, 0)), pl.BlockSpec((512, m_len), lambda i: (0, i))],
        out_shape=[jax.ShapeDtypeStruct((b * m_len, 512), BF16), jax.ShapeDtypeStruct((512, b * m_len), BF16)],
        compiler_params=_params(1), name="mem_kv",
    )(mem.reshape(b * m_len, d), gmem_r, wmk, wmv_t, gkm_r)

    qcol = lambda rows: pl.BlockSpec((rows, TQ), lambda bi, qi: (0, bi * nq + qi))
    seq_rows = lambda width: pl.BlockSpec((s, width), lambda bi, qi: (bi, 0))
    seq_cols = lambda rows: pl.BlockSpec((rows, s), lambda bi, qi: (0, bi))
    o_spec = pl.BlockSpec((TQ, 512), lambda bi, qi: (bi * nq + qi, 0))
    o_shape = jax.ShapeDtypeStruct((n, 512), F32)
    attn_scratch = [pltpu.VMEM((s, TQ), F32), pltpu.VMEM((s, TQ), F32), pltpu.VMEM((2 * s, TQ), BF16),
                    pltpu.VMEM((512, TQ), F32)]

    def dsa(bounded):
        return pl.pallas_call(
            functools.partial(_dsa_kernel, seq=s, bounded=bounded),
            grid=(b, nq),
            in_specs=[qcol(512),
                      pl.BlockSpec((1, LANES, IDX_HEADS * TQ), lambda bi, qi: (bi * nq + qi, 0, 0)),
                      qcol(IDX_HEADS), seq_rows(512), seq_cols(512), seq_rows(LANES)],
            out_specs=o_spec, out_shape=o_shape,
            scratch_shapes=[pltpu.VMEM((s, TQ), F32)] + attn_scratch,
            compiler_params=_params(2), name="dsa" if bounded else "dsa_general")

    def mla(bounded):
        return pl.pallas_call(
            functools.partial(_mla_kernel, seq=s, bounded=bounded),
            grid=(b, nq),
            in_specs=[qcol(B_HEADS * LANES), seq_rows(B_HEADS * LANES), seq_cols(512)],
            out_specs=o_spec, out_shape=o_shape,
            scratch_shapes=attn_scratch,
            compiler_params=_params(2), name="mla" if bounded else "mla_general")

    per_b = s // TM

    def mem_attn(bounded):
        return pl.pallas_call(
            functools.partial(_mem_attn_kernel, mem_len=m_len, bounded=bounded),
            grid=(n // TM,),
            in_specs=[tile_t(512), pl.BlockSpec((m_len, 512), lambda i: (i // per_b, 0)),
                      pl.BlockSpec((512, m_len), lambda i: (0, i // per_b))],
            out_specs=tile(512), out_shape=o_shape,
            scratch_shapes=[pltpu.VMEM((m_len, TM), F32), pltpu.VMEM((M_HEADS * m_len, TM), BF16),
                            pltpu.VMEM((512, TM), F32)],
            compiler_params=_params(1), name="mem_attn" if bounded else "mem_attn_general")

    def attend(call, gq, gk, dim, *operands):
        bound = dim ** 0.5 * LOG2E * 1.02 * jnp.max(jnp.abs(gq)) * jnp.max(jnp.abs(gk))
        return lax.cond(bound <= BOUNDED_SCORE_LIMIT, call(True), call(False), *operands)

    oa = attend(dsa, g_qn_a, g_kn_a, A_HEAD_DIM, qat, qib, wt, ka, vat, ki)
    ob = attend(mla, g_qn_b, g_kn_b, B_QK, qbt, kb, vbt)
    om = attend(mem_attn, g_qn_m, g_kn_m, M_HEAD_DIM, qmt, km, vmt)

    out = pl.pallas_call(
        _final_kernel,
        grid=(n // TM,),
        in_specs=[tile(d), tile(512), tile(512), tile(512), _full(gn.shape), _full(wz.shape), _full(wg.shape),
                  _full(wb.shape), _full(wo.shape)],
        out_specs=tile(d), out_shape=jax.ShapeDtypeStruct((n, d), x.dtype),
        compiler_params=_params(1), name="final",
    )(x2, oa, ob, om, gn, wz, wg, wb, wo)
    return out.reshape(b, s, d)
```

```python
import functools

import numpy as np
import jax
import jax.numpy as jnp
from jax import lax
from jax.experimental import pallas as pl
from jax.experimental.pallas import tpu as pltpu

F32 = jnp.float32
BF16 = jnp.bfloat16
I32 = jnp.int32

D_MODEL = 1024
ROPE_THETA = 500000.0
EPS = 1e-6
NEG = -1e30
N_BRANCH = 3
BRANCH_WIDTH = 512
A_HEADS = 8
A_HEAD_DIM = 64
A_ROT = A_HEAD_DIM // 4
IDX_HEADS = 8
IDX_DIM = 64
TOPK_MAX = 256
B_HEADS = 8
B_NOPE = 64
B_ROPE = 32
B_VDIM = 64
B_QK = B_NOPE + B_ROPE
B_Q_RANK = 384
B_KV_RANK = 256
M_HEADS = 4
M_HEAD_DIM = 128

LANES = 128
TM = 256
TQ = 256
KC = 256
COUNT_ROWS = 32
VMEM_LIMIT = 56 * 1024 * 1024
INT_MIN = -2 ** 31
LOG2E = 1.4426950408889634
BOUNDED_SCORE_LIMIT = 32.0


def _nt(a, b):
    return lax.dot_general(a, b, (((1,), (1,)), ((), ())), preferred_element_type=F32)


def _mm(a, b):
    return jnp.dot(a, b, preferred_element_type=F32)


def _rms_lanes(xf, g_row, n=None):
    n = xf.shape[-1] if n is None else n
    ms = jnp.sum(xf * xf, axis=-1, keepdims=True) / n
    return xf * lax.rsqrt(ms + EPS) * g_row


def _rms_rows(blk, g_col, n=None):
    n = blk.shape[0] if n is None else n
    ms = jnp.sum(blk * blk, axis=0, keepdims=True) / n
    return blk * lax.rsqrt(ms + EPS) * g_col


def _rope_rows(blk, lo, half, cos_t, sin_t):
    x1 = blk[lo:lo + half]
    x2 = blk[lo + half:lo + 2 * half]
    parts = []
    if lo:
        parts.append(blk[:lo])
    parts += [x1 * cos_t - x2 * sin_t, x2 * cos_t + x1 * sin_t]
    if lo + 2 * half < blk.shape[0]:
        parts.append(blk[lo + 2 * half:])
    return jnp.concatenate(parts, axis=0)


def _rope_lanes(yc, half, cos_l, sin_first, sin_second):
    return (yc * cos_l + pltpu.roll(yc, LANES - half, 1) * sin_first
            + pltpu.roll(yc, half, 1) * sin_second)


def _bf16_terms(x, n_terms):
    terms = []
    for _ in range(n_terms):
        t = x.astype(BF16).astype(F32)
        terms.append(t)
        x = x - t
    return terms


def _lane_tables(cos_t, sin_t, expand_ref, nonrot_ref):
    rows = _bf16_terms(cos_t, 3) + _bf16_terms(sin_t, 3)
    pad = jnp.zeros((LANES - 6 * cos_t.shape[0], cos_t.shape[1]), F32)
    stack = jnp.concatenate(rows + [pad], axis=0)
    tab = _mm(stack.T.astype(BF16), expand_ref[...])
    return tab[:, 0:LANES] + nonrot_ref[...], tab[:, LANES:2 * LANES], tab[:, 2 * LANES:3 * LANES]


def _group_mean_square(yc, ones_ref, n):
    hi, lo = _bf16_terms(yc * yc, 2)
    both = jnp.concatenate([hi.astype(BF16), lo.astype(BF16)], axis=1)
    return _mm(both, ones_ref[...]) / n


def _proj_a_kernel(x_ref, posr_ref, gn_ref, wqa_ref, wqi_ref, wwi_ref, wka_ref, wva_ref,
                   wki_ref, gqa_ref, gka_ref, invc_ref, expand_ref, nonrot_ref, grp_ref,
                   qat_ref, qib_ref, wt_ref, ka_ref, vat_ref, ki_ref):
    h = _rms_lanes(x_ref[...], gn_ref[...]).astype(BF16)
    ang_t = invc_ref[...] * posr_ref[...].astype(F32)
    cos_t, sin_t = jnp.cos(ang_t), jnp.sin(ang_t)
    half = A_ROT // 2

    qa = _nt(wqa_ref[...], h)
    gq = gqa_ref[...]
    for hh in range(A_HEADS):
        blk = _rms_rows(qa[hh * A_HEAD_DIM:(hh + 1) * A_HEAD_DIM], gq)
        blk = _rope_rows(blk, 0, half, cos_t, sin_t) * (A_HEAD_DIM ** -0.5 * LOG2E)
        qat_ref[hh * A_HEAD_DIM:(hh + 1) * A_HEAD_DIM, :] = blk.astype(BF16)

    qi = _nt(wqi_ref[...], h)
    for hh in range(IDX_HEADS):
        blk = _rope_rows(qi[hh * IDX_DIM:(hh + 1) * IDX_DIM], 0, half, cos_t, sin_t)
        blk = (blk * (IDX_DIM ** -0.5)).astype(BF16)
        for j in range(TM // TQ):
            qib_ref[j, 0:IDX_DIM, hh * TQ:(hh + 1) * TQ] = blk[:, j * TQ:(j + 1) * TQ]
    qib_ref[:, IDX_DIM:, :] = jnp.zeros((TM // TQ, LANES - IDX_DIM, IDX_HEADS * TQ), BF16)

    wt_ref[...] = _nt(wwi_ref[...], h)[0:IDX_HEADS] * (IDX_HEADS ** -0.5)

    vat_ref[...] = _nt(wva_ref[...], h).astype(BF16)

    cos_l, s_first, s_second = _lane_tables(cos_t, sin_t, expand_ref, nonrot_ref)

    ka = _mm(h, wka_ref[...])
    gk = gka_ref[...]
    for c in range(A_HEADS * A_HEAD_DIM // LANES):
        kc = ka[:, c * LANES:(c + 1) * LANES]
        ms = _group_mean_square(kc, grp_ref, A_HEAD_DIM)
        y = kc * lax.rsqrt(ms + EPS) * gk
        ka_ref[:, c * LANES:(c + 1) * LANES] = _rope_lanes(y, half, cos_l, s_first, s_second).astype(BF16)

    ki = _mm(h, wki_ref[...])
    ki_ref[...] = _rope_lanes(ki, half, cos_l, s_first, s_second).astype(BF16)


def _proj_b_kernel(x_ref, posr_ref, gn_ref, wcq_ref, wckv_ref, wkr_ref, wqm_ref, wuq_ref,
                   wuk_ref, wuv_ref, gcq_ref, gckv_ref, gqb_ref, gkb_ref, gqm_ref, invc_ref, expand_ref,
                   nonrot_ref, grp_ref,
                   qbt_ref, kb_ref, vbt_ref, qmt_ref):
    h = _rms_lanes(x_ref[...], gn_ref[...]).astype(BF16)
    half = B_ROPE // 2
    ang_t = invc_ref[...] * posr_ref[...].astype(F32)
    cos_t, sin_t = jnp.cos(ang_t), jnp.sin(ang_t)
    cos_l, s_first, s_second = _lane_tables(cos_t, sin_t, expand_ref, nonrot_ref)

    cq = _rms_lanes(_mm(h, wcq_ref[...]), gcq_ref[...]).astype(BF16)
    qb = _nt(wuq_ref[...], cq)
    gq = gqb_ref[...]
    for hh in range(B_HEADS):
        blk = _rms_rows(qb[hh * LANES:(hh + 1) * LANES], gq, n=B_QK)
        blk = _rope_rows(blk, B_NOPE, half, cos_t, sin_t) * (B_QK ** -0.5 * LOG2E)
        qbt_ref[hh * LANES:(hh + 1) * LANES, :] = blk.astype(BF16)

    ckv = _rms_lanes(_mm(h, wckv_ref[...]), gckv_ref[...]).astype(BF16)
    kn = _mm(ckv, wuk_ref[...])
    kr = _mm(h, wkr_ref[...])
    gk = gkb_ref[...]
    for hh in range(B_HEADS):
        kc = kn[:, hh * LANES:(hh + 1) * LANES] + kr
        y = kc * lax.rsqrt(_group_mean_square(kc, grp_ref, B_QK) + EPS) * gk
        kb_ref[:, hh * LANES:(hh + 1) * LANES] = _rope_lanes(y, half, cos_l, s_first, s_second).astype(BF16)
    vbt_ref[...] = _nt(wuv_ref[...], ckv).astype(BF16)

    qm = _nt(wqm_ref[...], h)
    gm = gqm_ref[...]
    for hh in range(M_HEADS):
        blk = _rms_rows(qm[hh * M_HEAD_DIM:(hh + 1) * M_HEAD_DIM], gm) * (M_HEAD_DIM ** -0.5 * LOG2E)
        qmt_ref[hh * M_HEAD_DIM:(hh + 1) * M_HEAD_DIM, :] = blk.astype(BF16)


def _mem_kv_kernel(mem_ref, gmem_ref, wk_ref, wvt_ref, gkm_ref, km_ref, vmt_ref):
    hm = _rms_lanes(mem_ref[...], gmem_ref[...]).astype(BF16)
    k = _mm(hm, wk_ref[...])
    gk = gkm_ref[...]
    for hh in range(M_HEADS):
        kc = _rms_lanes(k[:, hh * M_HEAD_DIM:(hh + 1) * M_HEAD_DIM], gk)
        km_ref[:, hh * M_HEAD_DIM:(hh + 1) * M_HEAD_DIM] = kc.astype(BF16)
    vmt_ref[...] = _nt(wvt_ref[...], hm).astype(BF16)


def _attend(nk, n_heads, dv, q_of, k_of, v_of, bias_of, bounded, s_ref, p_ref, ot_ref):
    nq = ot_ref.shape[1]
    chunks = [slice(c * KC, (c + 1) * KC) for c in range(nk // KC)]

    def scores(hh, q, c):
        s = _mm(k_of(hh, chunks[c]), q)
        b = bias_of(c)
        return s if b is None else s + b

    depth = max(1, min(n_heads - 1, 8 // len(chunks)))
    slots = depth + 1

    def p_rows(hh, sl):
        base = (hh % slots) * nk
        return slice(base + sl.start, base + sl.stop)

    def probabilities(hh):
        q = q_of(hh)
        if bounded:
            l8 = jnp.zeros((8, nq), F32)
            for c in range(len(chunks)):
                p = jnp.exp2(scores(hh, q, c))
                l8 = l8 + p.reshape(KC // 8, 8, nq).sum(axis=0)
                p_ref[p_rows(hh, chunks[c]), :] = p.astype(BF16)
            return jnp.sum(l8, axis=0, keepdims=True)
        m = jnp.full((1, nq), -jnp.inf, F32)
        for c in range(len(chunks)):
            s = scores(hh, q, c)
            s_ref[chunks[c], :] = s
            m = jnp.maximum(m, jnp.max(s, axis=0, keepdims=True))
        l = jnp.zeros((1, nq), F32)
        for c in range(len(chunks)):
            p = jnp.exp2(s_ref[chunks[c], :] - m)
            l = l + jnp.sum(p, axis=0, keepdims=True)
            p_ref[p_rows(hh, chunks[c]), :] = p.astype(BF16)
        return l

    def weighted_values(hh, l):
        o = _mm(v_of(hh, slice(0, nk)), p_ref[p_rows(hh, slice(0, nk)), :])
        ot_ref[hh * dv:(hh + 1) * dv, :] = o / l

    sums = {}
    for step in range(n_heads + depth):
        if step < n_heads:
            sums[step] = probabilities(step)
        if step >= depth:
            weighted_values(step - depth, sums.pop(step - depth))


def _count(score_ref, nk, pred):
    cnt = jnp.zeros((COUNT_ROWS, TQ), I32)
    for c in range(nk // KC):
        hit = pred(score_ref[c * KC:(c + 1) * KC, :])
        cnt = cnt + jnp.where(hit, 1, 0).reshape(KC // COUNT_ROWS, COUNT_ROWS, TQ).sum(axis=0)
    return jnp.sum(cnt, axis=0, keepdims=True)


def _ordered_pattern_to_float(u):
    k = u ^ INT_MIN
    return pltpu.bitcast(k ^ ((k >> 31) & 0x7FFFFFFF), F32)


def _select_topk(nk, q_pos, row, chunks, qib_ref, wt_ref, ki_ref, score_ref, bias_ref):
    for c, sl in enumerate(chunks):
        ki_c = ki_ref[sl, :]
        acc = jnp.zeros((KC, TQ), F32)
        for hh in range(IDX_HEADS):
            d = _mm(ki_c, qib_ref[0, :, hh * TQ:(hh + 1) * TQ])
            acc = acc + jnp.maximum(d, 0.0) * wt_ref[hh:hh + 1, :]
        score_ref[sl, :] = jnp.where(row + c * KC <= q_pos, acc, NEG)

    def step(i, carry):
        t_u, cnt_t = carry
        cand_u = t_u | jnp.left_shift(jnp.int32(1), 31 - i)
        cand = _ordered_pattern_to_float(cand_u)
        cnt = _count(score_ref, nk, lambda x: x >= cand)
        ok = cnt >= TOPK_MAX
        return jnp.where(ok, cand_u, t_u), jnp.where(ok, cnt, cnt_t)

    t_u, cnt_t = lax.fori_loop(0, 32, step, (jnp.zeros((1, TQ), I32), jnp.full((1, TQ), nk, I32)))
    thr = _ordered_pattern_to_float(t_u)
    split_ties = jnp.max(jnp.where(cnt_t > TOPK_MAX, 1, 0)) > 0

    @pl.when(jnp.logical_not(split_ties))
    def _():
        for c, sl in enumerate(chunks):
            sel = score_ref[sl, :] >= thr
            bias_ref[sl, :] = jnp.where(sel & (row + c * KC <= q_pos), 0.0, NEG)

    @pl.when(split_ties)
    def _():
        room = (TOPK_MAX - _count(score_ref, nk, lambda x: x > thr)).astype(F32)
        tri = lax.broadcasted_iota(I32, (KC, KC), 0) >= lax.broadcasted_iota(I32, (KC, KC), 1)
        tri = jnp.where(tri, 1.0, 0.0).astype(BF16)
        running = jnp.zeros((1, TQ), F32)
        for c, sl in enumerate(chunks):
            x = score_ref[sl, :]
            tie = x == thr
            rank = _mm(tri, jnp.where(tie, 1.0, 0.0).astype(BF16)) + running
            running = rank[KC - 1:KC, :]
            sel = (x > thr) | (tie & (rank <= room))
            bias_ref[sl, :] = jnp.where(sel & (row + c * KC <= q_pos), 0.0, NEG)


def _dsa_body(nk, start, bounded, qat_ref, qib_ref, wt_ref, ka_ref, vat_ref, ki_ref, oa_ref,
              score_ref, bias_ref, s_ref, p_ref, ot_ref):
    q_pos = start + lax.broadcasted_iota(I32, (1, TQ), 1)
    row = lax.broadcasted_iota(I32, (KC, TQ), 0)
    chunks = [slice(c * KC, (c + 1) * KC) for c in range(nk // KC)]

    if nk <= TOPK_MAX:
        for c, sl in enumerate(chunks):
            bias_ref[sl, :] = jnp.where(row + c * KC <= q_pos, 0.0, NEG)
    else:
        _select_topk(nk, q_pos, row, chunks, qib_ref, wt_ref, ki_ref, score_ref, bias_ref)

    odd = (lax.broadcasted_iota(I32, (LANES, TQ), 0) >= A_HEAD_DIM)

    def q_of(hh):
        blk = qat_ref[(hh // 2) * LANES:(hh // 2 + 1) * LANES, :]
        keep = odd if hh % 2 else jnp.logical_not(odd)
        return jnp.where(keep, blk, jnp.zeros_like(blk))

    def k_of(hh, sl):
        return ka_ref[sl, (hh // 2) * LANES:(hh // 2 + 1) * LANES]

    def v_of(hh, sl):
        return vat_ref[hh * A_HEAD_DIM:(hh + 1) * A_HEAD_DIM, sl]

    _attend(nk, A_HEADS, A_HEAD_DIM, q_of, k_of, v_of, lambda c: bias_ref[chunks[c], :], bounded,
            s_ref, p_ref, ot_ref)
    oa_ref[...] = ot_ref[...].T


def _dsa_kernel(qat_ref, qib_ref, wt_ref, ka_ref, vat_ref, ki_ref, oa_ref,
                score_ref, bias_ref, s_ref, p_ref, ot_ref, *, seq, bounded):
    qb = pl.program_id(1)
    for cls in range(seq // TQ):
        @pl.when(qb == cls)
        def _():
            _dsa_body(TQ * (cls + 1), cls * TQ, bounded, qat_ref, qib_ref, wt_ref, ka_ref, vat_ref,
                      ki_ref, oa_ref, score_ref, bias_ref, s_ref, p_ref, ot_ref)


def _mla_body(nk, start, bounded, qbt_ref, kb_ref, vbt_ref, ob_ref, bias_ref, s_ref, p_ref, ot_ref):
    last = nk // KC - 1
    q_pos = start + lax.broadcasted_iota(I32, (1, TQ), 1)
    row = lax.broadcasted_iota(I32, (KC, TQ), 0)
    bias_ref[0:KC, :] = jnp.where(row + last * KC <= q_pos, 0.0, NEG)

    def q_of(hh):
        return qbt_ref[hh * LANES:(hh + 1) * LANES, :]

    def k_of(hh, sl):
        return kb_ref[sl, hh * LANES:(hh + 1) * LANES]

    def v_of(hh, sl):
        return vbt_ref[hh * B_VDIM:(hh + 1) * B_VDIM, sl]

    _attend(nk, B_HEADS, B_VDIM, q_of, k_of, v_of, lambda c: bias_ref[0:KC, :] if c == last else None,
            bounded, s_ref, p_ref, ot_ref)
    ob_ref[...] = ot_ref[...].T


def _mla_kernel(qbt_ref, kb_ref, vbt_ref, ob_ref, bias_ref, s_ref, p_ref, ot_ref, *, seq, bounded):
    qb = pl.program_id(1)
    for cls in range(seq // TQ):
        @pl.when(qb == cls)
        def _():
            _mla_body(TQ * (cls + 1), cls * TQ, bounded, qbt_ref, kb_ref, vbt_ref, ob_ref,
                      bias_ref, s_ref, p_ref, ot_ref)


def _mem_attn_kernel(qmt_ref, km_ref, vmt_ref, om_ref, s_ref, p_ref, ot_ref, *, mem_len, bounded):
    def q_of(hh):
        return qmt_ref[hh * M_HEAD_DIM:(hh + 1) * M_HEAD_DIM, :]

    def k_of(hh, sl):
        return km_ref[sl, hh * M_HEAD_DIM:(hh + 1) * M_HEAD_DIM]

    def v_of(hh, sl):
        return vmt_ref[hh * M_HEAD_DIM:(hh + 1) * M_HEAD_DIM, sl]

    _attend(mem_len, M_HEADS, M_HEAD_DIM, q_of, k_of, v_of, lambda c: None, bounded, s_ref, p_ref, ot_ref)
    om_ref[...] = ot_ref[...].T


def _final_kernel(x_ref, oa_ref, ob_ref, om_ref, gn_ref, wz_ref, wg_ref, wb_ref, wo_ref, out_ref):
    x = x_ref[...]
    h = _rms_lanes(x, gn_ref[...]).astype(BF16)
    merged = jnp.zeros((TM, D_MODEL), F32)
    for n, o_ref in enumerate((oa_ref, ob_ref, om_ref)):
        z = _mm(h, wz_ref[n])
        y = (o_ref[...] * (z * jax.nn.sigmoid(z))).astype(BF16)
        branch = _mm(y, wb_ref[n])
        gate = jax.nn.sigmoid(_mm(h, wg_ref[n]))
        merged = merged + gate * branch
    out_ref[...] = x + _mm(merged.astype(BF16), wo_ref[...])


def _full(shape):
    return pl.BlockSpec(shape, lambda *_: (0,) * len(shape))


def _params(n_axes):
    return pltpu.CompilerParams(dimension_semantics=("arbitrary",) * n_axes,
                                vmem_limit_bytes=VMEM_LIMIT)


def _rope_expansion(half, period, lo):
    j = np.arange(LANES) % period
    first = (j >= lo) & (j < lo + half)
    second = (j >= lo + half) & (j < lo + 2 * half)
    freq = np.where(first, j - lo, j - lo - half)
    mat = np.zeros((LANES, 3 * LANES), np.float32)
    for lane in np.nonzero(first | second)[0]:
        for term in range(3):
            mat[term * half + freq[lane], lane] = 1.0
            col = LANES + lane if first[lane] else 2 * LANES + lane
            mat[(3 + term) * half + freq[lane], col] = -1.0 if first[lane] else 1.0
    nonrot = np.where(first | second, 0.0, 1.0).astype(np.float32).reshape(1, LANES)
    return jnp.asarray(mat, BF16), jnp.asarray(nonrot)


def _group_matrix(group):
    g = np.arange(LANES) // group
    member = (g[:, None] == g[None, :]).astype(np.float32)
    return jnp.asarray(np.concatenate([member, member], axis=0), BF16)


def kernel(x, mem, positions, g_norm, w_in, g_qn_a, g_kn_a, g_cq, g_ckv, w_uq, w_ukv, g_qn_b, g_kn_b,
           g_mem, w_mem_kv, g_qn_m, g_kn_m, w_branch, w_out):
    b, s, d = x.shape
    m_len = mem.shape[1]
    n = b * s
    nq = s // TQ
    assert d == D_MODEL and s % TQ == 0 and TQ == KC and TM % TQ == 0 and n % TM == 0 and m_len % KC == 0
    assert g_norm.shape[0] == 1, "single-layer block"

    w = w_in[0]
    off = np.cumsum([0, 512, 512, 512, 512, IDX_DIM, IDX_HEADS, BRANCH_WIDTH, B_Q_RANK, B_KV_RANK, B_ROPE,
                     BRANCH_WIDTH, M_HEADS * M_HEAD_DIM, BRANCH_WIDTH, N_BRANCH * D_MODEL])
    seg = [w[:, off[i]:off[i + 1]] for i in range(14)]
    (w_qa, w_ka, w_va, w_qi, w_ki, w_wi, w_za, w_cq, w_ckv, w_kr, w_zb, w_qm, w_zm, w_gate) = seg
    bf = lambda a: a.astype(BF16)
    wqa_t, wqi_t, wva_t, wqm_t = bf(w_qa.T), bf(w_qi.T), bf(w_va.T), bf(w_qm.T)
    wwi_t = bf(jnp.pad(w_wi.T, ((0, 16 - IDX_HEADS), (0, 0))))
    wki_p = bf(jnp.pad(w_ki, ((0, 0), (0, LANES - IDX_DIM))))
    wkr_p = bf(jnp.pad(w_kr, ((0, 0), (B_NOPE, LANES - B_QK))))
    wuq_t = bf(jnp.pad(w_uq[0].reshape(B_Q_RANK, B_HEADS, B_QK), ((0, 0), (0, 0), (0, LANES - B_QK)))
               .reshape(B_Q_RANK, B_HEADS * LANES).T)
    ukv = w_ukv[0].reshape(B_KV_RANK, B_HEADS, B_NOPE + B_VDIM)
    wuk_p = bf(jnp.pad(ukv[:, :, :B_NOPE], ((0, 0), (0, 0), (0, LANES - B_NOPE)))
               .reshape(B_KV_RANK, B_HEADS * LANES))
    wuv_t = bf(ukv[:, :, B_NOPE:].reshape(B_KV_RANK, B_HEADS * B_VDIM).T)
    wmk = bf(w_mem_kv[0][:, :M_HEADS * M_HEAD_DIM])
    wmv_t = bf(w_mem_kv[0][:, M_HEADS * M_HEAD_DIM:].T)
    wz = bf(jnp.stack([w_za, w_zb, w_zm]))
    wg = bf(w_gate.reshape(D_MODEL, N_BRANCH, D_MODEL).transpose(1, 0, 2))
    wb = bf(w_branch[0])
    wo = bf(w_out[0])

    gn = g_norm[0].reshape(1, D_MODEL)
    gqa_c = g_qn_a[0].reshape(A_HEAD_DIM, 1)
    gka_r = jnp.tile(g_kn_a[0], LANES // A_HEAD_DIM).reshape(1, LANES)
    gcq_r = g_cq[0].reshape(1, B_Q_RANK)
    gckv_r = g_ckv[0].reshape(1, B_KV_RANK)
    gqb_c = jnp.pad(g_qn_b[0], (0, LANES - B_QK)).reshape(LANES, 1)
    gkb_r = jnp.pad(g_kn_b[0], (0, LANES - B_QK)).reshape(1, LANES)
    gqm_c = g_qn_m[0].reshape(M_HEAD_DIM, 1)
    gkm_r = g_kn_m[0].reshape(1, M_HEAD_DIM)
    gmem_r = g_mem[0].reshape(1, D_MODEL)

    inv_a = ROPE_THETA ** (-(jnp.arange(0, A_ROT, 2, dtype=F32) / A_ROT))
    inv_b = ROPE_THETA ** (-(jnp.arange(0, B_ROPE, 2, dtype=F32) / B_ROPE))
    expand_a, nonrot_a = _rope_expansion(A_ROT // 2, A_HEAD_DIM, 0)
    expand_b, nonrot_b = _rope_expansion(B_ROPE // 2, LANES, B_NOPE)

    x2 = x.reshape(n, d)
    pos_r = positions.reshape(1, n)
    tile = lambda width: pl.BlockSpec((TM, width), lambda i: (i, 0))
    tile_t = lambda rows: pl.BlockSpec((rows, TM), lambda i: (0, i))
    pos_spec = pl.BlockSpec((1, TM), lambda i: (0, i))

    a_in = [x2, pos_r, gn, wqa_t, wqi_t, wwi_t, bf(w_ka), wva_t, wki_p, gqa_c, gka_r,
            inv_a.reshape(-1, 1), expand_a, nonrot_a, _group_matrix(A_HEAD_DIM)]
    qat, qib, wt, ka, vat, ki = pl.pallas_call(
        _proj_a_kernel,
        grid=(n // TM,),
        in_specs=[tile(d), pos_spec] + [_full(a.shape) for a in a_in[2:]],
        out_specs=[tile_t(512), pl.BlockSpec((TM // TQ, LANES, IDX_HEADS * TQ), lambda i: (i, 0, 0)),
                   tile_t(IDX_HEADS), tile(512), tile_t(512), tile(LANES)],
        out_shape=[jax.ShapeDtypeStruct((512, n), BF16),
                   jax.ShapeDtypeStruct((n // TQ, LANES, IDX_HEADS * TQ), BF16),
                   jax.ShapeDtypeStruct((IDX_HEADS, n), F32),
                   jax.ShapeDtypeStruct((n, 512), BF16),
                   jax.ShapeDtypeStruct((512, n), BF16),
                   jax.ShapeDtypeStruct((n, LANES), BF16)],
        compiler_params=_params(1), name="proj_a",
    )(*a_in)

    b_in = [x2, pos_r, gn, bf(w_cq), bf(w_ckv), wkr_p, wqm_t, wuq_t, wuk_p, wuv_t,
            gcq_r, gckv_r, gqb_c, gkb_r, gqm_c, inv_b.reshape(-1, 1), expand_b, nonrot_b,
            _group_matrix(LANES)]
    qbt, kb, vbt, qmt = pl.pallas_call(
        _proj_b_kernel,
        grid=(n // TM,),
        in_specs=[tile(d), pos_spec] + [_full(a.shape) for a in b_in[2:]],
        out_specs=[tile_t(B_HEADS * LANES), tile(B_HEADS * LANES), tile_t(512), tile_t(512)],
        out_shape=[jax.ShapeDtypeStruct((B_HEADS * LANES, n), BF16),
                   jax.ShapeDtypeStruct((n, B_HEADS * LANES), BF16),
                   jax.ShapeDtypeStruct((512, n), BF16),
                   jax.ShapeDtypeStruct((512, n), BF16)],
        compiler_params=_params(1), name="proj_b",
    )(*b_in)

    km, vmt = pl.pallas_call(
        _mem_kv_kernel,
        grid=(b,),
        in_specs=[pl.BlockSpec((m_len, d), lambda i: (i, 0)), _full(gmem_r.shape), _full(wmk.shape),
                  _full(wmv_t.shape), _full(gkm_r.shape)],
        out_specs=[pl.BlockSpec((m_len, 512), lambda i: (i, 0)), pl.BlockSpec((512, m_len), lambda i: (0, i))],
        out_shape=[jax.ShapeDtypeStruct((b * m_len, 512), BF16), jax.ShapeDtypeStruct((512, b * m_len), BF16)],
        compiler_params=_params(1), name="mem_kv",
    )(mem.reshape(b * m_len, d), gmem_r, wmk, wmv_t, gkm_r)

    qcol = lambda rows: pl.BlockSpec((rows, TQ), lambda bi, qi: (0, bi * nq + qi))
    seq_rows = lambda width: pl.BlockSpec((s, width), lambda bi, qi: (bi, 0))
    seq_cols = lambda rows: pl.BlockSpec((rows, s), lambda bi, qi: (0, bi))
    o_spec = pl.BlockSpec((TQ, 512), lambda bi, qi: (bi * nq + qi, 0))
    o_shape = jax.ShapeDtypeStruct((n, 512), F32)
    attn_scratch = [pltpu.VMEM((s, TQ), F32), pltpu.VMEM((s, TQ), F32), pltpu.VMEM((2 * s, TQ), BF16),
                    pltpu.VMEM((512, TQ), F32)]

    def dsa(bounded):
        return pl.pallas_call(
            functools.partial(_dsa_kernel, seq=s, bounded=bounded),
            grid=(b, nq),
            in_specs=[qcol(512),
                      pl.BlockSpec((1, LANES, IDX_HEADS * TQ), lambda bi, qi: (bi * nq + qi, 0, 0)),
                      qcol(IDX_HEADS), seq_rows(512), seq_cols(512), seq_rows(LANES)],
            out_specs=o_spec, out_shape=o_shape,
            scratch_shapes=[pltpu.VMEM((s, TQ), F32)] + attn_scratch,
            compiler_params=_params(2), name="dsa" if bounded else "dsa_general")

    def mla(bounded):
        return pl.pallas_call(
            functools.partial(_mla_kernel, seq=s, bounded=bounded),
            grid=(b, nq),
            in_specs=[qcol(B_HEADS * LANES), seq_rows(B_HEADS * LANES), seq_cols(512)],
            out_specs=o_spec, out_shape=o_shape,
            scratch_shapes=attn_scratch,
            compiler_params=_params(2), name="mla" if bounded else "mla_general")

    per_b = s // TM

    def mem_attn(bounded):
        return pl.pallas_call(
            functools.partial(_mem_attn_kernel, mem_len=m_len, bounded=bounded),
            grid=(n // TM,),
            in_specs=[tile_t(512), pl.BlockSpec((m_len, 512), lambda i: (i // per_b, 0)),
                      pl.BlockSpec((512, m_len), lambda i: (0, i // per_b))],
            out_specs=tile(512), out_shape=o_shape,
            scratch_shapes=[pltpu.VMEM((m_len, TM), F32), pltpu.VMEM((M_HEADS * m_len, TM), BF16),
                            pltpu.VMEM((512, TM), F32)],
            compiler_params=_params(1), name="mem_attn" if bounded else "mem_attn_general")

    def attend(call, gq, gk, dim, *operands):
        bound = dim ** 0.5 * LOG2E * 1.02 * jnp.max(jnp.abs(gq)) * jnp.max(jnp.abs(gk))
        return lax.cond(bound <= BOUNDED_SCORE_LIMIT, call(True), call(False), *operands)

    oa = attend(dsa, g_qn_a, g_kn_a, A_HEAD_DIM, qat, qib, wt, ka, vat, ki)
    ob = attend(mla, g_qn_b, g_kn_b, B_QK, qbt, kb, vbt)
    om = attend(mem_attn, g_qn_m, g_kn_m, M_HEAD_DIM, qmt, km, vmt)

    out = pl.pallas_call(
        _final_kernel,
        grid=(n // TM,),
        in_specs=[tile(d), tile(512), tile(512), tile(512), _full(gn.shape), _full(wz.shape), _full(wg.shape),
                  _full(wb.shape), _full(wo.shape)],
        out_specs=tile(d), out_shape=jax.ShapeDtypeStruct((n, d), x.dtype),
        compiler_params=_params(1), name="final",
    )(x2, oa, ob, om, gn, wz, wg, wb, wo)
    return out.reshape(b, s, d)
```

```python
import functools

import numpy as np
import jax
import jax.numpy as jnp
from jax import lax
from jax.experimental import pallas as pl
from jax.experimental.pallas import tpu as pltpu

F32 = jnp.float32
BF16 = jnp.bfloat16
I32 = jnp.int32

D_MODEL = 1024
ROPE_THETA = 500000.0
EPS = 1e-6
NEG = -1e30
N_BRANCH = 3
BRANCH_WIDTH = 512
A_HEADS = 8
A_HEAD_DIM = 64
A_ROT = A_HEAD_DIM // 4
IDX_HEADS = 8
IDX_DIM = 64
TOPK_MAX = 256
B_HEADS = 8
B_NOPE = 64
B_ROPE = 32
B_VDIM = 64
B_QK = B_NOPE + B_ROPE
B_Q_RANK = 384
B_KV_RANK = 256
M_HEADS = 4
M_HEAD_DIM = 128

LANES = 128
TM = 512
TQ = 256
KC = 256
COUNT_ROWS = 64
VMEM_LIMIT = 56 * 1024 * 1024
INT_MIN = -2 ** 31
LOG2E = 1.4426950408889634
BOUNDED_SCORE_LIMIT = 32.0


def _nt(a, b):
    return lax.dot_general(a, b, (((1,), (1,)), ((), ())), preferred_element_type=F32)


def _mm(a, b):
    return jnp.dot(a, b, preferred_element_type=F32)


def _rms_lanes(xf, g_row, n=None):
    n = xf.shape[-1] if n is None else n
    ms = jnp.sum(xf * xf, axis=-1, keepdims=True) / n
    return xf * lax.rsqrt(ms + EPS) * g_row


def _rms_rows(blk, g_col, n=None):
    n = blk.shape[0] if n is None else n
    ms = jnp.sum(blk * blk, axis=0, keepdims=True) / n
    return blk * lax.rsqrt(ms + EPS) * g_col


def _rope_rows(blk, lo, half, cos_t, sin_t):
    x1 = blk[lo:lo + half]
    x2 = blk[lo + half:lo + 2 * half]
    parts = []
    if lo:
        parts.append(blk[:lo])
    parts += [x1 * cos_t - x2 * sin_t, x2 * cos_t + x1 * sin_t]
    if lo + 2 * half < blk.shape[0]:
        parts.append(blk[lo + 2 * half:])
    return jnp.concatenate(parts, axis=0)


def _rope_lanes(yc, half, cos_l, sin_first, sin_second):
    return (yc * cos_l + pltpu.roll(yc, LANES - half, 1) * sin_first
            + pltpu.roll(yc, half, 1) * sin_second)


def _bf16_terms(x, n_terms):
    terms = []
    for _ in range(n_terms):
        t = x.astype(BF16).astype(F32)
        terms.append(t)
        x = x - t
    return terms


def _lane_tables(cos_t, sin_t, expand_ref, nonrot_ref):
    rows = _bf16_terms(cos_t, 3) + _bf16_terms(sin_t, 3)
    pad = jnp.zeros((LANES - 6 * cos_t.shape[0], cos_t.shape[1]), F32)
    stack = jnp.concatenate(rows + [pad], axis=0)
    tab = _mm(stack.T.astype(BF16), expand_ref[...])
    return tab[:, 0:LANES] + nonrot_ref[...], tab[:, LANES:2 * LANES], tab[:, 2 * LANES:3 * LANES]


def _group_mean_square(yc, ones_ref, n):
    hi, lo = _bf16_terms(yc * yc, 2)
    both = jnp.concatenate([hi.astype(BF16), lo.astype(BF16)], axis=1)
    return _mm(both, ones_ref[...]) / n


N_PROJ_A_IN, N_PROJ_A_OUT = 12, 6
N_PROJ_B_IN = 16


def _proj_kernel(x_ref, posr_ref, gn_ref, *refs):
    h = _rms_lanes(x_ref[...], gn_ref[...]).astype(BF16)
    a_in = refs[:N_PROJ_A_IN]
    b_in = refs[N_PROJ_A_IN:N_PROJ_A_IN + N_PROJ_B_IN]
    outs = refs[N_PROJ_A_IN + N_PROJ_B_IN:]
    _proj_a(h, posr_ref, *a_in, *outs[:N_PROJ_A_OUT])
    _proj_b(h, posr_ref, *b_in, *outs[N_PROJ_A_OUT:])


def _proj_a(h, posr_ref, wqa_ref, wqi_ref, wwi_ref, wka_ref, wva_ref,
            wki_ref, gqa_ref, gka_ref, invc_ref, expand_ref, nonrot_ref, grp_ref,
            qat_ref, qib_ref, wt_ref, ka_ref, vat_ref, ki_ref):
    ang_t = invc_ref[...] * posr_ref[...].astype(F32)
    cos_t, sin_t = jnp.cos(ang_t), jnp.sin(ang_t)
    half = A_ROT // 2

    qa = _nt(wqa_ref[...], h)
    gq = gqa_ref[...]
    for hh in range(A_HEADS):
        blk = _rms_rows(qa[hh * A_HEAD_DIM:(hh + 1) * A_HEAD_DIM], gq)
        blk = _rope_rows(blk, 0, half, cos_t, sin_t) * (A_HEAD_DIM ** -0.5 * LOG2E)
        own = hh * LANES + (hh % 2) * A_HEAD_DIM
        other = hh * LANES + (1 - hh % 2) * A_HEAD_DIM
        qat_ref[own:own + A_HEAD_DIM, :] = blk.astype(BF16)
        qat_ref[other:other + A_HEAD_DIM, :] = jnp.zeros((A_HEAD_DIM, TM), BF16)

    qi = _nt(wqi_ref[...], h)
    for hh in range(IDX_HEADS):
        blk = _rope_rows(qi[hh * IDX_DIM:(hh + 1) * IDX_DIM], 0, half, cos_t, sin_t)
        blk = (blk * (IDX_DIM ** -0.5)).astype(BF16)
        for j in range(TM // TQ):
            qib_ref[j, 0:IDX_DIM, hh * TQ:(hh + 1) * TQ] = blk[:, j * TQ:(j + 1) * TQ]
    qib_ref[:, IDX_DIM:, :] = jnp.zeros((TM // TQ, LANES - IDX_DIM, IDX_HEADS * TQ), BF16)

    wt_ref[...] = _nt(wwi_ref[...], h)[0:IDX_HEADS] * (IDX_HEADS ** -0.5)

    vat_ref[...] = _nt(wva_ref[...], h).astype(BF16)

    cos_l, s_first, s_second = _lane_tables(cos_t, sin_t, expand_ref, nonrot_ref)

    ka = _mm(h, wka_ref[...])
    gk = gka_ref[...]
    for c in range(A_HEADS * A_HEAD_DIM // LANES):
        kc = ka[:, c * LANES:(c + 1) * LANES]
        ms = _group_mean_square(kc, grp_ref, A_HEAD_DIM)
        y = kc * lax.rsqrt(ms + EPS) * gk
        ka_ref[c, :, :] = _rope_lanes(y, half, cos_l, s_first, s_second).astype(BF16)

    ki = _mm(h, wki_ref[...])
    ki_ref[...] = _rope_lanes(ki, half, cos_l, s_first, s_second).astype(BF16)


def _proj_b(h, posr_ref, wcq_ref, wckv_ref, wkr_ref, wqm_ref, wuq_ref,
            wuk_ref, wuv_ref, gcq_ref, gckv_ref, gqb_ref, gkb_ref, gqm_ref, invc_ref, expand_ref,
            nonrot_ref, grp_ref,
            qbt_ref, kb_ref, vbt_ref, qmt_ref):
    half = B_ROPE // 2
    ang_t = invc_ref[...] * posr_ref[...].astype(F32)
    cos_t, sin_t = jnp.cos(ang_t), jnp.sin(ang_t)
    cos_l, s_first, s_second = _lane_tables(cos_t, sin_t, expand_ref, nonrot_ref)

    cq = _rms_lanes(_mm(h, wcq_ref[...]), gcq_ref[...]).astype(BF16)
    qb = _nt(wuq_ref[...], cq)
    gq = gqb_ref[...]
    for hh in range(B_HEADS):
        blk = _rms_rows(qb[hh * LANES:(hh + 1) * LANES], gq, n=B_QK)
        blk = _rope_rows(blk, B_NOPE, half, cos_t, sin_t) * (B_QK ** -0.5 * LOG2E)
        qbt_ref[hh * LANES:(hh + 1) * LANES, :] = blk.astype(BF16)

    ckv = _rms_lanes(_mm(h, wckv_ref[...]), gckv_ref[...]).astype(BF16)
    kn = _mm(ckv, wuk_ref[...])
    kr = _mm(h, wkr_ref[...])
    gk = gkb_ref[...]
    for hh in range(B_HEADS):
        kc = kn[:, hh * LANES:(hh + 1) * LANES] + kr
        y = kc * lax.rsqrt(_group_mean_square(kc, grp_ref, B_QK) + EPS) * gk
        kb_ref[:, hh * LANES:(hh + 1) * LANES] = _rope_lanes(y, half, cos_l, s_first, s_second).astype(BF16)
    vbt_ref[...] = _nt(wuv_ref[...], ckv).astype(BF16)

    qm = _nt(wqm_ref[...], h)
    gm = gqm_ref[...]
    for hh in range(M_HEADS):
        blk = _rms_rows(qm[hh * M_HEAD_DIM:(hh + 1) * M_HEAD_DIM], gm) * (M_HEAD_DIM ** -0.5 * LOG2E)
        qmt_ref[hh * M_HEAD_DIM:(hh + 1) * M_HEAD_DIM, :] = blk.astype(BF16)


def _mem_kv_kernel(mem_ref, gmem_ref, wk_ref, wvt_ref, gkm_ref, km_ref, vmt_ref):
    hm = _rms_lanes(mem_ref[...], gmem_ref[...]).astype(BF16)
    k = _mm(hm, wk_ref[...])
    gk = gkm_ref[...]
    for hh in range(M_HEADS):
        kc = _rms_lanes(k[:, hh * M_HEAD_DIM:(hh + 1) * M_HEAD_DIM], gk)
        km_ref[:, hh * M_HEAD_DIM:(hh + 1) * M_HEAD_DIM] = kc.astype(BF16)
    vmt_ref[...] = _nt(wvt_ref[...], hm).astype(BF16)


def _attend(nk, n_heads, dv, q_of, k_of, v_of, bias_of, bounded, s_ref, p_ref, ot_ref):
    nq = ot_ref.shape[1]
    chunks = [slice(c * KC, (c + 1) * KC) for c in range(nk // KC)]

    def scores(hh, q, c):
        s = _mm(k_of(hh, chunks[c]), q)
        b = bias_of(c)
        return s if b is None else s + b

    depth = max(1, min(n_heads - 1, 8 // len(chunks)))
    slots = depth + 1

    def p_rows(hh, sl):
        base = (hh % slots) * nk
        return slice(base + sl.start, base + sl.stop)

    def probabilities(hh):
        q = q_of(hh)
        if bounded:
            l8 = jnp.zeros((8, nq), F32)
            for c in range(len(chunks)):
                p = jnp.exp2(scores(hh, q, c))
                l8 = l8 + p.reshape(KC // 8, 8, nq).sum(axis=0)
                p_ref[p_rows(hh, chunks[c]), :] = p.astype(BF16)
            return jnp.sum(l8, axis=0, keepdims=True)
        m = jnp.full((1, nq), -jnp.inf, F32)
        for c in range(len(chunks)):
            s = scores(hh, q, c)
            s_ref[chunks[c], :] = s
            m = jnp.maximum(m, jnp.max(s, axis=0, keepdims=True))
        l = jnp.zeros((1, nq), F32)
        for c in range(len(chunks)):
            p = jnp.exp2(s_ref[chunks[c], :] - m)
            l = l + jnp.sum(p, axis=0, keepdims=True)
            p_ref[p_rows(hh, chunks[c]), :] = p.astype(BF16)
        return l

    def weighted_values(hh, l):
        o = _mm(v_of(hh, slice(0, nk)), p_ref[p_rows(hh, slice(0, nk)), :])
        ot_ref[hh * dv:(hh + 1) * dv, :] = o / l

    sums = {}
    for step in range(n_heads + depth):
        if step < n_heads:
            sums[step] = probabilities(step)
        if step >= depth:
            weighted_values(step - depth, sums.pop(step - depth))


def _count(score_ref, nk, pred):
    cnt = jnp.zeros((COUNT_ROWS, TQ), I32)
    for r in range(0, nk, COUNT_ROWS):
        cnt = jnp.where(pred(score_ref[r:r + COUNT_ROWS, :]), cnt + 1, cnt)
    return jnp.sum(cnt, axis=0, keepdims=True)


def _ordered_pattern_to_float(u):
    k = u ^ INT_MIN
    return pltpu.bitcast(k ^ ((k >> 31) & 0x7FFFFFFF), F32)


def _select_topk(nk, q_pos, row, chunks, qib_ref, wt_ref, ki_ref, score_ref, emit):
    for c, sl in enumerate(chunks):
        ki_c = ki_ref[sl, :]
        acc = jnp.zeros((KC, TQ), F32)
        for hh in range(IDX_HEADS):
            d = _mm(ki_c, qib_ref[0, :, hh * TQ:(hh + 1) * TQ])
            acc = acc + jnp.maximum(d, 0.0) * wt_ref[hh:hh + 1, :]
        score_ref[sl, :] = jnp.where(row + c * KC <= q_pos, acc, NEG)

    def step(i, carry):
        t_u, cnt_t = carry
        cand_u = t_u | jnp.left_shift(jnp.int32(1), 31 - i)
        cand = _ordered_pattern_to_float(cand_u)
        cnt = _count(score_ref, nk, lambda x: x >= cand)
        ok = cnt >= TOPK_MAX
        return jnp.where(ok, cand_u, t_u), jnp.where(ok, cnt, cnt_t)

    t_u, cnt_t = lax.fori_loop(0, 32, step, (jnp.zeros((1, TQ), I32), jnp.full((1, TQ), nk, I32)))
    thr = _ordered_pattern_to_float(t_u)
    split_ties = jnp.max(jnp.where(cnt_t > TOPK_MAX, 1, 0)) > 0

    @pl.when(jnp.logical_not(split_ties))
    def _():
        for c, sl in enumerate(chunks):
            emit(sl, (score_ref[sl, :] >= thr) & (row + c * KC <= q_pos))

    @pl.when(split_ties)
    def _():
        room = (TOPK_MAX - _count(score_ref, nk, lambda x: x > thr)).astype(F32)
        tri = lax.broadcasted_iota(I32, (KC, KC), 0) >= lax.broadcasted_iota(I32, (KC, KC), 1)
        tri = jnp.where(tri, 1.0, 0.0).astype(BF16)
        running = jnp.zeros((1, TQ), F32)
        for c, sl in enumerate(chunks):
            x = score_ref[sl, :]
            tie = x == thr
            rank = _mm(tri, jnp.where(tie, 1.0, 0.0).astype(BF16)) + running
            running = rank[KC - 1:KC, :]
            emit(sl, ((x > thr) | (tie & (rank <= room))) & (row + c * KC <= q_pos))


def _dsa_body(nk, start, bounded, qat_ref, qib_ref, wt_ref, ka_ref, vat_ref, ki_ref, oa_ref,
              score_ref, bias_ref, s_ref, p_ref, ot_ref):
    q_pos = start + lax.broadcasted_iota(I32, (1, TQ), 1)
    row = lax.broadcasted_iota(I32, (KC, TQ), 0)
    chunks = [slice(c * KC, (c + 1) * KC) for c in range(nk // KC)]

    def emit_bias(sl, keep):
        bias_ref[sl, :] = jnp.where(keep, 0.0, NEG)

    if nk <= TOPK_MAX:
        for c, sl in enumerate(chunks):
            emit_bias(sl, row + c * KC <= q_pos)
    else:
        _select_topk(nk, q_pos, row, chunks, qib_ref, wt_ref, ki_ref, score_ref, emit_bias)

    def q_of(hh):
        return qat_ref[hh * LANES:(hh + 1) * LANES, :]

    def k_of(hh, sl):
        return ka_ref[hh // 2, sl, :]

    def v_of(hh, sl):
        return vat_ref[hh * A_HEAD_DIM:(hh + 1) * A_HEAD_DIM, sl]

    _attend(nk, A_HEADS, A_HEAD_DIM, q_of, k_of, v_of, lambda c: bias_ref[chunks[c], :], bounded,
            s_ref, p_ref, ot_ref)
    oa_ref[...] = ot_ref[...].T


def _dsa_kernel(qat_ref, qib_ref, wt_ref, ka_ref, vat_ref, ki_ref, oa_ref,
                score_ref, bias_ref, s_ref, p_ref, ot_ref, *, seq, bounded):
    qb = pl.program_id(1)
    for cls in range(seq // TQ):
        @pl.when(qb == cls)
        def _():
            _dsa_body(TQ * (cls + 1), cls * TQ, bounded, qat_ref, qib_ref, wt_ref, ka_ref, vat_ref,
                      ki_ref, oa_ref, score_ref, bias_ref, s_ref, p_ref, ot_ref)


def _mla_body(nk, start, bounded, qbt_ref, kb_ref, vbt_ref, ob_ref, bias_ref, s_ref, p_ref, ot_ref):
    last = nk // KC - 1
    q_pos = start + lax.broadcasted_iota(I32, (1, TQ), 1)
    row = lax.broadcasted_iota(I32, (KC, TQ), 0)
    bias_ref[0:KC, :] = jnp.where(row + last * KC <= q_pos, 0.0, NEG)

    def q_of(hh):
        return qbt_ref[hh * LANES:(hh + 1) * LANES, :]

    def k_of(hh, sl):
        return kb_ref[sl, hh * LANES:(hh + 1) * LANES]

    def v_of(hh, sl):
        return vbt_ref[hh * B_VDIM:(hh + 1) * B_VDIM, sl]

    _attend(nk, B_HEADS, B_VDIM, q_of, k_of, v_of, lambda c: bias_ref[0:KC, :] if c == last else None,
            bounded, s_ref, p_ref, ot_ref)
    ob_ref[...] = ot_ref[...].T


def _mla_kernel(qbt_ref, kb_ref, vbt_ref, ob_ref, bias_ref, s_ref, p_ref, ot_ref, *, seq, bounded):
    qb = pl.program_id(1)
    for cls in range(seq // TQ):
        @pl.when(qb == cls)
        def _():
            _mla_body(TQ * (cls + 1), cls * TQ, bounded, qbt_ref, kb_ref, vbt_ref, ob_ref,
                      bias_ref, s_ref, p_ref, ot_ref)


def _mem_attn_kernel(qmt_ref, km_ref, vmt_ref, om_ref, s_ref, p_ref, ot_ref, *, mem_len, bounded):
    def q_of(hh):
        return qmt_ref[hh * M_HEAD_DIM:(hh + 1) * M_HEAD_DIM, :]

    def k_of(hh, sl):
        return km_ref[sl, hh * M_HEAD_DIM:(hh + 1) * M_HEAD_DIM]

    def v_of(hh, sl):
        return vmt_ref[hh * M_HEAD_DIM:(hh + 1) * M_HEAD_DIM, sl]

    _attend(mem_len, M_HEADS, M_HEAD_DIM, q_of, k_of, v_of, lambda c: None, bounded, s_ref, p_ref, ot_ref)
    om_ref[...] = ot_ref[...].T


def _final_kernel(x_ref, oa_ref, ob_ref, om_ref, gn_ref, wz_ref, wg_ref, wb_ref, wo_ref, out_ref):
    x = x_ref[...]
    h = _rms_lanes(x, gn_ref[...]).astype(BF16)
    merged = jnp.zeros((TM, D_MODEL), F32)
    for n, o_ref in enumerate((oa_ref, ob_ref, om_ref)):
        z = _mm(h, wz_ref[n])
        y = (o_ref[...] * (z * jax.nn.sigmoid(z))).astype(BF16)
        branch = _mm(y, wb_ref[n])
        gate = jax.nn.sigmoid(_mm(h, wg_ref[:, n * D_MODEL:(n + 1) * D_MODEL]))
        merged = merged + gate * branch
    out_ref[...] = x + _mm(merged.astype(BF16), wo_ref[...])


def _full(shape):
    return pl.BlockSpec(shape, lambda *_: (0,) * len(shape), pipeline_mode=pl.Buffered(1))


def _params(n_axes):
    return pltpu.CompilerParams(dimension_semantics=("arbitrary",) * n_axes,
                                vmem_limit_bytes=VMEM_LIMIT)


def _rope_expansion(half, period, lo):
    j = np.arange(LANES) % period
    first = (j >= lo) & (j < lo + half)
    second = (j >= lo + half) & (j < lo + 2 * half)
    freq = np.where(first, j - lo, j - lo - half)
    mat = np.zeros((LANES, 3 * LANES), np.float32)
    for lane in np.nonzero(first | second)[0]:
        for term in range(3):
            mat[term * half + freq[lane], lane] = 1.0
            col = LANES + lane if first[lane] else 2 * LANES + lane
            mat[(3 + term) * half + freq[lane], col] = -1.0 if first[lane] else 1.0
    nonrot = np.where(first | second, 0.0, 1.0).astype(np.float32).reshape(1, LANES)
    return jnp.asarray(mat, BF16), jnp.asarray(nonrot)


def _group_matrix(group):
    g = np.arange(LANES) // group
    member = (g[:, None] == g[None, :]).astype(np.float32)
    return jnp.asarray(np.concatenate([member, member], axis=0), BF16)


def kernel(x, mem, positions, g_norm, w_in, g_qn_a, g_kn_a, g_cq, g_ckv, w_uq, w_ukv, g_qn_b, g_kn_b,
           g_mem, w_mem_kv, g_qn_m, g_kn_m, w_branch, w_out):
    b, s, d = x.shape
    m_len = mem.shape[1]
    n = b * s
    nq = s // TQ
    assert d == D_MODEL and s % TQ == 0 and TQ == KC and TM % TQ == 0 and n % TM == 0 and m_len % KC == 0
    assert g_norm.shape[0] == 1, "single-layer block"

    w = w_in[0]
    off = np.cumsum([0, 512, 512, 512, 512, IDX_DIM, IDX_HEADS, BRANCH_WIDTH, B_Q_RANK, B_KV_RANK, B_ROPE,
                     BRANCH_WIDTH, M_HEADS * M_HEAD_DIM, BRANCH_WIDTH, N_BRANCH * D_MODEL])
    seg = [w[:, off[i]:off[i + 1]] for i in range(14)]
    (w_qa, w_ka, w_va, w_qi, w_ki, w_wi, w_za, w_cq, w_ckv, w_kr, w_zb, w_qm, w_zm, w_gate) = seg
    bf = lambda a: a.astype(BF16)
    wqa_t, wqi_t, wva_t, wqm_t = bf(w_qa.T), bf(w_qi.T), bf(w_va.T), bf(w_qm.T)
    wwi_t = bf(jnp.pad(w_wi.T, ((0, 16 - IDX_HEADS), (0, 0))))
    wki_p = bf(jnp.pad(w_ki, ((0, 0), (0, LANES - IDX_DIM))))
    wkr_p = bf(jnp.pad(w_kr, ((0, 0), (B_NOPE, LANES - B_QK))))
    wuq_t = bf(jnp.pad(w_uq[0].reshape(B_Q_RANK, B_HEADS, B_QK), ((0, 0), (0, 0), (0, LANES - B_QK)))
               .reshape(B_Q_RANK, B_HEADS * LANES).T)
    ukv = w_ukv[0].reshape(B_KV_RANK, B_HEADS, B_NOPE + B_VDIM)
    wuk_p = bf(jnp.pad(ukv[:, :, :B_NOPE], ((0, 0), (0, 0), (0, LANES - B_NOPE)))
               .reshape(B_KV_RANK, B_HEADS * LANES))
    wuv_t = bf(ukv[:, :, B_NOPE:].reshape(B_KV_RANK, B_HEADS * B_VDIM).T)
    wmk = bf(w_mem_kv[0][:, :M_HEADS * M_HEAD_DIM])
    wmv_t = bf(w_mem_kv[0][:, M_HEADS * M_HEAD_DIM:].T)
    wz = bf(jnp.stack([w_za, w_zb, w_zm]))
    wg = bf(w_gate)
    wb = bf(w_branch[0])
    wo = bf(w_out[0])

    gn = g_norm[0].reshape(1, D_MODEL)
    gqa_c = g_qn_a[0].reshape(A_HEAD_DIM, 1)
    gka_r = jnp.tile(g_kn_a[0], LANES // A_HEAD_DIM).reshape(1, LANES)
    gcq_r = g_cq[0].reshape(1, B_Q_RANK)
    gckv_r = g_ckv[0].reshape(1, B_KV_RANK)
    gqb_c = jnp.pad(g_qn_b[0], (0, LANES - B_QK)).reshape(LANES, 1)
    gkb_r = jnp.pad(g_kn_b[0], (0, LANES - B_QK)).reshape(1, LANES)
    gqm_c = g_qn_m[0].reshape(M_HEAD_DIM, 1)
    gkm_r = g_kn_m[0].reshape(1, M_HEAD_DIM)
    gmem_r = g_mem[0].reshape(1, D_MODEL)

    inv_a = ROPE_THETA ** (-(jnp.arange(0, A_ROT, 2, dtype=F32) / A_ROT))
    inv_b = ROPE_THETA ** (-(jnp.arange(0, B_ROPE, 2, dtype=F32) / B_ROPE))
    expand_a, nonrot_a = _rope_expansion(A_ROT // 2, A_HEAD_DIM, 0)
    expand_b, nonrot_b = _rope_expansion(B_ROPE // 2, LANES, B_NOPE)

    x2 = x.reshape(n, d)
    pos_r = positions.reshape(1, n)
    tile = lambda width: pl.BlockSpec((TM, width), lambda i: (i, 0))
    tile_t = lambda rows: pl.BlockSpec((rows, TM), lambda i: (0, i))
    pos_spec = pl.BlockSpec((1, TM), lambda i: (0, i))

    a_in = [wqa_t, wqi_t, wwi_t, bf(w_ka), wva_t, wki_p, gqa_c, gka_r,
            inv_a.reshape(-1, 1), expand_a, nonrot_a, _group_matrix(A_HEAD_DIM)]
    b_in = [bf(w_cq), bf(w_ckv), wkr_p, wqm_t, wuq_t, wuk_p, wuv_t,
            gcq_r, gckv_r, gqb_c, gkb_r, gqm_c, inv_b.reshape(-1, 1), expand_b, nonrot_b,
            _group_matrix(LANES)]
    assert len(a_in) == N_PROJ_A_IN and len(b_in) == N_PROJ_B_IN
    qat, qib, wt, ka, vat, ki, qbt, kb, vbt, qmt = pl.pallas_call(
        _proj_kernel,
        grid=(n // TM,),
        in_specs=[tile(d), pos_spec, _full(gn.shape)] + [_full(a.shape) for a in a_in + b_in],
        out_specs=[tile_t(A_HEADS * LANES), pl.BlockSpec((TM // TQ, LANES, IDX_HEADS * TQ), lambda i: (i, 0, 0)),
                   tile_t(IDX_HEADS), pl.BlockSpec((512 // LANES, TM, LANES), lambda i: (0, i, 0)),
                   tile_t(512), tile(LANES),
                   tile_t(B_HEADS * LANES), tile(B_HEADS * LANES), tile_t(512), tile_t(512)],
        out_shape=[jax.ShapeDtypeStruct((A_HEADS * LANES, n), BF16),
                   jax.ShapeDtypeStruct((n // TQ, LANES, IDX_HEADS * TQ), BF16),
                   jax.ShapeDtypeStruct((IDX_HEADS, n), F32),
                   jax.ShapeDtypeStruct((512 // LANES, n, LANES), BF16),
                   jax.ShapeDtypeStruct((512, n), BF16),
                   jax.ShapeDtypeStruct((n, LANES), BF16),
                   jax.ShapeDtypeStruct((B_HEADS * LANES, n), BF16),
                   jax.ShapeDtypeStruct((n, B_HEADS * LANES), BF16),
                   jax.ShapeDtypeStruct((512, n), BF16),
                   jax.ShapeDtypeStruct((512, n), BF16)],
        compiler_params=_params(1), name="proj",
    )(x2, pos_r, gn, *a_in, *b_in)

    km, vmt = pl.pallas_call(
        _mem_kv_kernel,
        grid=(b,),
        in_specs=[pl.BlockSpec((m_len, d), lambda i: (i, 0)), _full(gmem_r.shape), _full(wmk.shape),
                  _full(wmv_t.shape), _full(gkm_r.shape)],
        out_specs=[pl.BlockSpec((m_len, 512), lambda i: (i, 0)), pl.BlockSpec((512, m_len), lambda i: (0, i))],
        out_shape=[jax.ShapeDtypeStruct((b * m_len, 512), BF16), jax.ShapeDtypeStruct((512, b * m_len), BF16)],
        compiler_params=_params(1), name="mem_kv",
    )(mem.reshape(b * m_len, d), gmem_r, wmk, wmv_t, gkm_r)

    qcol = lambda rows: pl.BlockSpec((rows, TQ), lambda bi, qi: (0, bi * nq + qi))
    seq_rows = lambda width: pl.BlockSpec((s, width), lambda bi, qi: (bi, 0))
    seq_cols = lambda rows: pl.BlockSpec((rows, s), lambda bi, qi: (0, bi))
    o_spec = pl.BlockSpec((TQ, 512), lambda bi, qi: (bi * nq + qi, 0))
    o_shape = jax.ShapeDtypeStruct((n, 512), F32)
    attn_scratch = [pltpu.VMEM((s, TQ), F32), pltpu.VMEM((s, TQ), F32), pltpu.VMEM((2 * s, TQ), BF16),
                    pltpu.VMEM((512, TQ), F32)]

    def dsa(bounded):
        return pl.pallas_call(
            functools.partial(_dsa_kernel, seq=s, bounded=bounded),
            grid=(b, nq),
            in_specs=[qcol(A_HEADS * LANES),
                      pl.BlockSpec((1, LANES, IDX_HEADS * TQ), lambda bi, qi: (bi * nq + qi, 0, 0)),
                      qcol(IDX_HEADS), pl.BlockSpec((512 // LANES, s, LANES), lambda bi, qi: (0, bi, 0)),
                      seq_cols(512), seq_rows(LANES)],
            out_specs=o_spec, out_shape=o_shape,
            scratch_shapes=[pltpu.VMEM((s, TQ), F32)] + attn_scratch,
            compiler_params=_params(2), name="dsa" if bounded else "dsa_general")

    def mla(bounded):
        return pl.pallas_call(
            functools.partial(_mla_kernel, seq=s, bounded=bounded),
            grid=(b, nq),
            in_specs=[qcol(B_HEADS * LANES), seq_rows(B_HEADS * LANES), seq_cols(512)],
            out_specs=o_spec, out_shape=o_shape,
            scratch_shapes=attn_scratch,
            compiler_params=_params(2), name="mla" if bounded else "mla_general")

    def mem_attn(bounded):
        return pl.pallas_call(
            functools.partial(_mem_attn_kernel, mem_len=m_len, bounded=bounded),
            grid=(b, nq),
            in_specs=[qcol(512), pl.BlockSpec((m_len, 512), lambda bi, qi: (bi, 0)),
                      pl.BlockSpec((512, m_len), lambda bi, qi: (0, bi))],
            out_specs=o_spec, out_shape=o_shape,
            scratch_shapes=[pltpu.VMEM((m_len, TQ), F32), pltpu.VMEM((M_HEADS * m_len, TQ), BF16),
                            pltpu.VMEM((512, TQ), F32)],
            compiler_params=_params(2), name="mem_attn" if bounded else "mem_attn_general")

    def score_bound(gq, gk, dim):
        return dim ** 0.5 * LOG2E * 1.02 * jnp.max(jnp.abs(gq)) * jnp.max(jnp.abs(gk))

    def attention(bounded):
        def run(*ops):
            return (dsa(bounded)(*ops[:6]), mla(bounded)(*ops[6:9]), mem_attn(bounded)(*ops[9:]))
        return run

    worst = jnp.maximum(jnp.maximum(score_bound(g_qn_a, g_kn_a, A_HEAD_DIM), score_bound(g_qn_b, g_kn_b, B_QK)),
                        score_bound(g_qn_m, g_kn_m, M_HEAD_DIM))
    oa, ob, om = lax.cond(worst <= BOUNDED_SCORE_LIMIT, attention(True), attention(False),
                          qat, qib, wt, ka, vat, ki, qbt, kb, vbt, qmt, km, vmt)

    out = pl.pallas_call(
        _final_kernel,
        grid=(n // TM,),
        in_specs=[tile(d), tile(512), tile(512), tile(512), _full(gn.shape), _full(wz.shape), _full(wg.shape),
                  _full(wb.shape), _full(wo.shape)],
        out_specs=tile(d), out_shape=jax.ShapeDtypeStruct((n, d), x.dtype),
        compiler_params=_params(1), name="final",
    )(x2, oa, ob, om, gn, wz, wg, wb, wo)
    return out.reshape(b, s, d)
```

```python
import functools

import numpy as np
import jax
import jax.numpy as jnp
from jax import lax
from jax.experimental import pallas as pl
from jax.experimental.pallas import tpu as pltpu

F32 = jnp.float32
BF16 = jnp.bfloat16
I32 = jnp.int32

D_MODEL = 1024
ROPE_THETA = 500000.0
EPS = 1e-6
NEG = -1e30
N_BRANCH = 3
BRANCH_WIDTH = 512
A_HEADS = 8
A_HEAD_DIM = 64
A_ROT = A_HEAD_DIM // 4
IDX_HEADS = 8
IDX_DIM = 64
TOPK_MAX = 256
B_HEADS = 8
B_NOPE = 64
B_ROPE = 32
B_VDIM = 64
B_QK = B_NOPE + B_ROPE
B_Q_RANK = 384
B_KV_RANK = 256
M_HEADS = 4
M_HEAD_DIM = 128

LANES = 128
TM = 512
TQ = 256
KC = 256
COUNT_ROWS = 64
VMEM_LIMIT = 56 * 1024 * 1024
INT_MIN = -2 ** 31
LOG2E = 1.4426950408889634
BOUNDED_SCORE_LIMIT = 32.0


def _nt(a, b):
    return lax.dot_general(a, b, (((1,), (1,)), ((), ())), preferred_element_type=F32)


def _mm(a, b):
    return jnp.dot(a, b, preferred_element_type=F32)


def _rms_lanes(xf, g_row, n=None):
    n = xf.shape[-1] if n is None else n
    ms = jnp.sum(xf * xf, axis=-1, keepdims=True) / n
    return xf * lax.rsqrt(ms + EPS) * g_row


def _rms_rows(blk, g_col, n=None):
    n = blk.shape[0] if n is None else n
    ms = jnp.sum(blk * blk, axis=0, keepdims=True) / n
    return blk * lax.rsqrt(ms + EPS) * g_col


def _rope_rows(blk, lo, half, cos_t, sin_t):
    x1 = blk[lo:lo + half]
    x2 = blk[lo + half:lo + 2 * half]
    parts = []
    if lo:
        parts.append(blk[:lo])
    parts += [x1 * cos_t - x2 * sin_t, x2 * cos_t + x1 * sin_t]
    if lo + 2 * half < blk.shape[0]:
        parts.append(blk[lo + 2 * half:])
    return jnp.concatenate(parts, axis=0)


def _rope_lanes(yc, half, cos_l, sin_first, sin_second):
    return (yc * cos_l + pltpu.roll(yc, LANES - half, 1) * sin_first
            + pltpu.roll(yc, half, 1) * sin_second)


def _bf16_terms(x, n_terms):
    terms = []
    for _ in range(n_terms):
        t = x.astype(BF16).astype(F32)
        terms.append(t)
        x = x - t
    return terms


def _lane_tables(cos_t, sin_t, expand_ref, nonrot_ref):
    rows = _bf16_terms(cos_t, 3) + _bf16_terms(sin_t, 3)
    pad = jnp.zeros((LANES - 6 * cos_t.shape[0], cos_t.shape[1]), F32)
    stack = jnp.concatenate(rows + [pad], axis=0)
    tab = _mm(stack.T.astype(BF16), expand_ref[...])
    return tab[:, 0:LANES] + nonrot_ref[...], tab[:, LANES:2 * LANES], tab[:, 2 * LANES:3 * LANES]


def _group_mean_square(yc, ones_ref, n):
    hi, lo = _bf16_terms(yc * yc, 2)
    both = jnp.concatenate([hi.astype(BF16), lo.astype(BF16)], axis=1)
    return _mm(both, ones_ref[...]) / n


N_PROJ_A_IN, N_PROJ_A_OUT = 12, 6
N_PROJ_B_IN = 16


def _proj_kernel(x_ref, posr_ref, gn_ref, *refs):
    h = _rms_lanes(x_ref[...], gn_ref[...]).astype(BF16)
    a_in = refs[:N_PROJ_A_IN]
    b_in = refs[N_PROJ_A_IN:N_PROJ_A_IN + N_PROJ_B_IN]
    outs = refs[N_PROJ_A_IN + N_PROJ_B_IN:]
    _proj_a(h, posr_ref, *a_in, *outs[:N_PROJ_A_OUT])
    _proj_b(h, posr_ref, *b_in, *outs[N_PROJ_A_OUT:])


def _proj_a(h, posr_ref, wqa_ref, wqi_ref, wwi_ref, wka_ref, wva_ref,
            wki_ref, gqa_ref, gka_ref, invc_ref, expand_ref, nonrot_ref, grp_ref,
            qat_ref, qib_ref, wt_ref, ka_ref, vat_ref, ki_ref):
    ang_t = invc_ref[...] * posr_ref[...].astype(F32)
    cos_t, sin_t = jnp.cos(ang_t), jnp.sin(ang_t)
    half = A_ROT // 2

    qa = _nt(wqa_ref[...], h)
    gq = gqa_ref[...]
    for hh in range(A_HEADS):
        blk = _rms_rows(qa[hh * A_HEAD_DIM:(hh + 1) * A_HEAD_DIM], gq)
        blk = _rope_rows(blk, 0, half, cos_t, sin_t) * (A_HEAD_DIM ** -0.5 * LOG2E)
        own = hh * LANES + (hh % 2) * A_HEAD_DIM
        other = hh * LANES + (1 - hh % 2) * A_HEAD_DIM
        qat_ref[own:own + A_HEAD_DIM, :] = blk.astype(BF16)
        qat_ref[other:other + A_HEAD_DIM, :] = jnp.zeros((A_HEAD_DIM, TM), BF16)

    qi = _nt(wqi_ref[...], h)
    for hh in range(IDX_HEADS):
        blk = _rope_rows(qi[hh * IDX_DIM:(hh + 1) * IDX_DIM], 0, half, cos_t, sin_t)
        blk = (blk * (IDX_DIM ** -0.5)).astype(BF16)
        for j in range(TM // TQ):
            qib_ref[j, 0:IDX_DIM, hh * TQ:(hh + 1) * TQ] = blk[:, j * TQ:(j + 1) * TQ]
    qib_ref[:, IDX_DIM:, :] = jnp.zeros((TM // TQ, LANES - IDX_DIM, IDX_HEADS * TQ), BF16)

    wt_ref[...] = _nt(wwi_ref[...], h)[0:IDX_HEADS] * (IDX_HEADS ** -0.5)

    vat_ref[...] = _nt(wva_ref[...], h).astype(BF16)

    cos_l, s_first, s_second = _lane_tables(cos_t, sin_t, expand_ref, nonrot_ref)

    ka = _mm(h, wka_ref[...])
    gk = gka_ref[...]
    for c in range(A_HEADS * A_HEAD_DIM // LANES):
        kc = ka[:, c * LANES:(c + 1) * LANES]
        ms = _group_mean_square(kc, grp_ref, A_HEAD_DIM)
        y = kc * lax.rsqrt(ms + EPS) * gk
        ka_ref[c, :, :] = _rope_lanes(y, half, cos_l, s_first, s_second).astype(BF16)

    ki = _mm(h, wki_ref[...])
    ki_ref[...] = _rope_lanes(ki, half, cos_l, s_first, s_second).astype(BF16)


def _proj_b(h, posr_ref, wcq_ref, wckv_ref, wkr_ref, wqm_ref, wuq_ref,
            wuk_ref, wuv_ref, gcq_ref, gckv_ref, gqb_ref, gkb_ref, gqm_ref, invc_ref, expand_ref,
            nonrot_ref, grp_ref,
            qbt_ref, kb_ref, vbt_ref, qmt_ref):
    half = B_ROPE // 2
    ang_t = invc_ref[...] * posr_ref[...].astype(F32)
    cos_t, sin_t = jnp.cos(ang_t), jnp.sin(ang_t)
    cos_l, s_first, s_second = _lane_tables(cos_t, sin_t, expand_ref, nonrot_ref)

    cq = _rms_lanes(_mm(h, wcq_ref[...]), gcq_ref[...]).astype(BF16)
    qb = _nt(wuq_ref[...], cq)
    gq = gqb_ref[...]
    for hh in range(B_HEADS):
        blk = _rms_rows(qb[hh * LANES:(hh + 1) * LANES], gq, n=B_QK)
        blk = _rope_rows(blk, B_NOPE, half, cos_t, sin_t) * (B_QK ** -0.5 * LOG2E)
        qbt_ref[hh * LANES:(hh + 1) * LANES, :] = blk.astype(BF16)

    ckv = _rms_lanes(_mm(h, wckv_ref[...]), gckv_ref[...]).astype(BF16)
    kn = _mm(ckv, wuk_ref[...])
    kr = _mm(h, wkr_ref[...])
    gk = gkb_ref[...]
    for hh in range(B_HEADS):
        kc = kn[:, hh * LANES:(hh + 1) * LANES] + kr
        y = kc * lax.rsqrt(_group_mean_square(kc, grp_ref, B_QK) + EPS) * gk
        kb_ref[:, hh * LANES:(hh + 1) * LANES] = _rope_lanes(y, half, cos_l, s_first, s_second).astype(BF16)
    vbt_ref[...] = _nt(wuv_ref[...], ckv).astype(BF16)

    qm = _nt(wqm_ref[...], h)
    gm = gqm_ref[...]
    for hh in range(M_HEADS):
        blk = _rms_rows(qm[hh * M_HEAD_DIM:(hh + 1) * M_HEAD_DIM], gm) * (M_HEAD_DIM ** -0.5 * LOG2E)
        qmt_ref[hh * M_HEAD_DIM:(hh + 1) * M_HEAD_DIM, :] = blk.astype(BF16)


def _mem_kv_kernel(mem_ref, gmem_ref, wk_ref, wvt_ref, gkm_ref, km_ref, vmt_ref):
    hm = _rms_lanes(mem_ref[...], gmem_ref[...]).astype(BF16)
    k = _mm(hm, wk_ref[...])
    gk = gkm_ref[...]
    for hh in range(M_HEADS):
        kc = _rms_lanes(k[:, hh * M_HEAD_DIM:(hh + 1) * M_HEAD_DIM], gk)
        km_ref[:, hh * M_HEAD_DIM:(hh + 1) * M_HEAD_DIM] = kc.astype(BF16)
    vmt_ref[...] = _nt(wvt_ref[...], hm).astype(BF16)


def _attend(nk, n_heads, dv, q_of, k_of, v_of, bias_of, bounded, s_ref, p_ref, ot_ref):
    nq = ot_ref.shape[1]
    chunks = [slice(c * KC, (c + 1) * KC) for c in range(nk // KC)]

    def scores(hh, q, c):
        s = _mm(k_of(hh, chunks[c]), q)
        b = bias_of(c)
        return s if b is None else s + b

    depth = max(1, min(n_heads - 1, 8 // len(chunks)))
    slots = depth + 1

    def p_rows(hh, sl):
        base = (hh % slots) * nk
        return slice(base + sl.start, base + sl.stop)

    def probabilities(hh):
        q = q_of(hh)
        if bounded:
            l8 = jnp.zeros((8, nq), F32)
            for c in range(len(chunks)):
                p = jnp.exp2(scores(hh, q, c))
                l8 = l8 + p.reshape(KC // 8, 8, nq).sum(axis=0)
                p_ref[p_rows(hh, chunks[c]), :] = p.astype(BF16)
            return jnp.sum(l8, axis=0, keepdims=True)
        m = jnp.full((1, nq), -jnp.inf, F32)
        for c in range(len(chunks)):
            s = scores(hh, q, c)
            s_ref[chunks[c], :] = s
            m = jnp.maximum(m, jnp.max(s, axis=0, keepdims=True))
        l = jnp.zeros((1, nq), F32)
        for c in range(len(chunks)):
            p = jnp.exp2(s_ref[chunks[c], :] - m)
            l = l + jnp.sum(p, axis=0, keepdims=True)
            p_ref[p_rows(hh, chunks[c]), :] = p.astype(BF16)
        return l

    def weighted_values(hh, l):
        o = _mm(v_of(hh, slice(0, nk)), p_ref[p_rows(hh, slice(0, nk)), :])
        ot_ref[hh * dv:(hh + 1) * dv, :] = o / l

    sums = {}
    for step in range(n_heads + depth):
        if step < n_heads:
            sums[step] = probabilities(step)
        if step >= depth:
            weighted_values(step - depth, sums.pop(step - depth))


def _count(score_ref, nk, pred):
    cnt = jnp.zeros((COUNT_ROWS, TQ), I32)
    for r in range(0, nk, COUNT_ROWS):
        cnt = jnp.where(pred(score_ref[r:r + COUNT_ROWS, :]), cnt + 1, cnt)
    return jnp.sum(cnt, axis=0, keepdims=True)


def _ordered_to_bits(u, magnitude_mask):
    k = u ^ INT_MIN
    return k ^ ((k >> 31) & magnitude_mask)


def _ordered_pattern_to_float(u):
    return pltpu.bitcast(_ordered_to_bits(u, 0x7FFFFFFF), F32)


def _count_rounded(round_ref, nk, cand):
    assert nk // COUNT_ROWS <= 256
    one, zero = jnp.ones((), BF16), jnp.zeros((), BF16)
    cnt = jnp.zeros((COUNT_ROWS, TQ), BF16)
    for r in range(0, nk, COUNT_ROWS):
        cnt = cnt + jnp.where(round_ref[r:r + COUNT_ROWS, :] >= cand, one, zero)
    return jnp.sum(cnt.astype(F32), axis=0, keepdims=True)


def _select_topk(nk, q_pos, row, chunks, qib_ref, wt_ref, ki_ref, score_ref, round_ref, emit):
    for c, sl in enumerate(chunks):
        ki_c = ki_ref[sl, :]
        acc = jnp.zeros((KC, TQ), F32)
        for hh in range(IDX_HEADS):
            d = _mm(ki_c, qib_ref[0, :, hh * TQ:(hh + 1) * TQ])
            acc = acc + jnp.maximum(d, 0.0) * wt_ref[hh:hh + 1, :]
        score = jnp.where(row + c * KC <= q_pos, acc, NEG)
        score_ref[sl, :] = score
        round_ref[sl, :] = score.astype(BF16)

    def coarse(i, c_u):
        cand_u = c_u | jnp.left_shift(jnp.int32(1), 31 - i)
        cand = pltpu.bitcast(_ordered_to_bits(cand_u, 0x7FFF0000), F32).astype(BF16)
        return jnp.where(_count_rounded(round_ref, nk, cand) >= TOPK_MAX, cand_u, c_u)

    c_u = lax.fori_loop(0, 16, coarse, jnp.zeros((1, TQ), I32))
    pred_bits = _ordered_to_bits(c_u - (1 << 16), 0x7FFF0000)
    base_u = (pred_bits ^ ((pred_bits >> 31) & 0x7FFFFFFF)) ^ INT_MIN

    def fine(i, carry):
        off, cnt_t = carry
        cand_off = off | jnp.left_shift(jnp.int32(1), 16 - i)
        cand = _ordered_pattern_to_float(base_u + cand_off)
        cnt = _count(score_ref, nk, lambda x: x >= cand)
        ok = cnt >= TOPK_MAX
        return jnp.where(ok, cand_off, off), jnp.where(ok, cnt, cnt_t)

    off, cnt_t = lax.fori_loop(0, 17, fine, (jnp.zeros((1, TQ), I32), jnp.full((1, TQ), nk, I32)))
    thr = _ordered_pattern_to_float(base_u + off)
    split_ties = jnp.max(jnp.where(cnt_t > TOPK_MAX, 1, 0)) > 0

    @pl.when(jnp.logical_not(split_ties))
    def _():
        for c, sl in enumerate(chunks):
            emit(sl, (score_ref[sl, :] >= thr) & (row + c * KC <= q_pos))

    @pl.when(split_ties)
    def _():
        room = (TOPK_MAX - _count(score_ref, nk, lambda x: x > thr)).astype(F32)
        tri = lax.broadcasted_iota(I32, (KC, KC), 0) >= lax.broadcasted_iota(I32, (KC, KC), 1)
        tri = jnp.where(tri, 1.0, 0.0).astype(BF16)
        running = jnp.zeros((1, TQ), F32)
        for c, sl in enumerate(chunks):
            x = score_ref[sl, :]
            tie = x == thr
            rank = _mm(tri, jnp.where(tie, 1.0, 0.0).astype(BF16)) + running
            running = rank[KC - 1:KC, :]
            emit(sl, ((x > thr) | (tie & (rank <= room))) & (row + c * KC <= q_pos))


def _dsa_body(nk, start, bounded, qat_ref, qib_ref, wt_ref, ka_ref, vat_ref, ki_ref, oa_ref,
              score_ref, bias_ref, s_ref, p_ref, ot_ref):
    q_pos = start + lax.broadcasted_iota(I32, (1, TQ), 1)
    row = lax.broadcasted_iota(I32, (KC, TQ), 0)
    chunks = [slice(c * KC, (c + 1) * KC) for c in range(nk // KC)]

    def emit_bias(sl, keep):
        bias_ref[sl, :] = jnp.where(keep, 0.0, NEG)

    if nk <= TOPK_MAX:
        for c, sl in enumerate(chunks):
            emit_bias(sl, row + c * KC <= q_pos)
    else:
        _select_topk(nk, q_pos, row, chunks, qib_ref, wt_ref, ki_ref, score_ref, p_ref, emit_bias)

    def q_of(hh):
        return qat_ref[hh * LANES:(hh + 1) * LANES, :]

    def k_of(hh, sl):
        return ka_ref[hh // 2, sl, :]

    def v_of(hh, sl):
        return vat_ref[hh * A_HEAD_DIM:(hh + 1) * A_HEAD_DIM, sl]

    _attend(nk, A_HEADS, A_HEAD_DIM, q_of, k_of, v_of, lambda c: bias_ref[chunks[c], :], bounded,
            s_ref, p_ref, ot_ref)
    oa_ref[...] = ot_ref[...].T


def _dsa_kernel(qat_ref, qib_ref, wt_ref, ka_ref, vat_ref, ki_ref, oa_ref,
                score_ref, bias_ref, s_ref, p_ref, ot_ref, *, seq, bounded):
    qb = pl.program_id(1)
    for cls in range(seq // TQ):
        @pl.when(qb == cls)
        def _():
            _dsa_body(TQ * (cls + 1), cls * TQ, bounded, qat_ref, qib_ref, wt_ref, ka_ref, vat_ref,
                      ki_ref, oa_ref, score_ref, bias_ref, s_ref, p_ref, ot_ref)


def _mla_body(nk, start, bounded, qbt_ref, kb_ref, vbt_ref, ob_ref, bias_ref, s_ref, p_ref, ot_ref):
    last = nk // KC - 1
    q_pos = start + lax.broadcasted_iota(I32, (1, TQ), 1)
    row = lax.broadcasted_iota(I32, (KC, TQ), 0)
    bias_ref[0:KC, :] = jnp.where(row + last * KC <= q_pos, 0.0, NEG)

    def q_of(hh):
        return qbt_ref[hh * LANES:(hh + 1) * LANES, :]

    def k_of(hh, sl):
        return kb_ref[sl, hh * LANES:(hh + 1) * LANES]

    def v_of(hh, sl):
        return vbt_ref[hh * B_VDIM:(hh + 1) * B_VDIM, sl]

    _attend(nk, B_HEADS, B_VDIM, q_of, k_of, v_of, lambda c: bias_ref[0:KC, :] if c == last else None,
            bounded, s_ref, p_ref, ot_ref)
    ob_ref[...] = ot_ref[...].T


def _mla_kernel(qbt_ref, kb_ref, vbt_ref, ob_ref, bias_ref, s_ref, p_ref, ot_ref, *, seq, bounded):
    qb = pl.program_id(1)
    for cls in range(seq // TQ):
        @pl.when(qb == cls)
        def _():
            _mla_body(TQ * (cls + 1), cls * TQ, bounded, qbt_ref, kb_ref, vbt_ref, ob_ref,
                      bias_ref, s_ref, p_ref, ot_ref)


def _mem_attn_kernel(qmt_ref, km_ref, vmt_ref, om_ref, s_ref, p_ref, ot_ref, *, mem_len, bounded):
    def q_of(hh):
        return qmt_ref[hh * M_HEAD_DIM:(hh + 1) * M_HEAD_DIM, :]

    def k_of(hh, sl):
        return km_ref[sl, hh * M_HEAD_DIM:(hh + 1) * M_HEAD_DIM]

    def v_of(hh, sl):
        return vmt_ref[hh * M_HEAD_DIM:(hh + 1) * M_HEAD_DIM, sl]

    _attend(mem_len, M_HEADS, M_HEAD_DIM, q_of, k_of, v_of, lambda c: None, bounded, s_ref, p_ref, ot_ref)
    om_ref[...] = ot_ref[...].T


def _final_kernel(x_ref, oa_ref, ob_ref, om_ref, gn_ref, wz_ref, wg_ref, wb_ref, wo_ref, out_ref):
    x = x_ref[...]
    h = _rms_lanes(x, gn_ref[...]).astype(BF16)
    merged = jnp.zeros((TM, D_MODEL), F32)
    for n, o_ref in enumerate((oa_ref, ob_ref, om_ref)):
        z = _mm(h, wz_ref[n])
        y = (o_ref[...] * (z * jax.nn.sigmoid(z))).astype(BF16)
        branch = _mm(y, wb_ref[n])
        gate = jax.nn.sigmoid(_mm(h, wg_ref[:, n * D_MODEL:(n + 1) * D_MODEL]))
        merged = merged + gate * branch
    out_ref[...] = x + _mm(merged.astype(BF16), wo_ref[...])


def _full(shape):
    return pl.BlockSpec(shape, lambda *_: (0,) * len(shape), pipeline_mode=pl.Buffered(1))


def _params(n_axes):
    return pltpu.CompilerParams(dimension_semantics=("arbitrary",) * n_axes,
                                vmem_limit_bytes=VMEM_LIMIT)


def _rope_expansion(half, period, lo):
    j = np.arange(LANES) % period
    first = (j >= lo) & (j < lo + half)
    second = (j >= lo + half) & (j < lo + 2 * half)
    freq = np.where(first, j - lo, j - lo - half)
    mat = np.zeros((LANES, 3 * LANES), np.float32)
    for lane in np.nonzero(first | second)[0]:
        for term in range(3):
            mat[term * half + freq[lane], lane] = 1.0
            col = LANES + lane if first[lane] else 2 * LANES + lane
            mat[(3 + term) * half + freq[lane], col] = -1.0 if first[lane] else 1.0
    nonrot = np.where(first | second, 0.0, 1.0).astype(np.float32).reshape(1, LANES)
    return jnp.asarray(mat, BF16), jnp.asarray(nonrot)


def _group_matrix(group):
    g = np.arange(LANES) // group
    member = (g[:, None] == g[None, :]).astype(np.float32)
    return jnp.asarray(np.concatenate([member, member], axis=0), BF16)


def kernel(x, mem, positions, g_norm, w_in, g_qn_a, g_kn_a, g_cq, g_ckv, w_uq, w_ukv, g_qn_b, g_kn_b,
           g_mem, w_mem_kv, g_qn_m, g_kn_m, w_branch, w_out):
    b, s, d = x.shape
    m_len = mem.shape[1]
    n = b * s
    nq = s // TQ
    assert d == D_MODEL and s % TQ == 0 and TQ == KC and TM % TQ == 0 and n % TM == 0 and m_len % KC == 0
    assert g_norm.shape[0] == 1, "single-layer block"

    w = w_in[0]
    off = np.cumsum([0, 512, 512, 512, 512, IDX_DIM, IDX_HEADS, BRANCH_WIDTH, B_Q_RANK, B_KV_RANK, B_ROPE,
                     BRANCH_WIDTH, M_HEADS * M_HEAD_DIM, BRANCH_WIDTH, N_BRANCH * D_MODEL])
    seg = [w[:, off[i]:off[i + 1]] for i in range(14)]
    (w_qa, w_ka, w_va, w_qi, w_ki, w_wi, w_za, w_cq, w_ckv, w_kr, w_zb, w_qm, w_zm, w_gate) = seg
    bf = lambda a: a.astype(BF16)
    wqa_t, wqi_t, wva_t, wqm_t = bf(w_qa.T), bf(w_qi.T), bf(w_va.T), bf(w_qm.T)
    wwi_t = bf(jnp.pad(w_wi.T, ((0, 16 - IDX_HEADS), (0, 0))))
    wki_p = bf(jnp.pad(w_ki, ((0, 0), (0, LANES - IDX_DIM))))
    wkr_p = bf(jnp.pad(w_kr, ((0, 0), (B_NOPE, LANES - B_QK))))
    wuq_t = bf(jnp.pad(w_uq[0].reshape(B_Q_RANK, B_HEADS, B_QK), ((0, 0), (0, 0), (0, LANES - B_QK)))
               .reshape(B_Q_RANK, B_HEADS * LANES).T)
    ukv = w_ukv[0].reshape(B_KV_RANK, B_HEADS, B_NOPE + B_VDIM)
    wuk_p = bf(jnp.pad(ukv[:, :, :B_NOPE], ((0, 0), (0, 0), (0, LANES - B_NOPE)))
               .reshape(B_KV_RANK, B_HEADS * LANES))
    wuv_t = bf(ukv[:, :, B_NOPE:].reshape(B_KV_RANK, B_HEADS * B_VDIM).T)
    wmk = bf(w_mem_kv[0][:, :M_HEADS * M_HEAD_DIM])
    wmv_t = bf(w_mem_kv[0][:, M_HEADS * M_HEAD_DIM:].T)
    wz = bf(jnp.stack([w_za, w_zb, w_zm]))
    wg = bf(w_gate)
    wb = bf(w_branch[0])
    wo = bf(w_out[0])

    gn = g_norm[0].reshape(1, D_MODEL)
    gqa_c = g_qn_a[0].reshape(A_HEAD_DIM, 1)
    gka_r = jnp.tile(g_kn_a[0], LANES // A_HEAD_DIM).reshape(1, LANES)
    gcq_r = g_cq[0].reshape(1, B_Q_RANK)
    gckv_r = g_ckv[0].reshape(1, B_KV_RANK)
    gqb_c = jnp.pad(g_qn_b[0], (0, LANES - B_QK)).reshape(LANES, 1)
    gkb_r = jnp.pad(g_kn_b[0], (0, LANES - B_QK)).reshape(1, LANES)
    gqm_c = g_qn_m[0].reshape(M_HEAD_DIM, 1)
    gkm_r = g_kn_m[0].reshape(1, M_HEAD_DIM)
    gmem_r = g_mem[0].reshape(1, D_MODEL)

    inv_a = ROPE_THETA ** (-(jnp.arange(0, A_ROT, 2, dtype=F32) / A_ROT))
    inv_b = ROPE_THETA ** (-(jnp.arange(0, B_ROPE, 2, dtype=F32) / B_ROPE))
    expand_a, nonrot_a = _rope_expansion(A_ROT // 2, A_HEAD_DIM, 0)
    expand_b, nonrot_b = _rope_expansion(B_ROPE // 2, LANES, B_NOPE)

    x2 = x.reshape(n, d)
    pos_r = positions.reshape(1, n)
    tile = lambda width: pl.BlockSpec((TM, width), lambda i: (i, 0))
    tile_t = lambda rows: pl.BlockSpec((rows, TM), lambda i: (0, i))
    pos_spec = pl.BlockSpec((1, TM), lambda i: (0, i))

    a_in = [wqa_t, wqi_t, wwi_t, bf(w_ka), wva_t, wki_p, gqa_c, gka_r,
            inv_a.reshape(-1, 1), expand_a, nonrot_a, _group_matrix(A_HEAD_DIM)]
    b_in = [bf(w_cq), bf(w_ckv), wkr_p, wqm_t, wuq_t, wuk_p, wuv_t,
            gcq_r, gckv_r, gqb_c, gkb_r, gqm_c, inv_b.reshape(-1, 1), expand_b, nonrot_b,
            _group_matrix(LANES)]
    assert len(a_in) == N_PROJ_A_IN and len(b_in) == N_PROJ_B_IN
    qat, qib, wt, ka, vat, ki, qbt, kb, vbt, qmt = pl.pallas_call(
        _proj_kernel,
        grid=(n // TM,),
        in_specs=[tile(d), pos_spec, _full(gn.shape)] + [_full(a.shape) for a in a_in + b_in],
        out_specs=[tile_t(A_HEADS * LANES), pl.BlockSpec((TM // TQ, LANES, IDX_HEADS * TQ), lambda i: (i, 0, 0)),
                   tile_t(IDX_HEADS), pl.BlockSpec((512 // LANES, TM, LANES), lambda i: (0, i, 0)),
                   tile_t(512), tile(LANES),
                   tile_t(B_HEADS * LANES), tile(B_HEADS * LANES), tile_t(512), tile_t(512)],
        out_shape=[jax.ShapeDtypeStruct((A_HEADS * LANES, n), BF16),
                   jax.ShapeDtypeStruct((n // TQ, LANES, IDX_HEADS * TQ), BF16),
                   jax.ShapeDtypeStruct((IDX_HEADS, n), F32),
                   jax.ShapeDtypeStruct((512 // LANES, n, LANES), BF16),
                   jax.ShapeDtypeStruct((512, n), BF16),
                   jax.ShapeDtypeStruct((n, LANES), BF16),
                   jax.ShapeDtypeStruct((B_HEADS * LANES, n), BF16),
                   jax.ShapeDtypeStruct((n, B_HEADS * LANES), BF16),
                   jax.ShapeDtypeStruct((512, n), BF16),
                   jax.ShapeDtypeStruct((512, n), BF16)],
        compiler_params=_params(1), name="proj",
    )(x2, pos_r, gn, *a_in, *b_in)

    km, vmt = pl.pallas_call(
        _mem_kv_kernel,
        grid=(b,),
        in_specs=[pl.BlockSpec((m_len, d), lambda i: (i, 0)), _full(gmem_r.shape), _full(wmk.shape),
                  _full(wmv_t.shape), _full(gkm_r.shape)],
        out_specs=[pl.BlockSpec((m_len, 512), lambda i: (i, 0)), pl.BlockSpec((512, m_len), lambda i: (0, i))],
        out_shape=[jax.ShapeDtypeStruct((b * m_len, 512), BF16), jax.ShapeDtypeStruct((512, b * m_len), BF16)],
        compiler_params=_params(1), name="mem_kv",
    )(mem.reshape(b * m_len, d), gmem_r, wmk, wmv_t, gkm_r)

    qcol = lambda rows: pl.BlockSpec((rows, TQ), lambda bi, qi: (0, bi * nq + qi))
    seq_rows = lambda width: pl.BlockSpec((s, width), lambda bi, qi: (bi, 0))
    seq_cols = lambda rows: pl.BlockSpec((rows, s), lambda bi, qi: (0, bi))
    o_spec = pl.BlockSpec((TQ, 512), lambda bi, qi: (bi * nq + qi, 0))
    o_shape = jax.ShapeDtypeStruct((n, 512), F32)
    attn_scratch = [pltpu.VMEM((s, TQ), F32), pltpu.VMEM((s, TQ), F32), pltpu.VMEM((2 * s, TQ), BF16),
                    pltpu.VMEM((512, TQ), F32)]

    def dsa(bounded):
        return pl.pallas_call(
            functools.partial(_dsa_kernel, seq=s, bounded=bounded),
            grid=(b, nq),
            in_specs=[qcol(A_HEADS * LANES),
                      pl.BlockSpec((1, LANES, IDX_HEADS * TQ), lambda bi, qi: (bi * nq + qi, 0, 0)),
                      qcol(IDX_HEADS), pl.BlockSpec((512 // LANES, s, LANES), lambda bi, qi: (0, bi, 0)),
                      seq_cols(512), seq_rows(LANES)],
            out_specs=o_spec, out_shape=o_shape,
            scratch_shapes=[pltpu.VMEM((s, TQ), F32)] + attn_scratch,
            compiler_params=_params(2), name="dsa" if bounded else "dsa_general")

    def mla(bounded):
        return pl.pallas_call(
            functools.partial(_mla_kernel, seq=s, bounded=bounded),
            grid=(b, nq),
            in_specs=[qcol(B_HEADS * LANES), seq_rows(B_HEADS * LANES), seq_cols(512)],
            out_specs=o_spec, out_shape=o_shape,
            scratch_shapes=attn_scratch,
            compiler_params=_params(2), name="mla" if bounded else "mla_general")

    def mem_attn(bounded):
        return pl.pallas_call(
            functools.partial(_mem_attn_kernel, mem_len=m_len, bounded=bounded),
            grid=(b, nq),
            in_specs=[qcol(512), pl.BlockSpec((m_len, 512), lambda bi, qi: (bi, 0)),
                      pl.BlockSpec((512, m_len), lambda bi, qi: (0, bi))],
            out_specs=o_spec, out_shape=o_shape,
            scratch_shapes=[pltpu.VMEM((m_len, TQ), F32), pltpu.VMEM((M_HEADS * m_len, TQ), BF16),
                            pltpu.VMEM((512, TQ), F32)],
            compiler_params=_params(2), name="mem_attn" if bounded else "mem_attn_general")

    def score_bound(gq, gk, dim):
        return dim ** 0.5 * LOG2E * 1.02 * jnp.max(jnp.abs(gq)) * jnp.max(jnp.abs(gk))

    def attention(bounded):
        def run(*ops):
            return (dsa(bounded)(*ops[:6]), mla(bounded)(*ops[6:9]), mem_attn(bounded)(*ops[9:]))
        return run

    worst = jnp.maximum(jnp.maximum(score_bound(g_qn_a, g_kn_a, A_HEAD_DIM), score_bound(g_qn_b, g_kn_b, B_QK)),
                        score_bound(g_qn_m, g_kn_m, M_HEAD_DIM))
    oa, ob, om = lax.cond(worst <= BOUNDED_SCORE_LIMIT, attention(True), attention(False),
                          qat, qib, wt, ka, vat, ki, qbt, kb, vbt, qmt, km, vmt)

    out = pl.pallas_call(
        _final_kernel,
        grid=(n // TM,),
        in_specs=[tile(d), tile(512), tile(512), tile(512), _full(gn.shape), _full(wz.shape), _full(wg.shape),
                  _full(wb.shape), _full(wo.shape)],
        out_specs=tile(d), out_shape=jax.ShapeDtypeStruct((n, d), x.dtype),
        compiler_params=_params(1), name="final",
    )(x2, oa, ob, om, gn, wz, wg, wb, wo)
    return out.reshape(b, s, d)
```

```python
import functools

import numpy as np
import jax
import jax.numpy as jnp
from jax import lax
from jax.experimental import pallas as pl
from jax.experimental.pallas import tpu as pltpu

F32 = jnp.float32
BF16 = jnp.bfloat16
I32 = jnp.int32

D_MODEL = 1024
ROPE_THETA = 500000.0
EPS = 1e-6
NEG = -1e30
N_BRANCH = 3
BRANCH_WIDTH = 512
A_HEADS = 8
A_HEAD_DIM = 64
A_ROT = A_HEAD_DIM // 4
IDX_HEADS = 8
IDX_DIM = 64
TOPK_MAX = 256
B_HEADS = 8
B_NOPE = 64
B_ROPE = 32
B_VDIM = 64
B_QK = B_NOPE + B_ROPE
B_Q_RANK = 384
B_KV_RANK = 256
M_HEADS = 4
M_HEAD_DIM = 128

LANES = 128
TM = 512
TQ = 256
KC = 256
COUNT_ROWS = 64
VMEM_LIMIT = 56 * 1024 * 1024
INT_MIN = -2 ** 31
LOG2E = 1.4426950408889634
BOUNDED_SCORE_LIMIT = 32.0


def _nt(a, b):
    return lax.dot_general(a, b, (((1,), (1,)), ((), ())), preferred_element_type=F32)


def _mm(a, b):
    return jnp.dot(a, b, preferred_element_type=F32)


def _rms_lanes(xf, g_row, n=None):
    n = xf.shape[-1] if n is None else n
    ms = jnp.sum(xf * xf, axis=-1, keepdims=True) / n
    return xf * lax.rsqrt(ms + EPS) * g_row


def _rms_rows(blk, g_col, n=None):
    n = blk.shape[0] if n is None else n
    ms = jnp.sum(blk * blk, axis=0, keepdims=True) / n
    return blk * lax.rsqrt(ms + EPS) * g_col


def _rope_rows(blk, lo, half, cos_t, sin_t):
    x1 = blk[lo:lo + half]
    x2 = blk[lo + half:lo + 2 * half]
    parts = []
    if lo:
        parts.append(blk[:lo])
    parts += [x1 * cos_t - x2 * sin_t, x2 * cos_t + x1 * sin_t]
    if lo + 2 * half < blk.shape[0]:
        parts.append(blk[lo + 2 * half:])
    return jnp.concatenate(parts, axis=0)


def _token_major(blocks):
    rows = sum(blk.shape[0] for blk in blocks)
    if rows < LANES:
        blocks = list(blocks) + [jnp.zeros((LANES - rows, blocks[0].shape[1]), F32)]
    return jnp.concatenate(blocks, axis=0).T.astype(BF16)


N_PROJ_A_IN, N_PROJ_A_OUT = 9, 6
N_PROJ_B_IN = 13


def _proj_kernel(x_ref, posr_ref, gn_ref, *refs):
    h = _rms_lanes(x_ref[...], gn_ref[...]).astype(BF16)
    a_in = refs[:N_PROJ_A_IN]
    b_in = refs[N_PROJ_A_IN:N_PROJ_A_IN + N_PROJ_B_IN]
    outs = refs[N_PROJ_A_IN + N_PROJ_B_IN:]
    _proj_a(h, posr_ref, *a_in, *outs[:N_PROJ_A_OUT])
    _proj_b(h, posr_ref, *b_in, *outs[N_PROJ_A_OUT:])


def _proj_a(h, posr_ref, wqa_ref, wqi_ref, wwi_ref, wka_ref, wva_ref, wki_ref, gqa_ref, gka_ref, invc_ref,
            qat_ref, qib_ref, wt_ref, ka_ref, vat_ref, ki_ref):
    ang_t = invc_ref[...] * posr_ref[...].astype(F32)
    cos_t, sin_t = jnp.cos(ang_t), jnp.sin(ang_t)
    half = A_ROT // 2

    qa = _nt(wqa_ref[...], h)
    gq = gqa_ref[...]
    for hh in range(A_HEADS):
        blk = _rms_rows(qa[hh * A_HEAD_DIM:(hh + 1) * A_HEAD_DIM], gq)
        blk = _rope_rows(blk, 0, half, cos_t, sin_t) * (A_HEAD_DIM ** -0.5 * LOG2E)
        own = hh * LANES + (hh % 2) * A_HEAD_DIM
        other = hh * LANES + (1 - hh % 2) * A_HEAD_DIM
        qat_ref[own:own + A_HEAD_DIM, :] = blk.astype(BF16)
        qat_ref[other:other + A_HEAD_DIM, :] = jnp.zeros((A_HEAD_DIM, TM), BF16)

    qi = _nt(wqi_ref[...], h)
    for hh in range(IDX_HEADS):
        blk = _rope_rows(qi[hh * IDX_DIM:(hh + 1) * IDX_DIM], 0, half, cos_t, sin_t)
        blk = (blk * (IDX_DIM ** -0.5)).astype(BF16)
        for j in range(TM // TQ):
            qib_ref[j, 0:IDX_DIM, hh * TQ:(hh + 1) * TQ] = blk[:, j * TQ:(j + 1) * TQ]
    qib_ref[:, IDX_DIM:, :] = jnp.zeros((TM // TQ, LANES - IDX_DIM, IDX_HEADS * TQ), BF16)

    wt_ref[...] = _nt(wwi_ref[...], h)[0:IDX_HEADS] * (IDX_HEADS ** -0.5)

    vat_ref[...] = _nt(wva_ref[...], h).astype(BF16)

    ka = _nt(wka_ref[...], h)
    gk = gka_ref[...]
    for c in range(A_HEADS // 2):
        pair = [_rope_rows(_rms_rows(ka[hh * A_HEAD_DIM:(hh + 1) * A_HEAD_DIM], gk), 0, half, cos_t, sin_t)
                for hh in (2 * c, 2 * c + 1)]
        ka_ref[c, :, :] = _token_major(pair)

    ki = _nt(wki_ref[...], h)
    ki_ref[...] = _token_major([_rope_rows(ki, 0, half, cos_t, sin_t)])


def _proj_b(h, posr_ref, wcq_ref, wckv_ref, wkr_ref, wqm_ref, wuq_ref, wuk_ref, wuv_ref,
            gcq_ref, gckv_ref, gqb_ref, gkb_ref, gqm_ref, invc_ref,
            qbt_ref, kb_ref, vbt_ref, qmt_ref):
    half = B_ROPE // 2
    ang_t = invc_ref[...] * posr_ref[...].astype(F32)
    cos_t, sin_t = jnp.cos(ang_t), jnp.sin(ang_t)

    cq = _rms_lanes(_mm(h, wcq_ref[...]), gcq_ref[...]).astype(BF16)
    qb = _nt(wuq_ref[...], cq)
    gq = gqb_ref[...]
    for hh in range(B_HEADS):
        blk = _rms_rows(qb[hh * LANES:(hh + 1) * LANES], gq, n=B_QK)
        blk = _rope_rows(blk, B_NOPE, half, cos_t, sin_t) * (B_QK ** -0.5 * LOG2E)
        qbt_ref[hh * LANES:(hh + 1) * LANES, :] = blk.astype(BF16)

    ckv = _rms_lanes(_mm(h, wckv_ref[...]), gckv_ref[...]).astype(BF16)
    kn = _nt(wuk_ref[...], ckv)
    kr = _nt(wkr_ref[...], h)
    gk = gkb_ref[...]
    pad = jnp.zeros((LANES - B_QK, TM), F32)
    for hh in range(B_HEADS):
        blk = jnp.concatenate([kn[hh * B_NOPE:(hh + 1) * B_NOPE], kr, pad], axis=0)
        blk = _rope_rows(_rms_rows(blk, gk, n=B_QK), B_NOPE, half, cos_t, sin_t)
        kb_ref[:, hh * LANES:(hh + 1) * LANES] = _token_major([blk])
    vbt_ref[...] = _nt(wuv_ref[...], ckv).astype(BF16)

    qm = _nt(wqm_ref[...], h)
    gm = gqm_ref[...]
    for hh in range(M_HEADS):
        blk = _rms_rows(qm[hh * M_HEAD_DIM:(hh + 1) * M_HEAD_DIM], gm) * (M_HEAD_DIM ** -0.5 * LOG2E)
        qmt_ref[hh * M_HEAD_DIM:(hh + 1) * M_HEAD_DIM, :] = blk.astype(BF16)


def _mem_kv_kernel(mem_ref, gmem_ref, wk_ref, wvt_ref, gkm_ref, km_ref, vmt_ref):
    hm = _rms_lanes(mem_ref[...], gmem_ref[...]).astype(BF16)
    k = _mm(hm, wk_ref[...])
    gk = gkm_ref[...]
    for hh in range(M_HEADS):
        kc = _rms_lanes(k[:, hh * M_HEAD_DIM:(hh + 1) * M_HEAD_DIM], gk)
        km_ref[:, hh * M_HEAD_DIM:(hh + 1) * M_HEAD_DIM] = kc.astype(BF16)
    vmt_ref[...] = _nt(wvt_ref[...], hm).astype(BF16)


def _attend(nk, n_heads, dv, q_of, k_of, v_of, bias_of, bounded, s_ref, p_ref, ot_ref):
    nq = ot_ref.shape[1]
    chunks = [slice(c * KC, (c + 1) * KC) for c in range(nk // KC)]

    def scores(hh, q, c):
        s = _mm(k_of(hh, chunks[c]), q)
        b = bias_of(c)
        return s if b is None else s + b

    depth = max(1, min(n_heads - 1, 8 // len(chunks)))
    slots = depth + 1

    def p_rows(hh, sl):
        base = (hh % slots) * nk
        return slice(base + sl.start, base + sl.stop)

    def probabilities(hh):
        q = q_of(hh)
        if bounded:
            l8 = jnp.zeros((8, nq), F32)
            for c in range(len(chunks)):
                p = jnp.exp2(scores(hh, q, c))
                l8 = l8 + p.reshape(KC // 8, 8, nq).sum(axis=0)
                p_ref[p_rows(hh, chunks[c]), :] = p.astype(BF16)
            return jnp.sum(l8, axis=0, keepdims=True)
        m = jnp.full((1, nq), -jnp.inf, F32)
        for c in range(len(chunks)):
            s = scores(hh, q, c)
            s_ref[chunks[c], :] = s
            m = jnp.maximum(m, jnp.max(s, axis=0, keepdims=True))
        l = jnp.zeros((1, nq), F32)
        for c in range(len(chunks)):
            p = jnp.exp2(s_ref[chunks[c], :] - m)
            l = l + jnp.sum(p, axis=0, keepdims=True)
            p_ref[p_rows(hh, chunks[c]), :] = p.astype(BF16)
        return l

    def weighted_values(hh, l):
        o = _mm(v_of(hh, slice(0, nk)), p_ref[p_rows(hh, slice(0, nk)), :])
        ot_ref[hh * dv:(hh + 1) * dv, :] = o / l

    sums = {}
    for step in range(n_heads + depth):
        if step < n_heads:
            sums[step] = probabilities(step)
        if step >= depth:
            weighted_values(step - depth, sums.pop(step - depth))


def _count(score_ref, nk, pred):
    cnt = jnp.zeros((COUNT_ROWS, TQ), I32)
    for r in range(0, nk, COUNT_ROWS):
        cnt = jnp.where(pred(score_ref[r:r + COUNT_ROWS, :]), cnt + 1, cnt)
    return jnp.sum(cnt, axis=0, keepdims=True)


def _ordered_to_bits(u, magnitude_mask):
    k = u ^ INT_MIN
    return k ^ ((k >> 31) & magnitude_mask)


def _ordered_pattern_to_float(u):
    return pltpu.bitcast(_ordered_to_bits(u, 0x7FFFFFFF), F32)


def _count_rounded(round_ref, nk, cand):
    assert nk // COUNT_ROWS <= 256
    one, zero = jnp.ones((), BF16), jnp.zeros((), BF16)
    cnt = jnp.zeros((COUNT_ROWS, TQ), BF16)
    for r in range(0, nk, COUNT_ROWS):
        cnt = cnt + jnp.where(round_ref[r:r + COUNT_ROWS, :] >= cand, one, zero)
    return jnp.sum(cnt.astype(F32), axis=0, keepdims=True)


def _select_topk(nk, q_pos, row, chunks, qib_ref, wt_ref, ki_ref, score_ref, round_ref, emit):
    for c, sl in enumerate(chunks):
        ki_c = ki_ref[sl, :]
        acc = jnp.zeros((KC, TQ), F32)
        for hh in range(IDX_HEADS):
            d = _mm(ki_c, qib_ref[0, :, hh * TQ:(hh + 1) * TQ])
            acc = acc + jnp.maximum(d, 0.0) * wt_ref[hh:hh + 1, :]
        score = jnp.where(row + c * KC <= q_pos, acc, NEG)
        score_ref[sl, :] = score
        round_ref[sl, :] = score.astype(BF16)

    def coarse(i, c_u):
        cand_u = c_u | jnp.left_shift(jnp.int32(1), 31 - i)
        cand = pltpu.bitcast(_ordered_to_bits(cand_u, 0x7FFF0000), F32).astype(BF16)
        return jnp.where(_count_rounded(round_ref, nk, cand) >= TOPK_MAX, cand_u, c_u)

    c_u = lax.fori_loop(0, 16, coarse, jnp.zeros((1, TQ), I32))
    pred_bits = _ordered_to_bits(c_u - (1 << 16), 0x7FFF0000)
    base_u = (pred_bits ^ ((pred_bits >> 31) & 0x7FFFFFFF)) ^ INT_MIN

    def fine(i, carry):
        off, cnt_t = carry
        cand_off = off | jnp.left_shift(jnp.int32(1), 16 - i)
        cand = _ordered_pattern_to_float(base_u + cand_off)
        cnt = _count(score_ref, nk, lambda x: x >= cand)
        ok = cnt >= TOPK_MAX
        return jnp.where(ok, cand_off, off), jnp.where(ok, cnt, cnt_t)

    off, cnt_t = lax.fori_loop(0, 17, fine, (jnp.zeros((1, TQ), I32), jnp.full((1, TQ), nk, I32)))
    thr = _ordered_pattern_to_float(base_u + off)
    split_ties = jnp.max(jnp.where(cnt_t > TOPK_MAX, 1, 0)) > 0

    @pl.when(jnp.logical_not(split_ties))
    def _():
        for c, sl in enumerate(chunks):
            emit(sl, (score_ref[sl, :] >= thr) & (row + c * KC <= q_pos))

    @pl.when(split_ties)
    def _():
        room = (TOPK_MAX - _count(score_ref, nk, lambda x: x > thr)).astype(F32)
        tri = lax.broadcasted_iota(I32, (KC, KC), 0) >= lax.broadcasted_iota(I32, (KC, KC), 1)
        tri = jnp.where(tri, 1.0, 0.0).astype(BF16)
        running = jnp.zeros((1, TQ), F32)
        for c, sl in enumerate(chunks):
            x = score_ref[sl, :]
            tie = x == thr
            rank = _mm(tri, jnp.where(tie, 1.0, 0.0).astype(BF16)) + running
            running = rank[KC - 1:KC, :]
            emit(sl, ((x > thr) | (tie & (rank <= room))) & (row + c * KC <= q_pos))


def _dsa_body(nk, start, bounded, qat_ref, qib_ref, wt_ref, ka_ref, vat_ref, ki_ref, oa_ref,
              score_ref, bias_ref, s_ref, p_ref, ot_ref):
    q_pos = start + lax.broadcasted_iota(I32, (1, TQ), 1)
    row = lax.broadcasted_iota(I32, (KC, TQ), 0)
    chunks = [slice(c * KC, (c + 1) * KC) for c in range(nk // KC)]

    def emit_bias(sl, keep):
        bias_ref[sl, :] = jnp.where(keep, 0.0, NEG)

    if nk <= TOPK_MAX:
        for c, sl in enumerate(chunks):
            emit_bias(sl, row + c * KC <= q_pos)
    else:
        _select_topk(nk, q_pos, row, chunks, qib_ref, wt_ref, ki_ref, score_ref, p_ref, emit_bias)

    def q_of(hh):
        return qat_ref[hh * LANES:(hh + 1) * LANES, :]

    def k_of(hh, sl):
        return ka_ref[hh // 2, sl, :]

    def v_of(hh, sl):
        return vat_ref[hh * A_HEAD_DIM:(hh + 1) * A_HEAD_DIM, sl]

    _attend(nk, A_HEADS, A_HEAD_DIM, q_of, k_of, v_of, lambda c: bias_ref[chunks[c], :], bounded,
            s_ref, p_ref, ot_ref)
    oa_ref[...] = ot_ref[...].T


def _dsa_kernel(qat_ref, qib_ref, wt_ref, ka_ref, vat_ref, ki_ref, oa_ref,
                score_ref, bias_ref, s_ref, p_ref, ot_ref, *, seq, bounded):
    qb = pl.program_id(1)
    for cls in range(seq // TQ):
        @pl.when(qb == cls)
        def _():
            _dsa_body(TQ * (cls + 1), cls * TQ, bounded, qat_ref, qib_ref, wt_ref, ka_ref, vat_ref,
                      ki_ref, oa_ref, score_ref, bias_ref, s_ref, p_ref, ot_ref)


def _mla_body(nk, start, bounded, qbt_ref, kb_ref, vbt_ref, ob_ref, bias_ref, s_ref, p_ref, ot_ref):
    last = nk // KC - 1
    q_pos = start + lax.broadcasted_iota(I32, (1, TQ), 1)
    row = lax.broadcasted_iota(I32, (KC, TQ), 0)
    bias_ref[0:KC, :] = jnp.where(row + last * KC <= q_pos, 0.0, NEG)

    def q_of(hh):
        return qbt_ref[hh * LANES:(hh + 1) * LANES, :]

    def k_of(hh, sl):
        return kb_ref[sl, hh * LANES:(hh + 1) * LANES]

    def v_of(hh, sl):
        return vbt_ref[hh * B_VDIM:(hh + 1) * B_VDIM, sl]

    _attend(nk, B_HEADS, B_VDIM, q_of, k_of, v_of, lambda c: bias_ref[0:KC, :] if c == last else None,
            bounded, s_ref, p_ref, ot_ref)
    ob_ref[...] = ot_ref[...].T


def _mla_kernel(qbt_ref, kb_ref, vbt_ref, ob_ref, bias_ref, s_ref, p_ref, ot_ref, *, seq, bounded):
    qb = pl.program_id(1)
    for cls in range(seq // TQ):
        @pl.when(qb == cls)
        def _():
            _mla_body(TQ * (cls + 1), cls * TQ, bounded, qbt_ref, kb_ref, vbt_ref, ob_ref,
                      bias_ref, s_ref, p_ref, ot_ref)


def _mem_attn_kernel(qmt_ref, km_ref, vmt_ref, om_ref, s_ref, p_ref, ot_ref, *, mem_len, bounded):
    def q_of(hh):
        return qmt_ref[hh * M_HEAD_DIM:(hh + 1) * M_HEAD_DIM, :]

    def k_of(hh, sl):
        return km_ref[sl, hh * M_HEAD_DIM:(hh + 1) * M_HEAD_DIM]

    def v_of(hh, sl):
        return vmt_ref[hh * M_HEAD_DIM:(hh + 1) * M_HEAD_DIM, sl]

    _attend(mem_len, M_HEADS, M_HEAD_DIM, q_of, k_of, v_of, lambda c: None, bounded, s_ref, p_ref, ot_ref)
    om_ref[...] = ot_ref[...].T


def _final_kernel(x_ref, oa_ref, ob_ref, om_ref, gn_ref, wz_ref, wg_ref, wb_ref, wo_ref, out_ref):
    x = x_ref[...]
    h = _rms_lanes(x, gn_ref[...]).astype(BF16)
    merged = jnp.zeros((TM, D_MODEL), F32)
    for n, o_ref in enumerate((oa_ref, ob_ref, om_ref)):
        z = _mm(h, wz_ref[n])
        y = (o_ref[...] * (z * jax.nn.sigmoid(z))).astype(BF16)
        branch = _mm(y, wb_ref[n])
        gate = jax.nn.sigmoid(_mm(h, wg_ref[:, n * D_MODEL:(n + 1) * D_MODEL]))
        merged = merged + gate * branch
    out_ref[...] = x + _mm(merged.astype(BF16), wo_ref[...])


def _full(shape):
    return pl.BlockSpec(shape, lambda *_: (0,) * len(shape), pipeline_mode=pl.Buffered(1))


def _params(n_axes):
    return pltpu.CompilerParams(dimension_semantics=("arbitrary",) * n_axes,
                                vmem_limit_bytes=VMEM_LIMIT)


def kernel(x, mem, positions, g_norm, w_in, g_qn_a, g_kn_a, g_cq, g_ckv, w_uq, w_ukv, g_qn_b, g_kn_b,
           g_mem, w_mem_kv, g_qn_m, g_kn_m, w_branch, w_out):
    b, s, d = x.shape
    m_len = mem.shape[1]
    n = b * s
    nq = s // TQ
    assert d == D_MODEL and s % TQ == 0 and TQ == KC and TM % TQ == 0 and n % TM == 0 and m_len % KC == 0
    assert g_norm.shape[0] == 1, "single-layer block"

    w = w_in[0]
    off = np.cumsum([0, 512, 512, 512, 512, IDX_DIM, IDX_HEADS, BRANCH_WIDTH, B_Q_RANK, B_KV_RANK, B_ROPE,
                     BRANCH_WIDTH, M_HEADS * M_HEAD_DIM, BRANCH_WIDTH, N_BRANCH * D_MODEL])
    seg = [w[:, off[i]:off[i + 1]] for i in range(14)]
    (w_qa, w_ka, w_va, w_qi, w_ki, w_wi, w_za, w_cq, w_ckv, w_kr, w_zb, w_qm, w_zm, w_gate) = seg
    bf = lambda a: a.astype(BF16)
    wqa_t, wqi_t, wva_t, wqm_t = bf(w_qa.T), bf(w_qi.T), bf(w_va.T), bf(w_qm.T)
    wka_t, wki_t, wkr_t = bf(w_ka.T), bf(w_ki.T), bf(w_kr.T)
    wwi_t = bf(jnp.pad(w_wi.T, ((0, 16 - IDX_HEADS), (0, 0))))
    wuq_t = bf(jnp.pad(w_uq[0].reshape(B_Q_RANK, B_HEADS, B_QK), ((0, 0), (0, 0), (0, LANES - B_QK)))
               .reshape(B_Q_RANK, B_HEADS * LANES).T)
    ukv = w_ukv[0].reshape(B_KV_RANK, B_HEADS, B_NOPE + B_VDIM)
    wuk_t = bf(ukv[:, :, :B_NOPE].reshape(B_KV_RANK, B_HEADS * B_NOPE).T)
    wuv_t = bf(ukv[:, :, B_NOPE:].reshape(B_KV_RANK, B_HEADS * B_VDIM).T)
    wmk = bf(w_mem_kv[0][:, :M_HEADS * M_HEAD_DIM])
    wmv_t = bf(w_mem_kv[0][:, M_HEADS * M_HEAD_DIM:].T)
    wz = bf(jnp.stack([w_za, w_zb, w_zm]))
    wg = bf(w_gate)
    wb = bf(w_branch[0])
    wo = bf(w_out[0])

    gn = g_norm[0].reshape(1, D_MODEL)
    gqa_c = g_qn_a[0].reshape(A_HEAD_DIM, 1)
    gka_c = g_kn_a[0].reshape(A_HEAD_DIM, 1)
    gcq_r = g_cq[0].reshape(1, B_Q_RANK)
    gckv_r = g_ckv[0].reshape(1, B_KV_RANK)
    gqb_c = jnp.pad(g_qn_b[0], (0, LANES - B_QK)).reshape(LANES, 1)
    gkb_c = jnp.pad(g_kn_b[0], (0, LANES - B_QK)).reshape(LANES, 1)
    gqm_c = g_qn_m[0].reshape(M_HEAD_DIM, 1)
    gkm_r = g_kn_m[0].reshape(1, M_HEAD_DIM)
    gmem_r = g_mem[0].reshape(1, D_MODEL)

    inv_a = ROPE_THETA ** (-(jnp.arange(0, A_ROT, 2, dtype=F32) / A_ROT))
    inv_b = ROPE_THETA ** (-(jnp.arange(0, B_ROPE, 2, dtype=F32) / B_ROPE))

    x2 = x.reshape(n, d)
    pos_r = positions.reshape(1, n)
    tile = lambda width: pl.BlockSpec((TM, width), lambda i: (i, 0))
    tile_t = lambda rows: pl.BlockSpec((rows, TM), lambda i: (0, i))
    pos_spec = pl.BlockSpec((1, TM), lambda i: (0, i))

    a_in = [wqa_t, wqi_t, wwi_t, wka_t, wva_t, wki_t, gqa_c, gka_c, inv_a.reshape(-1, 1)]
    b_in = [bf(w_cq), bf(w_ckv), wkr_t, wqm_t, wuq_t, wuk_t, wuv_t,
            gcq_r, gckv_r, gqb_c, gkb_c, gqm_c, inv_b.reshape(-1, 1)]
    assert len(a_in) == N_PROJ_A_IN and len(b_in) == N_PROJ_B_IN
    qat, qib, wt, ka, vat, ki, qbt, kb, vbt, qmt = pl.pallas_call(
        _proj_kernel,
        grid=(n // TM,),
        in_specs=[tile(d), pos_spec, _full(gn.shape)] + [_full(a.shape) for a in a_in + b_in],
        out_specs=[tile_t(A_HEADS * LANES), pl.BlockSpec((TM // TQ, LANES, IDX_HEADS * TQ), lambda i: (i, 0, 0)),
                   tile_t(IDX_HEADS), pl.BlockSpec((512 // LANES, TM, LANES), lambda i: (0, i, 0)),
                   tile_t(512), tile(LANES),
                   tile_t(B_HEADS * LANES), tile(B_HEADS * LANES), tile_t(512), tile_t(512)],
        out_shape=[jax.ShapeDtypeStruct((A_HEADS * LANES, n), BF16),
                   jax.ShapeDtypeStruct((n // TQ, LANES, IDX_HEADS * TQ), BF16),
                   jax.ShapeDtypeStruct((IDX_HEADS, n), F32),
                   jax.ShapeDtypeStruct((512 // LANES, n, LANES), BF16),
                   jax.ShapeDtypeStruct((512, n), BF16),
                   jax.ShapeDtypeStruct((n, LANES), BF16),
                   jax.ShapeDtypeStruct((B_HEADS * LANES, n), BF16),
                   jax.ShapeDtypeStruct((n, B_HEADS * LANES), BF16),
                   jax.ShapeDtypeStruct((512, n), BF16),
                   jax.ShapeDtypeStruct((512, n), BF16)],
        compiler_params=_params(1), name="proj",
    )(x2, pos_r, gn, *a_in, *b_in)

    km, vmt = pl.pallas_call(
        _mem_kv_kernel,
        grid=(b,),
        in_specs=[pl.BlockSpec((m_len, d), lambda i: (i, 0)), _full(gmem_r.shape), _full(wmk.shape),
                  _full(wmv_t.shape), _full(gkm_r.shape)],
        out_specs=[pl.BlockSpec((m_len, 512), lambda i: (i, 0)), pl.BlockSpec((512, m_len), lambda i: (0, i))],
        out_shape=[jax.ShapeDtypeStruct((b * m_len, 512), BF16), jax.ShapeDtypeStruct((512, b * m_len), BF16)],
        compiler_params=_params(1), name="mem_kv",
    )(mem.reshape(b * m_len, d), gmem_r, wmk, wmv_t, gkm_r)

    qcol = lambda rows: pl.BlockSpec((rows, TQ), lambda bi, qi: (0, bi * nq + qi))
    seq_rows = lambda width: pl.BlockSpec((s, width), lambda bi, qi: (bi, 0))
    seq_cols = lambda rows: pl.BlockSpec((rows, s), lambda bi, qi: (0, bi))
    o_spec = pl.BlockSpec((TQ, 512), lambda bi, qi: (bi * nq + qi, 0))
    o_shape = jax.ShapeDtypeStruct((n, 512), F32)
    attn_scratch = [pltpu.VMEM((s, TQ), F32), pltpu.VMEM((s, TQ), F32), pltpu.VMEM((2 * s, TQ), BF16),
                    pltpu.VMEM((512, TQ), F32)]

    def dsa(bounded):
        return pl.pallas_call(
            functools.partial(_dsa_kernel, seq=s, bounded=bounded),
            grid=(b, nq),
            in_specs=[qcol(A_HEADS * LANES),
                      pl.BlockSpec((1, LANES, IDX_HEADS * TQ), lambda bi, qi: (bi * nq + qi, 0, 0)),
                      qcol(IDX_HEADS), pl.BlockSpec((512 // LANES, s, LANES), lambda bi, qi: (0, bi, 0)),
                      seq_cols(512), seq_rows(LANES)],
            out_specs=o_spec, out_shape=o_shape,
            scratch_shapes=[pltpu.VMEM((s, TQ), F32)] + attn_scratch,
            compiler_params=_params(2), name="dsa" if bounded else "dsa_general")

    def mla(bounded):
        return pl.pallas_call(
            functools.partial(_mla_kernel, seq=s, bounded=bounded),
            grid=(b, nq),
            in_specs=[qcol(B_HEADS * LANES), seq_rows(B_HEADS * LANES), seq_cols(512)],
            out_specs=o_spec, out_shape=o_shape,
            scratch_shapes=attn_scratch,
            compiler_params=_params(2), name="mla" if bounded else "mla_general")

    def mem_attn(bounded):
        return pl.pallas_call(
            functools.partial(_mem_attn_kernel, mem_len=m_len, bounded=bounded),
            grid=(b, nq),
            in_specs=[qcol(512), pl.BlockSpec((m_len, 512), lambda bi, qi: (bi, 0)),
                      pl.BlockSpec((512, m_len), lambda bi, qi: (0, bi))],
            out_specs=o_spec, out_shape=o_shape,
            scratch_shapes=[pltpu.VMEM((m_len, TQ), F32), pltpu.VMEM((M_HEADS * m_len, TQ), BF16),
                            pltpu.VMEM((512, TQ), F32)],
            compiler_params=_params(2), name="mem_attn" if bounded else "mem_attn_general")

    def score_bound(gq, gk, dim):
        return dim ** 0.5 * LOG2E * 1.02 * jnp.max(jnp.abs(gq)) * jnp.max(jnp.abs(gk))

    def attention(bounded):
        def run(*ops):
            return (dsa(bounded)(*ops[:6]), mla(bounded)(*ops[6:9]), mem_attn(bounded)(*ops[9:]))
        return run

    worst = jnp.maximum(jnp.maximum(score_bound(g_qn_a, g_kn_a, A_HEAD_DIM), score_bound(g_qn_b, g_kn_b, B_QK)),
                        score_bound(g_qn_m, g_kn_m, M_HEAD_DIM))
    oa, ob, om = lax.cond(worst <= BOUNDED_SCORE_LIMIT, attention(True), attention(False),
                          qat, qib, wt, ka, vat, ki, qbt, kb, vbt, qmt, km, vmt)

    out = pl.pallas_call(
        _final_kernel,
        grid=(n // TM,),
        in_specs=[tile(d), tile(512), tile(512), tile(512), _full(gn.shape), _full(wz.shape), _full(wg.shape),
                  _full(wb.shape), _full(wo.shape)],
        out_specs=tile(d), out_shape=jax.ShapeDtypeStruct((n, d), x.dtype),
        compiler_params=_params(1), name="final",
    )(x2, oa, ob, om, gn, wz, wg, wb, wo)
    return out.reshape(b, s, d)
```

```python
import functools

import numpy as np
import jax
import jax.numpy as jnp
from jax import lax
from jax.experimental import pallas as pl
from jax.experimental.pallas import tpu as pltpu

F32 = jnp.float32
BF16 = jnp.bfloat16
I32 = jnp.int32

D_MODEL = 1024
ROPE_THETA = 500000.0
EPS = 1e-6
NEG = -1e30
N_BRANCH = 3
BRANCH_WIDTH = 512
A_HEADS = 8
A_HEAD_DIM = 64
A_ROT = A_HEAD_DIM // 4
IDX_HEADS = 8
IDX_DIM = 64
TOPK_MAX = 256
B_HEADS = 8
B_NOPE = 64
B_ROPE = 32
B_VDIM = 64
B_QK = B_NOPE + B_ROPE
B_Q_RANK = 384
B_KV_RANK = 256
M_HEADS = 4
M_HEAD_DIM = 128

LANES = 128
TM = 512
TQ = 256
TQ_MEM = 512
KC = 256
COUNT_ROWS = 64
VMEM_LIMIT = 56 * 1024 * 1024
INT_MIN = -2 ** 31
LOG2E = 1.4426950408889634
BOUNDED_SCORE_LIMIT = 32.0


def _nt(a, b):
    return lax.dot_general(a, b, (((1,), (1,)), ((), ())), preferred_element_type=F32)


def _mm(a, b):
    return jnp.dot(a, b, preferred_element_type=F32)


def _rms_lanes(xf, g_row, n=None):
    n = xf.shape[-1] if n is None else n
    ms = jnp.sum(xf * xf, axis=-1, keepdims=True) / n
    return xf * lax.rsqrt(ms + EPS) * g_row


def _rms_rows(blk, g_col, n=None):
    n = blk.shape[0] if n is None else n
    ms = jnp.sum(blk * blk, axis=0, keepdims=True) / n
    return blk * lax.rsqrt(ms + EPS) * g_col


def _rope_rows(blk, lo, half, cos_t, sin_t):
    x1 = blk[lo:lo + half]
    x2 = blk[lo + half:lo + 2 * half]
    parts = []
    if lo:
        parts.append(blk[:lo])
    parts += [x1 * cos_t - x2 * sin_t, x2 * cos_t + x1 * sin_t]
    if lo + 2 * half < blk.shape[0]:
        parts.append(blk[lo + 2 * half:])
    return jnp.concatenate(parts, axis=0)


def _token_major(blocks):
    rows = sum(blk.shape[0] for blk in blocks)
    if rows < LANES:
        blocks = list(blocks) + [jnp.zeros((LANES - rows, blocks[0].shape[1]), F32)]
    return jnp.concatenate(blocks, axis=0).T.astype(BF16)


N_PROJ_A_W, N_PROJ_B_W, N_PROJ_A_OUT = 6, 7, 6
COL_GQA, COL_GKA, COL_GQB, COL_GKB, COL_GQM, COL_INVA, COL_INVB, N_COLS = 0, 1, 2, 3, 4, 5, 6, 7
ROW_GN, ROW_GCQ, ROW_GCKV, N_ROWS = 0, 1, 2, 3


def _proj_kernel(x_ref, posr_ref, rows_ref, cols_ref, *refs):
    rows, cols = rows_ref[...], cols_ref[...]
    col = lambda j, n: cols[0:n, j:j + 1]
    h = _rms_lanes(x_ref[...], rows[ROW_GN:ROW_GN + 1, :]).astype(BF16)
    a_w = refs[:N_PROJ_A_W]
    b_w = refs[N_PROJ_A_W:N_PROJ_A_W + N_PROJ_B_W]
    outs = refs[N_PROJ_A_W + N_PROJ_B_W:]
    _proj_a(h, posr_ref, *a_w, col(COL_GQA, A_HEAD_DIM), col(COL_GKA, A_HEAD_DIM), col(COL_INVA, A_ROT // 2),
            *outs[:N_PROJ_A_OUT])
    _proj_b(h, posr_ref, *b_w, rows[ROW_GCQ:ROW_GCQ + 1, 0:B_Q_RANK], rows[ROW_GCKV:ROW_GCKV + 1, 0:B_KV_RANK],
            col(COL_GQB, LANES), col(COL_GKB, LANES), col(COL_GQM, M_HEAD_DIM), col(COL_INVB, B_ROPE // 2),
            *outs[N_PROJ_A_OUT:])


def _proj_a(h, posr_ref, wqa_ref, wqi_ref, wwi_ref, wka_ref, wva_ref, wki_ref, gq, gk, inv_freq,
            qat_ref, qib_ref, wt_ref, ka_ref, vat_ref, ki_ref):
    ang_t = inv_freq * posr_ref[...].astype(F32)
    cos_t, sin_t = jnp.cos(ang_t), jnp.sin(ang_t)
    half = A_ROT // 2

    qa = _nt(wqa_ref[...], h)
    for hh in range(A_HEADS):
        blk = _rms_rows(qa[hh * A_HEAD_DIM:(hh + 1) * A_HEAD_DIM], gq)
        blk = _rope_rows(blk, 0, half, cos_t, sin_t) * (A_HEAD_DIM ** -0.5 * LOG2E)
        own = hh * LANES + (hh % 2) * A_HEAD_DIM
        other = hh * LANES + (1 - hh % 2) * A_HEAD_DIM
        qat_ref[own:own + A_HEAD_DIM, :] = blk.astype(BF16)
        qat_ref[other:other + A_HEAD_DIM, :] = jnp.zeros((A_HEAD_DIM, TM), BF16)

    qi = _nt(wqi_ref[...], h)
    for hh in range(IDX_HEADS):
        blk = _rope_rows(qi[hh * IDX_DIM:(hh + 1) * IDX_DIM], 0, half, cos_t, sin_t)
        blk = (blk * (IDX_DIM ** -0.5)).astype(BF16)
        for j in range(TM // TQ):
            qib_ref[j, 0:IDX_DIM, hh * TQ:(hh + 1) * TQ] = blk[:, j * TQ:(j + 1) * TQ]
    qib_ref[:, IDX_DIM:, :] = jnp.zeros((TM // TQ, LANES - IDX_DIM, IDX_HEADS * TQ), BF16)

    wt_ref[...] = _nt(wwi_ref[...], h)[0:IDX_HEADS] * (IDX_HEADS ** -0.5)

    vat_ref[...] = _nt(wva_ref[...], h).astype(BF16)

    ka = _nt(wka_ref[...], h)
    for c in range(A_HEADS // 2):
        pair = [_rope_rows(_rms_rows(ka[hh * A_HEAD_DIM:(hh + 1) * A_HEAD_DIM], gk), 0, half, cos_t, sin_t)
                for hh in (2 * c, 2 * c + 1)]
        ka_ref[c, :, :] = _token_major(pair)

    ki = _nt(wki_ref[...], h)
    ki_ref[...] = _token_major([_rope_rows(ki, 0, half, cos_t, sin_t)])


def _proj_b(h, posr_ref, wcq_ref, wckv_ref, wkr_ref, wqm_ref, wuq_ref, wuk_ref, wuv_ref,
            gcq, gckv, gq, gk, gm, inv_freq,
            qbt_ref, kb_ref, vbt_ref, qmt_ref):
    half = B_ROPE // 2
    ang_t = inv_freq * posr_ref[...].astype(F32)
    cos_t, sin_t = jnp.cos(ang_t), jnp.sin(ang_t)

    cq = _rms_lanes(_mm(h, wcq_ref[...]), gcq).astype(BF16)
    qb = _nt(wuq_ref[...], cq)
    for hh in range(B_HEADS):
        blk = _rms_rows(qb[hh * LANES:(hh + 1) * LANES], gq, n=B_QK)
        blk = _rope_rows(blk, B_NOPE, half, cos_t, sin_t) * (B_QK ** -0.5 * LOG2E)
        qbt_ref[hh * LANES:(hh + 1) * LANES, :] = blk.astype(BF16)

    ckv = _rms_lanes(_mm(h, wckv_ref[...]), gckv).astype(BF16)
    kn = _nt(wuk_ref[...], ckv)
    kr = _nt(wkr_ref[...], h)
    pad = jnp.zeros((LANES - B_QK, TM), F32)
    for hh in range(B_HEADS):
        blk = jnp.concatenate([kn[hh * B_NOPE:(hh + 1) * B_NOPE], kr, pad], axis=0)
        blk = _rope_rows(_rms_rows(blk, gk, n=B_QK), B_NOPE, half, cos_t, sin_t)
        kb_ref[:, hh * LANES:(hh + 1) * LANES] = _token_major([blk])
    vbt_ref[...] = _nt(wuv_ref[...], ckv).astype(BF16)

    qm = _nt(wqm_ref[...], h)
    for hh in range(M_HEADS):
        blk = _rms_rows(qm[hh * M_HEAD_DIM:(hh + 1) * M_HEAD_DIM], gm) * (M_HEAD_DIM ** -0.5 * LOG2E)
        qmt_ref[hh * M_HEAD_DIM:(hh + 1) * M_HEAD_DIM, :] = blk.astype(BF16)


def _mem_kv(mem_ref, gains_ref, wk_ref, wvt_ref, km_ref, vmt_ref):
    hm = _rms_lanes(mem_ref[...], gains_ref[0:1, :]).astype(BF16)
    k = _mm(hm, wk_ref[...])
    gk = gains_ref[1:2, 0:M_HEAD_DIM]
    for hh in range(M_HEADS):
        kc = _rms_lanes(k[:, hh * M_HEAD_DIM:(hh + 1) * M_HEAD_DIM], gk)
        km_ref[:, hh * M_HEAD_DIM:(hh + 1) * M_HEAD_DIM] = kc.astype(BF16)
    vmt_ref[...] = _nt(wvt_ref[...], hm).astype(BF16)


def _attend(nk, n_heads, dv, q_of, k_of, v_of, bias_of, bounded, s_ref, p_ref, ot_ref):
    nq = ot_ref.shape[1]
    chunks = [slice(c * KC, (c + 1) * KC) for c in range(nk // KC)]

    def scores(hh, q, c):
        s = _mm(k_of(hh, chunks[c]), q)
        b = bias_of(c)
        return s if b is None else s + b

    depth = max(1, min(n_heads - 1, 8 // len(chunks)))
    slots = depth + 1

    def p_rows(hh, sl):
        base = (hh % slots) * nk
        return slice(base + sl.start, base + sl.stop)

    def probabilities(hh):
        q = q_of(hh)
        if bounded:
            l8 = jnp.zeros((8, nq), F32)
            for c in range(len(chunks)):
                p = jnp.exp2(scores(hh, q, c))
                l8 = l8 + p.reshape(KC // 8, 8, nq).sum(axis=0)
                p_ref[p_rows(hh, chunks[c]), :] = p.astype(BF16)
            return jnp.sum(l8, axis=0, keepdims=True)
        m = jnp.full((1, nq), -jnp.inf, F32)
        for c in range(len(chunks)):
            s = scores(hh, q, c)
            s_ref[chunks[c], :] = s
            m = jnp.maximum(m, jnp.max(s, axis=0, keepdims=True))
        l = jnp.zeros((1, nq), F32)
        for c in range(len(chunks)):
            p = jnp.exp2(s_ref[chunks[c], :] - m)
            l = l + jnp.sum(p, axis=0, keepdims=True)
            p_ref[p_rows(hh, chunks[c]), :] = p.astype(BF16)
        return l

    def weighted_values(hh, l):
        o = _mm(v_of(hh, slice(0, nk)), p_ref[p_rows(hh, slice(0, nk)), :])
        ot_ref[hh * dv:(hh + 1) * dv, :] = o / l

    sums = {}
    for step in range(n_heads + depth):
        if step < n_heads:
            sums[step] = probabilities(step)
        if step >= depth:
            weighted_values(step - depth, sums.pop(step - depth))


def _count(score_ref, nk, pred):
    cnt = jnp.zeros((COUNT_ROWS, TQ), I32)
    for r in range(0, nk, COUNT_ROWS):
        cnt = jnp.where(pred(score_ref[r:r + COUNT_ROWS, :]), cnt + 1, cnt)
    return jnp.sum(cnt, axis=0, keepdims=True)


def _ordered_to_bits(u, magnitude_mask):
    k = u ^ INT_MIN
    return k ^ ((k >> 31) & magnitude_mask)


def _ordered_pattern_to_float(u):
    return pltpu.bitcast(_ordered_to_bits(u, 0x7FFFFFFF), F32)


def _count_rounded(round_ref, nk, cand):
    assert nk // COUNT_ROWS <= 256
    one, zero = jnp.ones((), BF16), jnp.zeros((), BF16)
    cnt = jnp.zeros((COUNT_ROWS, TQ), BF16)
    for r in range(0, nk, COUNT_ROWS):
        cnt = cnt + jnp.where(round_ref[r:r + COUNT_ROWS, :] >= cand, one, zero)
    return jnp.sum(cnt.astype(F32), axis=0, keepdims=True)


def _select_topk(nk, q_pos, row, chunks, qib_ref, wt_ref, ki_ref, score_ref, round_ref, emit):
    for c, sl in enumerate(chunks):
        ki_c = ki_ref[sl, :]
        acc = jnp.zeros((KC, TQ), F32)
        for hh in range(IDX_HEADS):
            d = _mm(ki_c, qib_ref[0, :, hh * TQ:(hh + 1) * TQ])
            acc = acc + jnp.maximum(d, 0.0) * wt_ref[hh:hh + 1, :]
        score = jnp.where(row + c * KC <= q_pos, acc, NEG)
        score_ref[sl, :] = score
        round_ref[sl, :] = score.astype(BF16)

    def coarse(i, c_u):
        cand_u = c_u | jnp.left_shift(jnp.int32(1), 31 - i)
        cand = pltpu.bitcast(_ordered_to_bits(cand_u, 0x7FFF0000), F32).astype(BF16)
        return jnp.where(_count_rounded(round_ref, nk, cand) >= TOPK_MAX, cand_u, c_u)

    c_u = lax.fori_loop(0, 16, coarse, jnp.zeros((1, TQ), I32))
    pred_bits = _ordered_to_bits(c_u - (1 << 16), 0x7FFF0000)
    base_u = (pred_bits ^ ((pred_bits >> 31) & 0x7FFFFFFF)) ^ INT_MIN

    def fine(i, carry):
        off, cnt_t = carry
        cand_off = off | jnp.left_shift(jnp.int32(1), 16 - i)
        cand = _ordered_pattern_to_float(base_u + cand_off)
        cnt = _count(score_ref, nk, lambda x: x >= cand)
        ok = cnt >= TOPK_MAX
        return jnp.where(ok, cand_off, off), jnp.where(ok, cnt, cnt_t)

    off, cnt_t = lax.fori_loop(0, 17, fine, (jnp.zeros((1, TQ), I32), jnp.full((1, TQ), nk, I32)))
    thr = _ordered_pattern_to_float(base_u + off)
    split_ties = jnp.max(jnp.where(cnt_t > TOPK_MAX, 1, 0)) > 0

    @pl.when(jnp.logical_not(split_ties))
    def _():
        for c, sl in enumerate(chunks):
            emit(sl, (score_ref[sl, :] >= thr) & (row + c * KC <= q_pos))

    @pl.when(split_ties)
    def _():
        room = (TOPK_MAX - _count(score_ref, nk, lambda x: x > thr)).astype(F32)
        tri = lax.broadcasted_iota(I32, (KC, KC), 0) >= lax.broadcasted_iota(I32, (KC, KC), 1)
        tri = jnp.where(tri, 1.0, 0.0).astype(BF16)
        running = jnp.zeros((1, TQ), F32)
        for c, sl in enumerate(chunks):
            x = score_ref[sl, :]
            tie = x == thr
            rank = _mm(tri, jnp.where(tie, 1.0, 0.0).astype(BF16)) + running
            running = rank[KC - 1:KC, :]
            emit(sl, ((x > thr) | (tie & (rank <= room))) & (row + c * KC <= q_pos))


def _dsa_body(nk, start, bounded, qat_ref, qib_ref, wt_ref, ka_ref, vat_ref, ki_ref, oa_ref,
              score_ref, bias_ref, s_ref, p_ref, ot_ref):
    q_pos = start + lax.broadcasted_iota(I32, (1, TQ), 1)
    row = lax.broadcasted_iota(I32, (KC, TQ), 0)
    chunks = [slice(c * KC, (c + 1) * KC) for c in range(nk // KC)]

    def emit_bias(sl, keep):
        bias_ref[sl, :] = jnp.where(keep, 0.0, NEG)

    if nk <= TOPK_MAX:
        for c, sl in enumerate(chunks):
            emit_bias(sl, row + c * KC <= q_pos)
    else:
        _select_topk(nk, q_pos, row, chunks, qib_ref, wt_ref, ki_ref, score_ref, p_ref, emit_bias)

    def q_of(hh):
        return qat_ref[hh * LANES:(hh + 1) * LANES, :]

    def k_of(hh, sl):
        return ka_ref[hh // 2, sl, :]

    def v_of(hh, sl):
        return vat_ref[hh * A_HEAD_DIM:(hh + 1) * A_HEAD_DIM, sl]

    _attend(nk, A_HEADS, A_HEAD_DIM, q_of, k_of, v_of, lambda c: bias_ref[chunks[c], :], bounded,
            s_ref, p_ref, ot_ref)
    oa_ref[...] = ot_ref[...].T


def _dsa_kernel(qat_ref, qib_ref, wt_ref, ka_ref, vat_ref, ki_ref, oa_ref,
                score_ref, bias_ref, s_ref, p_ref, ot_ref, *, seq, bounded):
    qb = pl.program_id(1)
    for cls in range(seq // TQ):
        @pl.when(qb == cls)
        def _():
            _dsa_body(TQ * (cls + 1), cls * TQ, bounded, qat_ref, qib_ref, wt_ref, ka_ref, vat_ref,
                      ki_ref, oa_ref, score_ref, bias_ref, s_ref, p_ref, ot_ref)


def _mla_body(nk, start, bounded, qbt_ref, kb_ref, vbt_ref, ob_ref, bias_ref, s_ref, p_ref, ot_ref):
    last = nk // KC - 1
    q_pos = start + lax.broadcasted_iota(I32, (1, TQ), 1)
    row = lax.broadcasted_iota(I32, (KC, TQ), 0)
    bias_ref[0:KC, :] = jnp.where(row + last * KC <= q_pos, 0.0, NEG)

    def q_of(hh):
        return qbt_ref[hh * LANES:(hh + 1) * LANES, :]

    def k_of(hh, sl):
        return kb_ref[sl, hh * LANES:(hh + 1) * LANES]

    def v_of(hh, sl):
        return vbt_ref[hh * B_VDIM:(hh + 1) * B_VDIM, sl]

    _attend(nk, B_HEADS, B_VDIM, q_of, k_of, v_of, lambda c: bias_ref[0:KC, :] if c == last else None,
            bounded, s_ref, p_ref, ot_ref)
    ob_ref[...] = ot_ref[...].T


def _mla_kernel(qbt_ref, kb_ref, vbt_ref, ob_ref, bias_ref, s_ref, p_ref, ot_ref, *, seq, bounded):
    qb = pl.program_id(1)
    for cls in range(seq // TQ):
        @pl.when(qb == cls)
        def _():
            _mla_body(TQ * (cls + 1), cls * TQ, bounded, qbt_ref, kb_ref, vbt_ref, ob_ref,
                      bias_ref, s_ref, p_ref, ot_ref)


def _mem_attn_kernel(qmt_ref, mem_ref, gains_ref, wk_ref, wvt_ref, om_ref, km_ref, vmt_ref, s_ref, p_ref, ot_ref,
                     *, mem_len, bounded):
    @pl.when(pl.program_id(1) == 0)
    def _():
        _mem_kv(mem_ref, gains_ref, wk_ref, wvt_ref, km_ref, vmt_ref)

    def q_of(hh):
        return qmt_ref[hh * M_HEAD_DIM:(hh + 1) * M_HEAD_DIM, :]

    def k_of(hh, sl):
        return km_ref[sl, hh * M_HEAD_DIM:(hh + 1) * M_HEAD_DIM]

    def v_of(hh, sl):
        return vmt_ref[hh * M_HEAD_DIM:(hh + 1) * M_HEAD_DIM, sl]

    _attend(mem_len, M_HEADS, M_HEAD_DIM, q_of, k_of, v_of, lambda c: None, bounded, s_ref, p_ref, ot_ref)
    om_ref[...] = ot_ref[...].T


def _final_kernel(x_ref, oa_ref, ob_ref, om_ref, rows_ref, wz_ref, wg_ref, wb_ref, wo_ref, out_ref):
    x = x_ref[...]
    h = _rms_lanes(x, rows_ref[ROW_GN:ROW_GN + 1, :]).astype(BF16)
    merged = jnp.zeros((TM, D_MODEL), F32)
    for n, o_ref in enumerate((oa_ref, ob_ref, om_ref)):
        z = _mm(h, wz_ref[n])
        y = (o_ref[...] * (z * jax.nn.sigmoid(z))).astype(BF16)
        branch = _mm(y, wb_ref[n])
        gate = jax.nn.sigmoid(_mm(h, wg_ref[:, n * D_MODEL:(n + 1) * D_MODEL]))
        merged = merged + gate * branch
    out_ref[...] = x + _mm(merged.astype(BF16), wo_ref[...])


def _full(shape):
    return pl.BlockSpec(shape, lambda *_: (0,) * len(shape), pipeline_mode=pl.Buffered(1))


def _params(n_axes):
    return pltpu.CompilerParams(dimension_semantics=("arbitrary",) * n_axes,
                                vmem_limit_bytes=VMEM_LIMIT)


def kernel(x, mem, positions, g_norm, w_in, g_qn_a, g_kn_a, g_cq, g_ckv, w_uq, w_ukv, g_qn_b, g_kn_b,
           g_mem, w_mem_kv, g_qn_m, g_kn_m, w_branch, w_out):
    b, s, d = x.shape
    m_len = mem.shape[1]
    n = b * s
    nq = s // TQ
    assert d == D_MODEL and s % TQ == 0 and TQ == KC and TM % TQ == 0 and n % TM == 0 and m_len % KC == 0
    assert g_norm.shape[0] == 1, "single-layer block"

    w = w_in[0]
    off = np.cumsum([0, 512, 512, 512, 512, IDX_DIM, IDX_HEADS, BRANCH_WIDTH, B_Q_RANK, B_KV_RANK, B_ROPE,
                     BRANCH_WIDTH, M_HEADS * M_HEAD_DIM, BRANCH_WIDTH, N_BRANCH * D_MODEL])
    seg = [w[:, off[i]:off[i + 1]] for i in range(14)]
    (w_qa, w_ka, w_va, w_qi, w_ki, w_wi, w_za, w_cq, w_ckv, w_kr, w_zb, w_qm, w_zm, w_gate) = seg
    bf = lambda a: a.astype(BF16)
    wqa_t, wqi_t, wva_t, wqm_t = bf(w_qa.T), bf(w_qi.T), bf(w_va.T), bf(w_qm.T)
    wka_t, wki_t, wkr_t = bf(w_ka.T), bf(w_ki.T), bf(w_kr.T)
    wwi_t = bf(jnp.pad(w_wi.T, ((0, 16 - IDX_HEADS), (0, 0))))
    wuq_t = bf(jnp.pad(w_uq[0].reshape(B_Q_RANK, B_HEADS, B_QK), ((0, 0), (0, 0), (0, LANES - B_QK)))
               .reshape(B_Q_RANK, B_HEADS * LANES).T)
    ukv = w_ukv[0].reshape(B_KV_RANK, B_HEADS, B_NOPE + B_VDIM)
    wuk_t = bf(ukv[:, :, :B_NOPE].reshape(B_KV_RANK, B_HEADS * B_NOPE).T)
    wuv_t = bf(ukv[:, :, B_NOPE:].reshape(B_KV_RANK, B_HEADS * B_VDIM).T)
    wmk = bf(w_mem_kv[0][:, :M_HEADS * M_HEAD_DIM])
    wmv_t = bf(w_mem_kv[0][:, M_HEADS * M_HEAD_DIM:].T)
    wz = bf(jnp.stack([w_za, w_zb, w_zm]))
    wg = bf(w_gate)
    wb = bf(w_branch[0])
    wo = bf(w_out[0])

    pad_to = lambda v, size: jnp.pad(v, (0, size - v.shape[0]))
    inv_a = ROPE_THETA ** (-(jnp.arange(0, A_ROT, 2, dtype=F32) / A_ROT))
    inv_b = ROPE_THETA ** (-(jnp.arange(0, B_ROPE, 2, dtype=F32) / B_ROPE))
    col_vectors = [None] * N_COLS
    col_vectors[COL_GQA], col_vectors[COL_GKA] = g_qn_a[0], g_kn_a[0]
    col_vectors[COL_GQB], col_vectors[COL_GKB] = g_qn_b[0], g_kn_b[0]
    col_vectors[COL_GQM], col_vectors[COL_INVA], col_vectors[COL_INVB] = g_qn_m[0], inv_a, inv_b
    cols = jnp.stack([pad_to(v, LANES) for v in col_vectors], axis=1)
    row_vectors = [None] * N_ROWS
    row_vectors[ROW_GN], row_vectors[ROW_GCQ], row_vectors[ROW_GCKV] = g_norm[0], g_cq[0], g_ckv[0]
    rows = jnp.stack([pad_to(v, D_MODEL) for v in row_vectors])
    mem_gains = jnp.stack([g_mem[0], pad_to(g_kn_m[0], D_MODEL)])

    x2 = x.reshape(n, d)
    pos_r = positions.reshape(1, n)
    tile = lambda width: pl.BlockSpec((TM, width), lambda i: (i, 0))
    tile_t = lambda rows: pl.BlockSpec((rows, TM), lambda i: (0, i))
    pos_spec = pl.BlockSpec((1, TM), lambda i: (0, i))

    a_w = [wqa_t, wqi_t, wwi_t, wka_t, wva_t, wki_t]
    b_w = [bf(w_cq), bf(w_ckv), wkr_t, wqm_t, wuq_t, wuk_t, wuv_t]
    assert len(a_w) == N_PROJ_A_W and len(b_w) == N_PROJ_B_W
    qat, qib, wt, ka, vat, ki, qbt, kb, vbt, qmt = pl.pallas_call(
        _proj_kernel,
        grid=(n // TM,),
        in_specs=[tile(d), pos_spec, _full(rows.shape), _full(cols.shape)] + [_full(a.shape) for a in a_w + b_w],
        out_specs=[tile_t(A_HEADS * LANES), pl.BlockSpec((TM // TQ, LANES, IDX_HEADS * TQ), lambda i: (i, 0, 0)),
                   tile_t(IDX_HEADS), pl.BlockSpec((512 // LANES, TM, LANES), lambda i: (0, i, 0)),
                   tile_t(512), tile(LANES),
                   tile_t(B_HEADS * LANES), tile(B_HEADS * LANES), tile_t(512), tile_t(512)],
        out_shape=[jax.ShapeDtypeStruct((A_HEADS * LANES, n), BF16),
                   jax.ShapeDtypeStruct((n // TQ, LANES, IDX_HEADS * TQ), BF16),
                   jax.ShapeDtypeStruct((IDX_HEADS, n), F32),
                   jax.ShapeDtypeStruct((512 // LANES, n, LANES), BF16),
                   jax.ShapeDtypeStruct((512, n), BF16),
                   jax.ShapeDtypeStruct((n, LANES), BF16),
                   jax.ShapeDtypeStruct((B_HEADS * LANES, n), BF16),
                   jax.ShapeDtypeStruct((n, B_HEADS * LANES), BF16),
                   jax.ShapeDtypeStruct((512, n), BF16),
                   jax.ShapeDtypeStruct((512, n), BF16)],
        compiler_params=_params(1), name="proj",
    )(x2, pos_r, rows, cols, *a_w, *b_w)

    qcol = lambda rows: pl.BlockSpec((rows, TQ), lambda bi, qi: (0, bi * nq + qi))
    seq_rows = lambda width: pl.BlockSpec((s, width), lambda bi, qi: (bi, 0))
    seq_cols = lambda rows: pl.BlockSpec((rows, s), lambda bi, qi: (0, bi))
    o_spec = pl.BlockSpec((TQ, 512), lambda bi, qi: (bi * nq + qi, 0))
    o_shape = jax.ShapeDtypeStruct((n, 512), F32)
    attn_scratch = [pltpu.VMEM((s, TQ), F32), pltpu.VMEM((s, TQ), F32), pltpu.VMEM((2 * s, TQ), BF16),
                    pltpu.VMEM((512, TQ), F32)]

    def dsa(bounded):
        return pl.pallas_call(
            functools.partial(_dsa_kernel, seq=s, bounded=bounded),
            grid=(b, nq),
            in_specs=[qcol(A_HEADS * LANES),
                      pl.BlockSpec((1, LANES, IDX_HEADS * TQ), lambda bi, qi: (bi * nq + qi, 0, 0)),
                      qcol(IDX_HEADS), pl.BlockSpec((512 // LANES, s, LANES), lambda bi, qi: (0, bi, 0)),
                      seq_cols(512), seq_rows(LANES)],
            out_specs=o_spec, out_shape=o_shape,
            scratch_shapes=[pltpu.VMEM((s, TQ), F32)] + attn_scratch,
            compiler_params=_params(2), name="dsa" if bounded else "dsa_general")

    def mla(bounded):
        return pl.pallas_call(
            functools.partial(_mla_kernel, seq=s, bounded=bounded),
            grid=(b, nq),
            in_specs=[qcol(B_HEADS * LANES), seq_rows(B_HEADS * LANES), seq_cols(512)],
            out_specs=o_spec, out_shape=o_shape,
            scratch_shapes=attn_scratch,
            compiler_params=_params(2), name="mla" if bounded else "mla_general")

    nq_mem = s // TQ_MEM

    def mem_attn(bounded):
        return pl.pallas_call(
            functools.partial(_mem_attn_kernel, mem_len=m_len, bounded=bounded),
            grid=(b, nq_mem),
            in_specs=[pl.BlockSpec((512, TQ_MEM), lambda bi, qi: (0, bi * nq_mem + qi)),
                      pl.BlockSpec((m_len, d), lambda bi, qi: (bi, 0)),
                      _full(mem_gains.shape), _full(wmk.shape), _full(wmv_t.shape)],
            out_specs=pl.BlockSpec((TQ_MEM, 512), lambda bi, qi: (bi * nq_mem + qi, 0)), out_shape=o_shape,
            scratch_shapes=[pltpu.VMEM((m_len, 512), BF16), pltpu.VMEM((512, m_len), BF16),
                            pltpu.VMEM((m_len, TQ_MEM), F32), pltpu.VMEM((M_HEADS * m_len, TQ_MEM), BF16),
                            pltpu.VMEM((512, TQ_MEM), F32)],
            compiler_params=_params(2), name="mem_attn" if bounded else "mem_attn_general")

    gain_max = jnp.max(jnp.abs(cols), axis=0)

    def score_bound(cq_, ck_, dim):
        return dim ** 0.5 * LOG2E * 1.02 * gain_max[cq_] * gain_max[ck_]

    def attention(bounded):
        def run(*ops):
            return (dsa(bounded)(*ops[:6]), mla(bounded)(*ops[6:9]), mem_attn(bounded)(*ops[9:]))
        return run

    k_mem_gain = jnp.max(jnp.abs(g_kn_m[0]))
    worst = jnp.maximum(jnp.maximum(score_bound(COL_GQA, COL_GKA, A_HEAD_DIM), score_bound(COL_GQB, COL_GKB, B_QK)),
                        M_HEAD_DIM ** 0.5 * LOG2E * 1.02 * gain_max[COL_GQM] * k_mem_gain)
    oa, ob, om = lax.cond(worst <= BOUNDED_SCORE_LIMIT, attention(True), attention(False),
                          qat, qib, wt, ka, vat, ki, qbt, kb, vbt, qmt, mem.reshape(b * m_len, d), mem_gains,
                          wmk, wmv_t)

    out = pl.pallas_call(
        _final_kernel,
        grid=(n // TM,),
        in_specs=[tile(d), tile(512), tile(512), tile(512), _full(rows.shape), _full(wz.shape), _full(wg.shape),
                  _full(wb.shape), _full(wo.shape)],
        out_specs=tile(d), out_shape=jax.ShapeDtypeStruct((n, d), x.dtype),
        compiler_params=_params(1), name="final",
    )(x2, oa, ob, om, rows, wz, wg, wb, wo)
    return out.reshape(b, s, d)
```

```python
import functools

import numpy as np
import jax
import jax.numpy as jnp
from jax import lax
from jax.experimental import pallas as pl
from jax.experimental.pallas import tpu as pltpu

F32 = jnp.float32
BF16 = jnp.bfloat16
I32 = jnp.int32

D_MODEL = 1024
ROPE_THETA = 500000.0
EPS = 1e-6
NEG = -1e30
N_BRANCH = 3
BRANCH_WIDTH = 512
A_HEADS = 8
A_HEAD_DIM = 64
A_ROT = A_HEAD_DIM // 4
IDX_HEADS = 8
IDX_DIM = 64
TOPK_MAX = 256
B_HEADS = 8
B_NOPE = 64
B_ROPE = 32
B_VDIM = 64
B_QK = B_NOPE + B_ROPE
B_Q_RANK = 384
B_KV_RANK = 256
M_HEADS = 4
M_HEAD_DIM = 128

LANES = 128
TM = 512
TQ = 256
TQ_MEM = 512
KC = 256
COUNT_ROWS = 64
VMEM_LIMIT = 56 * 1024 * 1024
INT_MIN = -2 ** 31
LOG2E = 1.4426950408889634
BOUNDED_SCORE_LIMIT = 32.0


def _nt(a, b):
    return lax.dot_general(a, b, (((1,), (1,)), ((), ())), preferred_element_type=F32)


def _mm(a, b):
    return jnp.dot(a, b, preferred_element_type=F32)


def _rms_lanes(xf, g_row, n=None):
    n = xf.shape[-1] if n is None else n
    ms = jnp.sum(xf * xf, axis=-1, keepdims=True) / n
    return xf * lax.rsqrt(ms + EPS) * g_row


def _rms_rows(blk, g_col, n=None):
    n = blk.shape[0] if n is None else n
    ms = jnp.sum(blk * blk, axis=0, keepdims=True) / n
    return blk * lax.rsqrt(ms + EPS) * g_col


def _rope_rows(blk, lo, half, cos_t, sin_t):
    x1 = blk[lo:lo + half]
    x2 = blk[lo + half:lo + 2 * half]
    parts = []
    if lo:
        parts.append(blk[:lo])
    parts += [x1 * cos_t - x2 * sin_t, x2 * cos_t + x1 * sin_t]
    if lo + 2 * half < blk.shape[0]:
        parts.append(blk[lo + 2 * half:])
    return jnp.concatenate(parts, axis=0)


def _token_major(blocks):
    rows = sum(blk.shape[0] for blk in blocks)
    if rows < LANES:
        blocks = list(blocks) + [jnp.zeros((LANES - rows, blocks[0].shape[1]), F32)]
    return jnp.concatenate(blocks, axis=0).T.astype(BF16)


COL_GQA, COL_GKA, COL_GQB, COL_GKB, COL_GQM, COL_INVA, COL_INVB, N_COLS = 0, 1, 2, 3, 4, 5, 6, 7
ROW_GN, ROW_GCQ, ROW_GCKV, N_ROWS = 0, 1, 2, 3


def _proj_kernel(x_ref, posr_ref, rows_ref, cols_ref,
                 wqa_ref, wqi_ref, wwi_ref, wka_ref, wva_ref, wki_ref,
                 wcq_ref, wckv_ref, wkr_ref, wqm_ref, wuq_ref, wuk_ref, wuv_ref,
                 qat_ref, qib_ref, wt_ref, ka_ref, vat_ref, ki_ref, qbt_ref, kb_ref, vbt_ref, qmt_ref):
    rows, cols = rows_ref[...], cols_ref[...]
    col = lambda j, n: cols[0:n, j:j + 1]
    h = _rms_lanes(x_ref[...], rows[ROW_GN:ROW_GN + 1, :]).astype(BF16)
    pos = posr_ref[...].astype(F32)
    half_a, half_b = A_ROT // 2, B_ROPE // 2
    ang_a = col(COL_INVA, half_a) * pos
    cos_a, sin_a = jnp.cos(ang_a), jnp.sin(ang_a)
    ang_b = col(COL_INVB, half_b) * pos
    cos_b, sin_b = jnp.cos(ang_b), jnp.sin(ang_b)

    cq = _mm(h, wcq_ref[...])
    ckv = _mm(h, wckv_ref[...])
    qa = _nt(wqa_ref[...], h)
    cq = _rms_lanes(cq, rows[ROW_GCQ:ROW_GCQ + 1, 0:B_Q_RANK]).astype(BF16)
    ckv = _rms_lanes(ckv, rows[ROW_GCKV:ROW_GCKV + 1, 0:B_KV_RANK]).astype(BF16)
    qb = _nt(wuq_ref[...], cq)
    kn = _nt(wuk_ref[...], ckv)
    kr = _nt(wkr_ref[...], h)

    gq = col(COL_GQA, A_HEAD_DIM)
    for hh in range(A_HEADS):
        blk = _rms_rows(qa[hh * A_HEAD_DIM:(hh + 1) * A_HEAD_DIM], gq)
        blk = _rope_rows(blk, 0, half_a, cos_a, sin_a) * (A_HEAD_DIM ** -0.5 * LOG2E)
        own = hh * LANES + (hh % 2) * A_HEAD_DIM
        other = hh * LANES + (1 - hh % 2) * A_HEAD_DIM
        qat_ref[own:own + A_HEAD_DIM, :] = blk.astype(BF16)
        qat_ref[other:other + A_HEAD_DIM, :] = jnp.zeros((A_HEAD_DIM, TM), BF16)

    ka = _nt(wka_ref[...], h)
    qm = _nt(wqm_ref[...], h)

    gq = col(COL_GQB, LANES)
    for hh in range(B_HEADS):
        blk = _rms_rows(qb[hh * LANES:(hh + 1) * LANES], gq, n=B_QK)
        blk = _rope_rows(blk, B_NOPE, half_b, cos_b, sin_b) * (B_QK ** -0.5 * LOG2E)
        qbt_ref[hh * LANES:(hh + 1) * LANES, :] = blk.astype(BF16)
    gk = col(COL_GKB, LANES)
    pad = jnp.zeros((LANES - B_QK, TM), F32)
    for hh in range(B_HEADS):
        blk = jnp.concatenate([kn[hh * B_NOPE:(hh + 1) * B_NOPE], kr, pad], axis=0)
        blk = _rope_rows(_rms_rows(blk, gk, n=B_QK), B_NOPE, half_b, cos_b, sin_b)
        kb_ref[:, hh * LANES:(hh + 1) * LANES] = _token_major([blk])

    vbt_ref[...] = _nt(wuv_ref[...], ckv).astype(BF16)
    qi = _nt(wqi_ref[...], h)

    gk = col(COL_GKA, A_HEAD_DIM)
    for c in range(A_HEADS // 2):
        pair = [_rope_rows(_rms_rows(ka[hh * A_HEAD_DIM:(hh + 1) * A_HEAD_DIM], gk), 0, half_a, cos_a, sin_a)
                for hh in (2 * c, 2 * c + 1)]
        ka_ref[c, :, :] = _token_major(pair)
    gm = col(COL_GQM, M_HEAD_DIM)
    for hh in range(M_HEADS):
        blk = _rms_rows(qm[hh * M_HEAD_DIM:(hh + 1) * M_HEAD_DIM], gm) * (M_HEAD_DIM ** -0.5 * LOG2E)
        qmt_ref[hh * M_HEAD_DIM:(hh + 1) * M_HEAD_DIM, :] = blk.astype(BF16)

    vat_ref[...] = _nt(wva_ref[...], h).astype(BF16)
    wt_ref[...] = _nt(wwi_ref[...], h)[0:IDX_HEADS] * (IDX_HEADS ** -0.5)
    ki = _nt(wki_ref[...], h)

    for hh in range(IDX_HEADS):
        blk = _rope_rows(qi[hh * IDX_DIM:(hh + 1) * IDX_DIM], 0, half_a, cos_a, sin_a)
        blk = (blk * (IDX_DIM ** -0.5)).astype(BF16)
        for j in range(TM // TQ):
            qib_ref[j, 0:IDX_DIM, hh * TQ:(hh + 1) * TQ] = blk[:, j * TQ:(j + 1) * TQ]
    qib_ref[:, IDX_DIM:, :] = jnp.zeros((TM // TQ, LANES - IDX_DIM, IDX_HEADS * TQ), BF16)
    ki_ref[...] = _token_major([_rope_rows(ki, 0, half_a, cos_a, sin_a)])


def _mem_kv(mem_ref, gains_ref, wk_ref, wvt_ref, km_ref, vmt_ref):
    hm = _rms_lanes(mem_ref[...], gains_ref[0:1, :]).astype(BF16)
    k = _mm(hm, wk_ref[...])
    gk = gains_ref[1:2, 0:M_HEAD_DIM]
    for hh in range(M_HEADS):
        kc = _rms_lanes(k[:, hh * M_HEAD_DIM:(hh + 1) * M_HEAD_DIM], gk)
        km_ref[:, hh * M_HEAD_DIM:(hh + 1) * M_HEAD_DIM] = kc.astype(BF16)
    vmt_ref[...] = _nt(wvt_ref[...], hm).astype(BF16)


def _attend(nk, n_heads, dv, q_of, k_of, v_of, bias_of, bounded, s_ref, p_ref, ot_ref):
    nq = ot_ref.shape[1]
    chunks = [slice(c * KC, (c + 1) * KC) for c in range(nk // KC)]

    def scores(hh, q, c):
        s = _mm(k_of(hh, chunks[c]), q)
        b = bias_of(c)
        return s if b is None else s + b

    depth = max(1, min(n_heads - 1, 8 // len(chunks)))
    slots = depth + 1

    def p_rows(hh, sl):
        base = (hh % slots) * nk
        return slice(base + sl.start, base + sl.stop)

    def probabilities(hh):
        q = q_of(hh)
        if bounded:
            l8 = jnp.zeros((8, nq), F32)
            for c in range(len(chunks)):
                p = jnp.exp2(scores(hh, q, c))
                l8 = l8 + p.reshape(KC // 8, 8, nq).sum(axis=0)
                p_ref[p_rows(hh, chunks[c]), :] = p.astype(BF16)
            return jnp.sum(l8, axis=0, keepdims=True)
        m = jnp.full((1, nq), -jnp.inf, F32)
        for c in range(len(chunks)):
            s = scores(hh, q, c)
            s_ref[chunks[c], :] = s
            m = jnp.maximum(m, jnp.max(s, axis=0, keepdims=True))
        l = jnp.zeros((1, nq), F32)
        for c in range(len(chunks)):
            p = jnp.exp2(s_ref[chunks[c], :] - m)
            l = l + jnp.sum(p, axis=0, keepdims=True)
            p_ref[p_rows(hh, chunks[c]), :] = p.astype(BF16)
        return l

    def weighted_values(hh, l):
        o = _mm(v_of(hh, slice(0, nk)), p_ref[p_rows(hh, slice(0, nk)), :])
        ot_ref[hh * dv:(hh + 1) * dv, :] = o / l

    sums = {}
    for step in range(n_heads + depth):
        if step < n_heads:
            sums[step] = probabilities(step)
        if step >= depth:
            weighted_values(step - depth, sums.pop(step - depth))


def _count(score_ref, nk, pred):
    cnt = jnp.zeros((COUNT_ROWS, TQ), I32)
    for r in range(0, nk, COUNT_ROWS):
        cnt = jnp.where(pred(score_ref[r:r + COUNT_ROWS, :]), cnt + 1, cnt)
    return jnp.sum(cnt, axis=0, keepdims=True)


def _ordered_to_bits(u, magnitude_mask):
    k = u ^ INT_MIN
    return k ^ ((k >> 31) & magnitude_mask)


def _ordered_pattern_to_float(u):
    return pltpu.bitcast(_ordered_to_bits(u, 0x7FFFFFFF), F32)


def _count_rounded(round_ref, nk, cand):
    assert nk // COUNT_ROWS <= 256
    one, zero = jnp.ones((), BF16), jnp.zeros((), BF16)
    cnt = jnp.zeros((COUNT_ROWS, TQ), BF16)
    for r in range(0, nk, COUNT_ROWS):
        cnt = cnt + jnp.where(round_ref[r:r + COUNT_ROWS, :] >= cand, one, zero)
    return jnp.sum(cnt.astype(F32), axis=0, keepdims=True)


def _select_topk(nk, q_pos, row, chunks, qib_ref, wt_ref, ki_ref, score_ref, round_ref, emit):
    for c, sl in enumerate(chunks):
        ki_c = ki_ref[sl, :]
        acc = jnp.zeros((KC, TQ), F32)
        for hh in range(IDX_HEADS):
            d = _mm(ki_c, qib_ref[0, :, hh * TQ:(hh + 1) * TQ])
            acc = acc + jnp.maximum(d, 0.0) * wt_ref[hh:hh + 1, :]
        score = jnp.where(row + c * KC <= q_pos, acc, NEG)
        score_ref[sl, :] = score
        round_ref[sl, :] = score.astype(BF16)

    def coarse(i, c_u):
        cand_u = c_u | jnp.left_shift(jnp.int32(1), 31 - i)
        cand = pltpu.bitcast(_ordered_to_bits(cand_u, 0x7FFF0000), F32).astype(BF16)
        return jnp.where(_count_rounded(round_ref, nk, cand) >= TOPK_MAX, cand_u, c_u)

    c_u = lax.fori_loop(0, 16, coarse, jnp.zeros((1, TQ), I32))
    pred_bits = _ordered_to_bits(c_u - (1 << 16), 0x7FFF0000)
    base_u = (pred_bits ^ ((pred_bits >> 31) & 0x7FFFFFFF)) ^ INT_MIN

    def fine(i, carry):
        off, cnt_t = carry
        cand_off = off | jnp.left_shift(jnp.int32(1), 16 - i)
        cand = _ordered_pattern_to_float(base_u + cand_off)
        cnt = _count(score_ref, nk, lambda x: x >= cand)
        ok = cnt >= TOPK_MAX
        return jnp.where(ok, cand_off, off), jnp.where(ok, cnt, cnt_t)

    off, cnt_t = lax.fori_loop(0, 17, fine, (jnp.zeros((1, TQ), I32), jnp.full((1, TQ), nk, I32)))
    thr = _ordered_pattern_to_float(base_u + off)
    split_ties = jnp.max(jnp.where(cnt_t > TOPK_MAX, 1, 0)) > 0

    @pl.when(jnp.logical_not(split_ties))
    def _():
        for c, sl in enumerate(chunks):
            emit(sl, (score_ref[sl, :] >= thr) & (row + c * KC <= q_pos))

    @pl.when(split_ties)
    def _():
        room = (TOPK_MAX - _count(score_ref, nk, lambda x: x > thr)).astype(F32)
        tri = lax.broadcasted_iota(I32, (KC, KC), 0) >= lax.broadcasted_iota(I32, (KC, KC), 1)
        tri = jnp.where(tri, 1.0, 0.0).astype(BF16)
        running = jnp.zeros((1, TQ), F32)
        for c, sl in enumerate(chunks):
            x = score_ref[sl, :]
            tie = x == thr
            rank = _mm(tri, jnp.where(tie, 1.0, 0.0).astype(BF16)) + running
            running = rank[KC - 1:KC, :]
            emit(sl, ((x > thr) | (tie & (rank <= room))) & (row + c * KC <= q_pos))


def _dsa_body(nk, start, bounded, qat_ref, qib_ref, wt_ref, ka_ref, vat_ref, ki_ref, oa_ref,
              score_ref, bias_ref, s_ref, p_ref, ot_ref):
    q_pos = start + lax.broadcasted_iota(I32, (1, TQ), 1)
    row = lax.broadcasted_iota(I32, (KC, TQ), 0)
    chunks = [slice(c * KC, (c + 1) * KC) for c in range(nk // KC)]

    def emit_bias(sl, keep):
        bias_ref[sl, :] = jnp.where(keep, 0.0, NEG)

    if nk <= TOPK_MAX:
        for c, sl in enumerate(chunks):
            emit_bias(sl, row + c * KC <= q_pos)
    else:
        _select_topk(nk, q_pos, row, chunks, qib_ref, wt_ref, ki_ref, score_ref, p_ref, emit_bias)

    def q_of(hh):
        return qat_ref[hh * LANES:(hh + 1) * LANES, :]

    def k_of(hh, sl):
        return ka_ref[hh // 2, sl, :]

    def v_of(hh, sl):
        return vat_ref[hh * A_HEAD_DIM:(hh + 1) * A_HEAD_DIM, sl]

    _attend(nk, A_HEADS, A_HEAD_DIM, q_of, k_of, v_of, lambda c: bias_ref[chunks[c], :], bounded,
            s_ref, p_ref, ot_ref)
    oa_ref[...] = ot_ref[...].T


def _dsa_kernel(qat_ref, qib_ref, wt_ref, ka_ref, vat_ref, ki_ref, oa_ref,
                score_ref, bias_ref, s_ref, p_ref, ot_ref, *, seq, bounded):
    qb = pl.program_id(1)
    for cls in range(seq // TQ):
        @pl.when(qb == cls)
        def _():
            _dsa_body(TQ * (cls + 1), cls * TQ, bounded, qat_ref, qib_ref, wt_ref, ka_ref, vat_ref,
                      ki_ref, oa_ref, score_ref, bias_ref, s_ref, p_ref, ot_ref)


def _mla_body(nk, start, bounded, qbt_ref, kb_ref, vbt_ref, ob_ref, bias_ref, s_ref, p_ref, ot_ref):
    last = nk // KC - 1
    q_pos = start + lax.broadcasted_iota(I32, (1, TQ), 1)
    row = lax.broadcasted_iota(I32, (KC, TQ), 0)
    bias_ref[0:KC, :] = jnp.where(row + last * KC <= q_pos, 0.0, NEG)

    def q_of(hh):
        return qbt_ref[hh * LANES:(hh + 1) * LANES, :]

    def k_of(hh, sl):
        return kb_ref[sl, hh * LANES:(hh + 1) * LANES]

    def v_of(hh, sl):
        return vbt_ref[hh * B_VDIM:(hh + 1) * B_VDIM, sl]

    _attend(nk, B_HEADS, B_VDIM, q_of, k_of, v_of, lambda c: bias_ref[0:KC, :] if c == last else None,
            bounded, s_ref, p_ref, ot_ref)
    ob_ref[...] = ot_ref[...].T


def _mla_kernel(qbt_ref, kb_ref, vbt_ref, ob_ref, bias_ref, s_ref, p_ref, ot_ref, *, seq, bounded):
    qb = pl.program_id(1)
    for cls in range(seq // TQ):
        @pl.when(qb == cls)
        def _():
            _mla_body(TQ * (cls + 1), cls * TQ, bounded, qbt_ref, kb_ref, vbt_ref, ob_ref,
                      bias_ref, s_ref, p_ref, ot_ref)


def _mem_attn_kernel(qmt_ref, mem_ref, gains_ref, wk_ref, wvt_ref, om_ref, km_ref, vmt_ref, s_ref, p_ref, ot_ref,
                     *, mem_len, bounded):
    @pl.when(pl.program_id(1) == 0)
    def _():
        _mem_kv(mem_ref, gains_ref, wk_ref, wvt_ref, km_ref, vmt_ref)

    def q_of(hh):
        return qmt_ref[hh * M_HEAD_DIM:(hh + 1) * M_HEAD_DIM, :]

    def k_of(hh, sl):
        return km_ref[sl, hh * M_HEAD_DIM:(hh + 1) * M_HEAD_DIM]

    def v_of(hh, sl):
        return vmt_ref[hh * M_HEAD_DIM:(hh + 1) * M_HEAD_DIM, sl]

    _attend(mem_len, M_HEADS, M_HEAD_DIM, q_of, k_of, v_of, lambda c: None, bounded, s_ref, p_ref, ot_ref)
    om_ref[...] = ot_ref[...].T


def _final_kernel(x_ref, oa_ref, ob_ref, om_ref, rows_ref, wz_ref, wg_ref, wb_ref, wo_ref, out_ref):
    x = x_ref[...]
    h = _rms_lanes(x, rows_ref[ROW_GN:ROW_GN + 1, :]).astype(BF16)
    merged = jnp.zeros((TM, D_MODEL), F32)
    for n, o_ref in enumerate((oa_ref, ob_ref, om_ref)):
        z = _mm(h, wz_ref[n])
        y = (o_ref[...] * (z * jax.nn.sigmoid(z))).astype(BF16)
        branch = _mm(y, wb_ref[n])
        gate = jax.nn.sigmoid(_mm(h, wg_ref[:, n * D_MODEL:(n + 1) * D_MODEL]))
        merged = merged + gate * branch
    out_ref[...] = x + _mm(merged.astype(BF16), wo_ref[...])


def _full(shape):
    return pl.BlockSpec(shape, lambda *_: (0,) * len(shape), pipeline_mode=pl.Buffered(1))


def _params(n_axes):
    return pltpu.CompilerParams(dimension_semantics=("arbitrary",) * n_axes,
                                vmem_limit_bytes=VMEM_LIMIT)


def kernel(x, mem, positions, g_norm, w_in, g_qn_a, g_kn_a, g_cq, g_ckv, w_uq, w_ukv, g_qn_b, g_kn_b,
           g_mem, w_mem_kv, g_qn_m, g_kn_m, w_branch, w_out):
    b, s, d = x.shape
    m_len = mem.shape[1]
    n = b * s
    nq = s // TQ
    assert d == D_MODEL and s % TQ == 0 and TQ == KC and TM % TQ == 0 and n % TM == 0 and m_len % KC == 0
    assert g_norm.shape[0] == 1, "single-layer block"

    w = w_in[0]
    off = np.cumsum([0, 512, 512, 512, 512, IDX_DIM, IDX_HEADS, BRANCH_WIDTH, B_Q_RANK, B_KV_RANK, B_ROPE,
                     BRANCH_WIDTH, M_HEADS * M_HEAD_DIM, BRANCH_WIDTH, N_BRANCH * D_MODEL])
    seg = [w[:, off[i]:off[i + 1]] for i in range(14)]
    (w_qa, w_ka, w_va, w_qi, w_ki, w_wi, w_za, w_cq, w_ckv, w_kr, w_zb, w_qm, w_zm, w_gate) = seg
    bf = lambda a: a.astype(BF16)
    wqa_t, wqi_t, wva_t, wqm_t = bf(w_qa.T), bf(w_qi.T), bf(w_va.T), bf(w_qm.T)
    wka_t, wki_t, wkr_t = bf(w_ka.T), bf(w_ki.T), bf(w_kr.T)
    wwi_t = bf(jnp.pad(w_wi.T, ((0, 16 - IDX_HEADS), (0, 0))))
    wuq_t = bf(jnp.pad(w_uq[0].reshape(B_Q_RANK, B_HEADS, B_QK), ((0, 0), (0, 0), (0, LANES - B_QK)))
               .reshape(B_Q_RANK, B_HEADS * LANES).T)
    ukv = w_ukv[0].reshape(B_KV_RANK, B_HEADS, B_NOPE + B_VDIM)
    wuk_t = bf(ukv[:, :, :B_NOPE].reshape(B_KV_RANK, B_HEADS * B_NOPE).T)
    wuv_t = bf(ukv[:, :, B_NOPE:].reshape(B_KV_RANK, B_HEADS * B_VDIM).T)
    wmk = bf(w_mem_kv[0][:, :M_HEADS * M_HEAD_DIM])
    wmv_t = bf(w_mem_kv[0][:, M_HEADS * M_HEAD_DIM:].T)
    wz = bf(jnp.stack([w_za, w_zb, w_zm]))
    wg = bf(w_gate)
    wb = bf(w_branch[0])
    wo = bf(w_out[0])

    pad_to = lambda v, size: jnp.pad(v, (0, size - v.shape[0]))
    inv_a = ROPE_THETA ** (-(jnp.arange(0, A_ROT, 2, dtype=F32) / A_ROT))
    inv_b = ROPE_THETA ** (-(jnp.arange(0, B_ROPE, 2, dtype=F32) / B_ROPE))
    col_vectors = [None] * N_COLS
    col_vectors[COL_GQA], col_vectors[COL_GKA] = g_qn_a[0], g_kn_a[0]
    col_vectors[COL_GQB], col_vectors[COL_GKB] = g_qn_b[0], g_kn_b[0]
    col_vectors[COL_GQM], col_vectors[COL_INVA], col_vectors[COL_INVB] = g_qn_m[0], inv_a, inv_b
    cols = jnp.stack([pad_to(v, LANES) for v in col_vectors], axis=1)
    row_vectors = [None] * N_ROWS
    row_vectors[ROW_GN], row_vectors[ROW_GCQ], row_vectors[ROW_GCKV] = g_norm[0], g_cq[0], g_ckv[0]
    rows = jnp.stack([pad_to(v, D_MODEL) for v in row_vectors])
    mem_gains = jnp.stack([g_mem[0], pad_to(g_kn_m[0], D_MODEL)])

    x2 = x.reshape(n, d)
    pos_r = positions.reshape(1, n)
    tile = lambda width: pl.BlockSpec((TM, width), lambda i: (i, 0))
    tile_t = lambda rows: pl.BlockSpec((rows, TM), lambda i: (0, i))
    pos_spec = pl.BlockSpec((1, TM), lambda i: (0, i))

    a_w = [wqa_t, wqi_t, wwi_t, wka_t, wva_t, wki_t]
    b_w = [bf(w_cq), bf(w_ckv), wkr_t, wqm_t, wuq_t, wuk_t, wuv_t]
    qat, qib, wt, ka, vat, ki, qbt, kb, vbt, qmt = pl.pallas_call(
        _proj_kernel,
        grid=(n // TM,),
        in_specs=[tile(d), pos_spec, _full(rows.shape), _full(cols.shape)] + [_full(a.shape) for a in a_w + b_w],
        out_specs=[tile_t(A_HEADS * LANES), pl.BlockSpec((TM // TQ, LANES, IDX_HEADS * TQ), lambda i: (i, 0, 0)),
                   tile_t(IDX_HEADS), pl.BlockSpec((512 // LANES, TM, LANES), lambda i: (0, i, 0)),
                   tile_t(512), tile(LANES),
                   tile_t(B_HEADS * LANES), tile(B_HEADS * LANES), tile_t(512), tile_t(512)],
        out_shape=[jax.ShapeDtypeStruct((A_HEADS * LANES, n), BF16),
                   jax.ShapeDtypeStruct((n // TQ, LANES, IDX_HEADS * TQ), BF16),
                   jax.ShapeDtypeStruct((IDX_HEADS, n), F32),
                   jax.ShapeDtypeStruct((512 // LANES, n, LANES), BF16),
                   jax.ShapeDtypeStruct((512, n), BF16),
                   jax.ShapeDtypeStruct((n, LANES), BF16),
                   jax.ShapeDtypeStruct((B_HEADS * LANES, n), BF16),
                   jax.ShapeDtypeStruct((n, B_HEADS * LANES), BF16),
                   jax.ShapeDtypeStruct((512, n), BF16),
                   jax.ShapeDtypeStruct((512, n), BF16)],
        compiler_params=_params(1), name="proj",
    )(x2, pos_r, rows, cols, *a_w, *b_w)

    qcol = lambda rows: pl.BlockSpec((rows, TQ), lambda bi, qi: (0, bi * nq + qi))
    seq_rows = lambda width: pl.BlockSpec((s, width), lambda bi, qi: (bi, 0))
    seq_cols = lambda rows: pl.BlockSpec((rows, s), lambda bi, qi: (0, bi))
    o_spec = pl.BlockSpec((TQ, 512), lambda bi, qi: (bi * nq + qi, 0))
    o_shape = jax.ShapeDtypeStruct((n, 512), F32)
    attn_scratch = [pltpu.VMEM((s, TQ), F32), pltpu.VMEM((s, TQ), F32), pltpu.VMEM((2 * s, TQ), BF16),
                    pltpu.VMEM((512, TQ), F32)]

    def dsa(bounded):
        return pl.pallas_call(
            functools.partial(_dsa_kernel, seq=s, bounded=bounded),
            grid=(b, nq),
            in_specs=[qcol(A_HEADS * LANES),
                      pl.BlockSpec((1, LANES, IDX_HEADS * TQ), lambda bi, qi: (bi * nq + qi, 0, 0)),
                      qcol(IDX_HEADS), pl.BlockSpec((512 // LANES, s, LANES), lambda bi, qi: (0, bi, 0)),
                      seq_cols(512), seq_rows(LANES)],
            out_specs=o_spec, out_shape=o_shape,
            scratch_shapes=[pltpu.VMEM((s, TQ), F32)] + attn_scratch,
            compiler_params=_params(2), name="dsa" if bounded else "dsa_general")

    def mla(bounded):
        return pl.pallas_call(
            functools.partial(_mla_kernel, seq=s, bounded=bounded),
            grid=(b, nq),
            in_specs=[qcol(B_HEADS * LANES), seq_rows(B_HEADS * LANES), seq_cols(512)],
            out_specs=o_spec, out_shape=o_shape,
            scratch_shapes=attn_scratch,
            compiler_params=_params(2), name="mla" if bounded else "mla_general")

    nq_mem = s // TQ_MEM

    def mem_attn(bounded):
        return pl.pallas_call(
            functools.partial(_mem_attn_kernel, mem_len=m_len, bounded=bounded),
            grid=(b, nq_mem),
            in_specs=[pl.BlockSpec((512, TQ_MEM), lambda bi, qi: (0, bi * nq_mem + qi)),
                      pl.BlockSpec((m_len, d), lambda bi, qi: (bi, 0)),
                      _full(mem_gains.shape), _full(wmk.shape), _full(wmv_t.shape)],
            out_specs=pl.BlockSpec((TQ_MEM, 512), lambda bi, qi: (bi * nq_mem + qi, 0)), out_shape=o_shape,
            scratch_shapes=[pltpu.VMEM((m_len, 512), BF16), pltpu.VMEM((512, m_len), BF16),
                            pltpu.VMEM((m_len, TQ_MEM), F32), pltpu.VMEM((M_HEADS * m_len, TQ_MEM), BF16),
                            pltpu.VMEM((512, TQ_MEM), F32)],
            compiler_params=_params(2), name="mem_attn" if bounded else "mem_attn_general")

    gain_max = jnp.max(jnp.abs(cols), axis=0)

    def score_bound(cq_, ck_, dim):
        return dim ** 0.5 * LOG2E * 1.02 * gain_max[cq_] * gain_max[ck_]

    def attention(bounded):
        def run(*ops):
            return (dsa(bounded)(*ops[:6]), mla(bounded)(*ops[6:9]), mem_attn(bounded)(*ops[9:]))
        return run

    k_mem_gain = jnp.max(jnp.abs(g_kn_m[0]))
    worst = jnp.maximum(jnp.maximum(score_bound(COL_GQA, COL_GKA, A_HEAD_DIM), score_bound(COL_GQB, COL_GKB, B_QK)),
                        M_HEAD_DIM ** 0.5 * LOG2E * 1.02 * gain_max[COL_GQM] * k_mem_gain)
    oa, ob, om = lax.cond(worst <= BOUNDED_SCORE_LIMIT, attention(True), attention(False),
                          qat, qib, wt, ka, vat, ki, qbt, kb, vbt, qmt, mem.reshape(b * m_len, d), mem_gains,
                          wmk, wmv_t)

    out = pl.pallas_call(
        _final_kernel,
        grid=(n // TM,),
        in_specs=[tile(d), tile(512), tile(512), tile(512), _full(rows.shape), _full(wz.shape), _full(wg.shape),
                  _full(wb.shape), _full(wo.shape)],
        out_specs=tile(d), out_shape=jax.ShapeDtypeStruct((n, d), x.dtype),
        compiler_params=_params(1), name="final",
    )(x2, oa, ob, om, rows, wz, wg, wb, wo)
    return out.reshape(b, s, d)
```

```python
import functools

import numpy as np
import jax
import jax.numpy as jnp
from jax import lax
from jax.experimental import pallas as pl
from jax.experimental.pallas import tpu as pltpu

F32 = jnp.float32
BF16 = jnp.bfloat16
I32 = jnp.int32

D_MODEL = 1024
ROPE_THETA = 500000.0
EPS = 1e-6
NEG = -1e30
N_BRANCH = 3
BRANCH_WIDTH = 512
A_HEADS = 8
A_HEAD_DIM = 64
A_ROT = A_HEAD_DIM // 4
IDX_HEADS = 8
IDX_DIM = 64
TOPK_MAX = 256
B_HEADS = 8
B_NOPE = 64
B_ROPE = 32
B_VDIM = 64
B_QK = B_NOPE + B_ROPE
B_Q_RANK = 384
B_KV_RANK = 256
M_HEADS = 4
M_HEAD_DIM = 128

LANES = 128
TM = 512
TQ = 256
TQ_MEM = 512
KC = 256
COUNT_ROWS = 64
VMEM_LIMIT = 56 * 1024 * 1024
INT_MIN = -2 ** 31
LOG2E = 1.4426950408889634
BOUNDED_SCORE_LIMIT = 32.0


def _nt(a, b):
    return lax.dot_general(a, b, (((1,), (1,)), ((), ())), preferred_element_type=F32)


def _mm(a, b):
    return jnp.dot(a, b, preferred_element_type=F32)


def _rms_lanes(xf, g_row, n=None):
    n = xf.shape[-1] if n is None else n
    ms = jnp.sum(xf * xf, axis=-1, keepdims=True) / n
    return xf * lax.rsqrt(ms + EPS) * g_row


def _rms_rows(blk, g_col, n=None):
    n = blk.shape[0] if n is None else n
    ms = jnp.sum(blk * blk, axis=0, keepdims=True) / n
    return blk * lax.rsqrt(ms + EPS) * g_col


def _rope_rows(blk, lo, half, cos_t, sin_t):
    x1 = blk[lo:lo + half]
    x2 = blk[lo + half:lo + 2 * half]
    parts = []
    if lo:
        parts.append(blk[:lo])
    parts += [x1 * cos_t - x2 * sin_t, x2 * cos_t + x1 * sin_t]
    if lo + 2 * half < blk.shape[0]:
        parts.append(blk[lo + 2 * half:])
    return jnp.concatenate(parts, axis=0)


def _token_major(blocks):
    rows = sum(blk.shape[0] for blk in blocks)
    if rows < LANES:
        blocks = list(blocks) + [jnp.zeros((LANES - rows, blocks[0].shape[1]), F32)]
    return jnp.concatenate(blocks, axis=0).T.astype(BF16)


COL_GQA, COL_GKA, COL_GQB, COL_GKB, COL_GQM, COL_INVA, COL_INVB, N_COLS = 0, 1, 2, 3, 4, 5, 6, 7
ROW_GN, ROW_GCQ, ROW_GCKV, N_ROWS = 0, 1, 2, 3


def _proj_kernel(x_ref, posr_ref, rows_ref, cols_ref,
                 wqa_ref, wqi_ref, wwi_ref, wka_ref, wva_ref, wki_ref,
                 wcq_ref, wckv_ref, wkr_ref, wqm_ref, wuq_ref, wuk_ref, wuv_ref,
                 qat_ref, qib_ref, wt_ref, ka_ref, vat_ref, ki_ref, qbt_ref, kb_ref, vbt_ref, qmt_ref):
    rows, cols = rows_ref[...], cols_ref[...]
    col = lambda j, n: cols[0:n, j:j + 1]
    h = _rms_lanes(x_ref[...], rows[ROW_GN:ROW_GN + 1, :]).astype(BF16)
    pos = posr_ref[...].astype(F32)
    half_a, half_b = A_ROT // 2, B_ROPE // 2
    ang_a = col(COL_INVA, half_a) * pos
    cos_a, sin_a = jnp.cos(ang_a), jnp.sin(ang_a)
    ang_b = col(COL_INVB, half_b) * pos
    cos_b, sin_b = jnp.cos(ang_b), jnp.sin(ang_b)

    cq = _mm(h, wcq_ref[...])
    ckv = _mm(h, wckv_ref[...])
    qa = _nt(wqa_ref[...], h)
    cq = _rms_lanes(cq, rows[ROW_GCQ:ROW_GCQ + 1, 0:B_Q_RANK]).astype(BF16)
    ckv = _rms_lanes(ckv, rows[ROW_GCKV:ROW_GCKV + 1, 0:B_KV_RANK]).astype(BF16)
    qb = _nt(wuq_ref[...], cq)
    kn = _nt(wuk_ref[...], ckv)
    kr = _nt(wkr_ref[...], h)

    gq = col(COL_GQA, A_HEAD_DIM)
    for hh in range(A_HEADS):
        blk = _rms_rows(qa[hh * A_HEAD_DIM:(hh + 1) * A_HEAD_DIM], gq)
        blk = _rope_rows(blk, 0, half_a, cos_a, sin_a) * (A_HEAD_DIM ** -0.5 * LOG2E)
        own = hh * LANES + (hh % 2) * A_HEAD_DIM
        other = hh * LANES + (1 - hh % 2) * A_HEAD_DIM
        qat_ref[own:own + A_HEAD_DIM, :] = blk.astype(BF16)
        qat_ref[other:other + A_HEAD_DIM, :] = jnp.zeros((A_HEAD_DIM, TM), BF16)

    ka = _nt(wka_ref[...], h)
    qm = _nt(wqm_ref[...], h)

    gq = col(COL_GQB, LANES)
    for hh in range(B_HEADS):
        blk = _rms_rows(qb[hh * LANES:(hh + 1) * LANES], gq, n=B_QK)
        blk = _rope_rows(blk, B_NOPE, half_b, cos_b, sin_b) * (B_QK ** -0.5 * LOG2E)
        qbt_ref[hh * LANES:(hh + 1) * LANES, :] = blk.astype(BF16)
    gk = col(COL_GKB, LANES)
    pad = jnp.zeros((LANES - B_QK, TM), F32)
    for hh in range(B_HEADS):
        blk = jnp.concatenate([kn[hh * B_NOPE:(hh + 1) * B_NOPE], kr, pad], axis=0)
        blk = _rope_rows(_rms_rows(blk, gk, n=B_QK), B_NOPE, half_b, cos_b, sin_b)
        kb_ref[:, hh * LANES:(hh + 1) * LANES] = _token_major([blk])

    vbt_ref[...] = _nt(wuv_ref[...], ckv).astype(BF16)
    qi = _nt(wqi_ref[...], h)

    gk = col(COL_GKA, A_HEAD_DIM)
    for c in range(A_HEADS // 2):
        pair = [_rope_rows(_rms_rows(ka[hh * A_HEAD_DIM:(hh + 1) * A_HEAD_DIM], gk), 0, half_a, cos_a, sin_a)
                for hh in (2 * c, 2 * c + 1)]
        ka_ref[c, :, :] = _token_major(pair)
    gm = col(COL_GQM, M_HEAD_DIM)
    for hh in range(M_HEADS):
        blk = _rms_rows(qm[hh * M_HEAD_DIM:(hh + 1) * M_HEAD_DIM], gm) * (M_HEAD_DIM ** -0.5 * LOG2E)
        qmt_ref[hh * M_HEAD_DIM:(hh + 1) * M_HEAD_DIM, :] = blk.astype(BF16)

    vat_ref[...] = _nt(wva_ref[...], h).astype(BF16)
    wt_ref[...] = _nt(wwi_ref[...], h)[0:IDX_HEADS] * (IDX_HEADS ** -0.5)
    ki = _nt(wki_ref[...], h)

    for hh in range(IDX_HEADS):
        blk = _rope_rows(qi[hh * IDX_DIM:(hh + 1) * IDX_DIM], 0, half_a, cos_a, sin_a)
        blk = (blk * (IDX_DIM ** -0.5)).astype(BF16)
        for j in range(TM // TQ):
            qib_ref[j, 0:IDX_DIM, hh * TQ:(hh + 1) * TQ] = blk[:, j * TQ:(j + 1) * TQ]
    qib_ref[:, IDX_DIM:, :] = jnp.zeros((TM // TQ, LANES - IDX_DIM, IDX_HEADS * TQ), BF16)
    ki_ref[...] = _token_major([_rope_rows(ki, 0, half_a, cos_a, sin_a)])


def _mem_kv(mem_ref, gains_ref, wk_ref, wvt_ref, km_ref, vmt_ref):
    hm = _rms_lanes(mem_ref[...], gains_ref[0:1, :]).astype(BF16)
    k = _mm(hm, wk_ref[...])
    gk = gains_ref[1:2, 0:M_HEAD_DIM]
    for hh in range(M_HEADS):
        kc = _rms_lanes(k[:, hh * M_HEAD_DIM:(hh + 1) * M_HEAD_DIM], gk)
        km_ref[:, hh * M_HEAD_DIM:(hh + 1) * M_HEAD_DIM] = kc.astype(BF16)
    vmt_ref[...] = _nt(wvt_ref[...], hm).astype(BF16)


def _attend(nk, n_heads, dv, q_of, k_of, v_of, bias_of, bounded, s_ref, p_ref, ot_ref):
    nq = ot_ref.shape[1]
    chunks = [slice(c * KC, (c + 1) * KC) for c in range(nk // KC)]

    def scores(hh, q, c):
        s = _mm(k_of(hh, chunks[c]), q)
        b = bias_of(c)
        return s if b is None else s + b

    depth = max(1, min(n_heads - 1, 8 // len(chunks)))
    slots = depth + 1

    def p_rows(hh, sl):
        base = (hh % slots) * nk
        return slice(base + sl.start, base + sl.stop)

    def probabilities(hh):
        q = q_of(hh)
        if bounded:
            l8 = jnp.zeros((8, nq), F32)
            for c in range(len(chunks)):
                p = jnp.exp2(scores(hh, q, c))
                l8 = l8 + p.reshape(KC // 8, 8, nq).sum(axis=0)
                p_ref[p_rows(hh, chunks[c]), :] = p.astype(BF16)
            return jnp.sum(l8, axis=0, keepdims=True)
        m = jnp.full((1, nq), -jnp.inf, F32)
        for c in range(len(chunks)):
            s = scores(hh, q, c)
            s_ref[chunks[c], :] = s
            m = jnp.maximum(m, jnp.max(s, axis=0, keepdims=True))
        l = jnp.zeros((1, nq), F32)
        for c in range(len(chunks)):
            p = jnp.exp2(s_ref[chunks[c], :] - m)
            l = l + jnp.sum(p, axis=0, keepdims=True)
            p_ref[p_rows(hh, chunks[c]), :] = p.astype(BF16)
        return l

    def weighted_values(hh, l):
        o = _mm(v_of(hh, slice(0, nk)), p_ref[p_rows(hh, slice(0, nk)), :])
        ot_ref[hh * dv:(hh + 1) * dv, :] = o / l

    sums = {}
    for step in range(n_heads + depth):
        if step < n_heads:
            sums[step] = probabilities(step)
        if step >= depth:
            weighted_values(step - depth, sums.pop(step - depth))


def _count(score_ref, nk, pred):
    cnt = jnp.zeros((COUNT_ROWS, TQ), I32)
    for r in range(0, nk, COUNT_ROWS):
        cnt = jnp.where(pred(score_ref[r:r + COUNT_ROWS, :]), cnt + 1, cnt)
    return jnp.sum(cnt, axis=0, keepdims=True)


def _ordered_to_bits(u, magnitude_mask):
    k = u ^ INT_MIN
    return k ^ ((k >> 31) & magnitude_mask)


def _ordered_pattern_to_float(u):
    return pltpu.bitcast(_ordered_to_bits(u, 0x7FFFFFFF), F32)


def _count_rounded(round_ref, nk, cand):
    assert nk // COUNT_ROWS <= 256
    one, zero = jnp.ones((), BF16), jnp.zeros((), BF16)
    cnt = jnp.zeros((COUNT_ROWS, TQ), BF16)
    for r in range(0, nk, COUNT_ROWS):
        cnt = cnt + jnp.where(round_ref[r:r + COUNT_ROWS, :] >= cand, one, zero)
    return jnp.sum(cnt.astype(F32), axis=0, keepdims=True)


def _select_topk(nk, q_pos, row, chunks, qib_ref, wt_ref, ki_ref, score_ref, round_ref, emit):
    for c, sl in enumerate(chunks):
        ki_c = ki_ref[sl, :]
        acc = jnp.zeros((KC, TQ), F32)
        for hh in range(IDX_HEADS):
            d = _mm(ki_c, qib_ref[0, :, hh * TQ:(hh + 1) * TQ])
            acc = acc + jnp.maximum(d, 0.0) * wt_ref[hh:hh + 1, :]
        score = jnp.where(row + c * KC <= q_pos, acc, NEG)
        score_ref[sl, :] = score
        round_ref[sl, :] = score.astype(BF16)

    def coarse(i, c_u):
        cand_u = c_u | jnp.left_shift(jnp.int32(1), 31 - i)
        cand = pltpu.bitcast(_ordered_to_bits(cand_u, 0x7FFF0000), F32).astype(BF16)
        return jnp.where(_count_rounded(round_ref, nk, cand) >= TOPK_MAX, cand_u, c_u)

    c_u = lax.fori_loop(0, 16, coarse, jnp.zeros((1, TQ), I32))
    pred_bits = _ordered_to_bits(c_u - (1 << 16), 0x7FFF0000)
    base_u = (pred_bits ^ ((pred_bits >> 31) & 0x7FFFFFFF)) ^ INT_MIN

    def fine(i, carry):
        off, cnt_t = carry
        cand_off = off | jnp.left_shift(jnp.int32(1), 16 - i)
        cand = _ordered_pattern_to_float(base_u + cand_off)
        cnt = _count(score_ref, nk, lambda x: x >= cand)
        ok = cnt >= TOPK_MAX
        return jnp.where(ok, cand_off, off), jnp.where(ok, cnt, cnt_t)

    off, cnt_t = lax.fori_loop(0, 17, fine, (jnp.zeros((1, TQ), I32), jnp.full((1, TQ), nk, I32)))
    thr = _ordered_pattern_to_float(base_u + off)
    split_ties = jnp.max(jnp.where(cnt_t > TOPK_MAX, 1, 0)) > 0

    @pl.when(jnp.logical_not(split_ties))
    def _():
        for c, sl in enumerate(chunks):
            emit(sl, (score_ref[sl, :] >= thr) & (row + c * KC <= q_pos))

    @pl.when(split_ties)
    def _():
        room = (TOPK_MAX - _count(score_ref, nk, lambda x: x > thr)).astype(F32)
        tri = lax.broadcasted_iota(I32, (KC, KC), 0) >= lax.broadcasted_iota(I32, (KC, KC), 1)
        tri = jnp.where(tri, 1.0, 0.0).astype(BF16)
        running = jnp.zeros((1, TQ), F32)
        for c, sl in enumerate(chunks):
            x = score_ref[sl, :]
            tie = x == thr
            rank = _mm(tri, jnp.where(tie, 1.0, 0.0).astype(BF16)) + running
            running = rank[KC - 1:KC, :]
            emit(sl, ((x > thr) | (tie & (rank <= room))) & (row + c * KC <= q_pos))


def _dsa_body(nk, start, bounded, qat_ref, qib_ref, wt_ref, ka_ref, vat_ref, ki_ref, oa_ref,
              score_ref, bias_ref, s_ref, p_ref, ot_ref):
    q_pos = start + lax.broadcasted_iota(I32, (1, TQ), 1)
    row = lax.broadcasted_iota(I32, (KC, TQ), 0)
    chunks = [slice(c * KC, (c + 1) * KC) for c in range(nk // KC)]

    def emit_bias(sl, keep):
        bias_ref[sl, :] = jnp.where(keep, 0.0, NEG)

    if nk <= TOPK_MAX:
        for c, sl in enumerate(chunks):
            emit_bias(sl, row + c * KC <= q_pos)
    else:
        _select_topk(nk, q_pos, row, chunks, qib_ref, wt_ref, ki_ref, score_ref, p_ref, emit_bias)

    def q_of(hh):
        return qat_ref[hh * LANES:(hh + 1) * LANES, :]

    def k_of(hh, sl):
        return ka_ref[hh // 2, sl, :]

    def v_of(hh, sl):
        return vat_ref[hh * A_HEAD_DIM:(hh + 1) * A_HEAD_DIM, sl]

    _attend(nk, A_HEADS, A_HEAD_DIM, q_of, k_of, v_of, lambda c: bias_ref[chunks[c], :], bounded,
            s_ref, p_ref, ot_ref)
    oa_ref[...] = ot_ref[...].T


def _dsa_kernel(qat_ref, qib_ref, wt_ref, ka_ref, vat_ref, ki_ref, oa_ref,
                score_ref, bias_ref, s_ref, p_ref, ot_ref, *, seq, bounded):
    qb = pl.program_id(1)
    for cls in range(seq // TQ):
        @pl.when(qb == cls)
        def _():
            _dsa_body(TQ * (cls + 1), cls * TQ, bounded, qat_ref, qib_ref, wt_ref, ka_ref, vat_ref,
                      ki_ref, oa_ref, score_ref, bias_ref, s_ref, p_ref, ot_ref)


def _mla_body(nk, start, bounded, qbt_ref, kb_ref, vbt_ref, ob_ref, bias_ref, s_ref, p_ref, ot_ref):
    last = nk // KC - 1
    q_pos = start + lax.broadcasted_iota(I32, (1, TQ), 1)
    row = lax.broadcasted_iota(I32, (KC, TQ), 0)
    bias_ref[0:KC, :] = jnp.where(row + last * KC <= q_pos, 0.0, NEG)

    def q_of(hh):
        return qbt_ref[hh * LANES:(hh + 1) * LANES, :]

    def k_of(hh, sl):
        return kb_ref[sl, hh * LANES:(hh + 1) * LANES]

    def v_of(hh, sl):
        return vbt_ref[hh * B_VDIM:(hh + 1) * B_VDIM, sl]

    _attend(nk, B_HEADS, B_VDIM, q_of, k_of, v_of, lambda c: bias_ref[0:KC, :] if c == last else None,
            bounded, s_ref, p_ref, ot_ref)
    ob_ref[...] = ot_ref[...].T


def _mla_kernel(qbt_ref, kb_ref, vbt_ref, ob_ref, bias_ref, s_ref, p_ref, ot_ref, *, seq, bounded):
    qb = pl.program_id(1)
    for cls in range(seq // TQ):
        @pl.when(qb == cls)
        def _():
            _mla_body(TQ * (cls + 1), cls * TQ, bounded, qbt_ref, kb_ref, vbt_ref, ob_ref,
                      bias_ref, s_ref, p_ref, ot_ref)


def _mem_attn_kernel(qmt_ref, mem_ref, gains_ref, wk_ref, wvt_ref, om_ref, km_ref, vmt_ref, s_ref, p_ref, ot_ref,
                     *, mem_len, bounded):
    @pl.when(pl.program_id(1) == 0)
    def _():
        _mem_kv(mem_ref, gains_ref, wk_ref, wvt_ref, km_ref, vmt_ref)

    def q_of(hh):
        return qmt_ref[hh * M_HEAD_DIM:(hh + 1) * M_HEAD_DIM, :]

    def k_of(hh, sl):
        return km_ref[sl, hh * M_HEAD_DIM:(hh + 1) * M_HEAD_DIM]

    def v_of(hh, sl):
        return vmt_ref[hh * M_HEAD_DIM:(hh + 1) * M_HEAD_DIM, sl]

    _attend(mem_len, M_HEADS, M_HEAD_DIM, q_of, k_of, v_of, lambda c: None, bounded, s_ref, p_ref, ot_ref)
    om_ref[...] = ot_ref[...].T


def _final_kernel(x_ref, oa_ref, ob_ref, om_ref, rows_ref, wz_ref, wg_ref, wb_ref, wo_ref, out_ref):
    x = x_ref[...]
    h = _rms_lanes(x, rows_ref[ROW_GN:ROW_GN + 1, :]).astype(BF16)
    zs = [_mm(h, wz_ref[n]) for n in range(N_BRANCH)]
    gate_logits = [_mm(h, wg_ref[:, n * D_MODEL:(n + 1) * D_MODEL]) for n in range(N_BRANCH)]
    merged = jnp.zeros((TM, D_MODEL), F32)
    for n, o_ref in enumerate((oa_ref, ob_ref, om_ref)):
        y = (o_ref[...] * (zs[n] * jax.nn.sigmoid(zs[n]))).astype(BF16)
        branch = _mm(y, wb_ref[n])
        merged = merged + jax.nn.sigmoid(gate_logits[n]) * branch
    out_ref[...] = x + _mm(merged.astype(BF16), wo_ref[...])


def _full(shape):
    return pl.BlockSpec(shape, lambda *_: (0,) * len(shape), pipeline_mode=pl.Buffered(1))


def _params(n_axes):
    return pltpu.CompilerParams(dimension_semantics=("arbitrary",) * n_axes,
                                vmem_limit_bytes=VMEM_LIMIT)


def kernel(x, mem, positions, g_norm, w_in, g_qn_a, g_kn_a, g_cq, g_ckv, w_uq, w_ukv, g_qn_b, g_kn_b,
           g_mem, w_mem_kv, g_qn_m, g_kn_m, w_branch, w_out):
    b, s, d = x.shape
    m_len = mem.shape[1]
    n = b * s
    nq = s // TQ
    assert d == D_MODEL and s % TQ == 0 and TQ == KC and TM % TQ == 0 and n % TM == 0 and m_len % KC == 0
    assert g_norm.shape[0] == 1, "single-layer block"

    w = w_in[0]
    off = np.cumsum([0, 512, 512, 512, 512, IDX_DIM, IDX_HEADS, BRANCH_WIDTH, B_Q_RANK, B_KV_RANK, B_ROPE,
                     BRANCH_WIDTH, M_HEADS * M_HEAD_DIM, BRANCH_WIDTH, N_BRANCH * D_MODEL])
    seg = [w[:, off[i]:off[i + 1]] for i in range(14)]
    (w_qa, w_ka, w_va, w_qi, w_ki, w_wi, w_za, w_cq, w_ckv, w_kr, w_zb, w_qm, w_zm, w_gate) = seg
    bf = lambda a: a.astype(BF16)
    wqa_t, wqi_t, wva_t, wqm_t = bf(w_qa.T), bf(w_qi.T), bf(w_va.T), bf(w_qm.T)
    wka_t, wki_t, wkr_t = bf(w_ka.T), bf(w_ki.T), bf(w_kr.T)
    wwi_t = bf(jnp.pad(w_wi.T, ((0, 16 - IDX_HEADS), (0, 0))))
    wuq_t = bf(jnp.pad(w_uq[0].reshape(B_Q_RANK, B_HEADS, B_QK), ((0, 0), (0, 0), (0, LANES - B_QK)))
               .reshape(B_Q_RANK, B_HEADS * LANES).T)
    ukv = w_ukv[0].reshape(B_KV_RANK, B_HEADS, B_NOPE + B_VDIM)
    wuk_t = bf(ukv[:, :, :B_NOPE].reshape(B_KV_RANK, B_HEADS * B_NOPE).T)
    wuv_t = bf(ukv[:, :, B_NOPE:].reshape(B_KV_RANK, B_HEADS * B_VDIM).T)
    wmk = bf(w_mem_kv[0][:, :M_HEADS * M_HEAD_DIM])
    wmv_t = bf(w_mem_kv[0][:, M_HEADS * M_HEAD_DIM:].T)
    wz = bf(jnp.stack([w_za, w_zb, w_zm]))
    wg = bf(w_gate)
    wb = bf(w_branch[0])
    wo = bf(w_out[0])

    pad_to = lambda v, size: jnp.pad(v, (0, size - v.shape[0]))
    inv_a = ROPE_THETA ** (-(jnp.arange(0, A_ROT, 2, dtype=F32) / A_ROT))
    inv_b = ROPE_THETA ** (-(jnp.arange(0, B_ROPE, 2, dtype=F32) / B_ROPE))
    col_vectors = [None] * N_COLS
    col_vectors[COL_GQA], col_vectors[COL_GKA] = g_qn_a[0], g_kn_a[0]
    col_vectors[COL_GQB], col_vectors[COL_GKB] = g_qn_b[0], g_kn_b[0]
    col_vectors[COL_GQM], col_vectors[COL_INVA], col_vectors[COL_INVB] = g_qn_m[0], inv_a, inv_b
    cols = jnp.stack([pad_to(v, LANES) for v in col_vectors], axis=1)
    row_vectors = [None] * N_ROWS
    row_vectors[ROW_GN], row_vectors[ROW_GCQ], row_vectors[ROW_GCKV] = g_norm[0], g_cq[0], g_ckv[0]
    rows = jnp.stack([pad_to(v, D_MODEL) for v in row_vectors])
    mem_gains = jnp.stack([g_mem[0], pad_to(g_kn_m[0], D_MODEL)])

    x2 = x.reshape(n, d)
    pos_r = positions.reshape(1, n)
    tile = lambda width: pl.BlockSpec((TM, width), lambda i: (i, 0))
    tile_t = lambda rows: pl.BlockSpec((rows, TM), lambda i: (0, i))
    pos_spec = pl.BlockSpec((1, TM), lambda i: (0, i))

    a_w = [wqa_t, wqi_t, wwi_t, wka_t, wva_t, wki_t]
    b_w = [bf(w_cq), bf(w_ckv), wkr_t, wqm_t, wuq_t, wuk_t, wuv_t]
    qat, qib, wt, ka, vat, ki, qbt, kb, vbt, qmt = pl.pallas_call(
        _proj_kernel,
        grid=(n // TM,),
        in_specs=[tile(d), pos_spec, _full(rows.shape), _full(cols.shape)] + [_full(a.shape) for a in a_w + b_w],
        out_specs=[tile_t(A_HEADS * LANES), pl.BlockSpec((TM // TQ, LANES, IDX_HEADS * TQ), lambda i: (i, 0, 0)),
                   tile_t(IDX_HEADS), pl.BlockSpec((512 // LANES, TM, LANES), lambda i: (0, i, 0)),
                   tile_t(512), tile(LANES),
                   tile_t(B_HEADS * LANES), tile(B_HEADS * LANES), tile_t(512), tile_t(512)],
        out_shape=[jax.ShapeDtypeStruct((A_HEADS * LANES, n), BF16),
                   jax.ShapeDtypeStruct((n // TQ, LANES, IDX_HEADS * TQ), BF16),
                   jax.ShapeDtypeStruct((IDX_HEADS, n), F32),
                   jax.ShapeDtypeStruct((512 // LANES, n, LANES), BF16),
                   jax.ShapeDtypeStruct((512, n), BF16),
                   jax.ShapeDtypeStruct((n, LANES), BF16),
                   jax.ShapeDtypeStruct((B_HEADS * LANES, n), BF16),
                   jax.ShapeDtypeStruct((n, B_HEADS * LANES), BF16),
                   jax.ShapeDtypeStruct((512, n), BF16),
                   jax.ShapeDtypeStruct((512, n), BF16)],
        compiler_params=_params(1), name="proj",
    )(x2, pos_r, rows, cols, *a_w, *b_w)

    qcol = lambda rows: pl.BlockSpec((rows, TQ), lambda bi, qi: (0, bi * nq + qi))
    seq_rows = lambda width: pl.BlockSpec((s, width), lambda bi, qi: (bi, 0))
    seq_cols = lambda rows: pl.BlockSpec((rows, s), lambda bi, qi: (0, bi))
    o_spec = pl.BlockSpec((TQ, 512), lambda bi, qi: (bi * nq + qi, 0))
    o_shape = jax.ShapeDtypeStruct((n, 512), F32)
    attn_scratch = [pltpu.VMEM((s, TQ), F32), pltpu.VMEM((s, TQ), F32), pltpu.VMEM((2 * s, TQ), BF16),
                    pltpu.VMEM((512, TQ), F32)]

    def dsa(bounded):
        return pl.pallas_call(
            functools.partial(_dsa_kernel, seq=s, bounded=bounded),
            grid=(b, nq),
            in_specs=[qcol(A_HEADS * LANES),
                      pl.BlockSpec((1, LANES, IDX_HEADS * TQ), lambda bi, qi: (bi * nq + qi, 0, 0)),
                      qcol(IDX_HEADS), pl.BlockSpec((512 // LANES, s, LANES), lambda bi, qi: (0, bi, 0)),
                      seq_cols(512), seq_rows(LANES)],
            out_specs=o_spec, out_shape=o_shape,
            scratch_shapes=[pltpu.VMEM((s, TQ), F32)] + attn_scratch,
            compiler_params=_params(2), name="dsa" if bounded else "dsa_general")

    def mla(bounded):
        return pl.pallas_call(
            functools.partial(_mla_kernel, seq=s, bounded=bounded),
            grid=(b, nq),
            in_specs=[qcol(B_HEADS * LANES), seq_rows(B_HEADS * LANES), seq_cols(512)],
            out_specs=o_spec, out_shape=o_shape,
            scratch_shapes=attn_scratch,
            compiler_params=_params(2), name="mla" if bounded else "mla_general")

    nq_mem = s // TQ_MEM

    def mem_attn(bounded):
        return pl.pallas_call(
            functools.partial(_mem_attn_kernel, mem_len=m_len, bounded=bounded),
            grid=(b, nq_mem),
            in_specs=[pl.BlockSpec((512, TQ_MEM), lambda bi, qi: (0, bi * nq_mem + qi)),
                      pl.BlockSpec((m_len, d), lambda bi, qi: (bi, 0)),
                      _full(mem_gains.shape), _full(wmk.shape), _full(wmv_t.shape)],
            out_specs=pl.BlockSpec((TQ_MEM, 512), lambda bi, qi: (bi * nq_mem + qi, 0)), out_shape=o_shape,
            scratch_shapes=[pltpu.VMEM((m_len, 512), BF16), pltpu.VMEM((512, m_len), BF16),
                            pltpu.VMEM((m_len, TQ_MEM), F32), pltpu.VMEM((M_HEADS * m_len, TQ_MEM), BF16),
                            pltpu.VMEM((512, TQ_MEM), F32)],
            compiler_params=_params(2), name="mem_attn" if bounded else "mem_attn_general")

    gain_max = jnp.max(jnp.abs(cols), axis=0)

    def score_bound(cq_, ck_, dim):
        return dim ** 0.5 * LOG2E * 1.02 * gain_max[cq_] * gain_max[ck_]

    def attention(bounded):
        def run(*ops):
            return (dsa(bounded)(*ops[:6]), mla(bounded)(*ops[6:9]), mem_attn(bounded)(*ops[9:]))
        return run

    k_mem_gain = jnp.max(jnp.abs(g_kn_m[0]))
    worst = jnp.maximum(jnp.maximum(score_bound(COL_GQA, COL_GKA, A_HEAD_DIM), score_bound(COL_GQB, COL_GKB, B_QK)),
                        M_HEAD_DIM ** 0.5 * LOG2E * 1.02 * gain_max[COL_GQM] * k_mem_gain)
    oa, ob, om = lax.cond(worst <= BOUNDED_SCORE_LIMIT, attention(True), attention(False),
                          qat, qib, wt, ka, vat, ki, qbt, kb, vbt, qmt, mem.reshape(b * m_len, d), mem_gains,
                          wmk, wmv_t)

    out = pl.pallas_call(
        _final_kernel,
        grid=(n // TM,),
        in_specs=[tile(d), tile(512), tile(512), tile(512), _full(rows.shape), _full(wz.shape), _full(wg.shape),
                  _full(wb.shape), _full(wo.shape)],
        out_specs=tile(d), out_shape=jax.ShapeDtypeStruct((n, d), x.dtype),
        compiler_params=_params(1), name="final",
    )(x2, oa, ob, om, rows, wz, wg, wb, wo)
    return out.reshape(b, s, d)
```

```python
import functools

import numpy as np
import jax
import jax.numpy as jnp
from jax import lax
from jax.experimental import pallas as pl
from jax.experimental.pallas import tpu as pltpu

F32 = jnp.float32
BF16 = jnp.bfloat16
I32 = jnp.int32

D_MODEL = 1024
ROPE_THETA = 500000.0
EPS = 1e-6
NEG = -1e30
N_BRANCH = 3
BRANCH_WIDTH = 512
A_HEADS = 8
A_HEAD_DIM = 64
A_ROT = A_HEAD_DIM // 4
IDX_HEADS = 8
IDX_DIM = 64
TOPK_MAX = 256
B_HEADS = 8
B_NOPE = 64
B_ROPE = 32
B_VDIM = 64
B_QK = B_NOPE + B_ROPE
B_Q_RANK = 384
B_KV_RANK = 256
M_HEADS = 4
M_HEAD_DIM = 128

LANES = 128
TM = 512
TQ = 256
TQ_MEM = 512
KC = 256
COUNT_ROWS = 64
VMEM_LIMIT = 56 * 1024 * 1024
INT_MIN = -2 ** 31
LOG2E = 1.4426950408889634
BOUNDED_SCORE_LIMIT = 32.0


def _nt(a, b):
    return lax.dot_general(a, b, (((1,), (1,)), ((), ())), preferred_element_type=F32)


def _mm(a, b):
    return jnp.dot(a, b, preferred_element_type=F32)


def _rms_lanes(xf, g_row, n=None):
    n = xf.shape[-1] if n is None else n
    ms = jnp.sum(xf * xf, axis=-1, keepdims=True) / n
    return xf * lax.rsqrt(ms + EPS) * g_row


def _rms_rows(blk, g_col, n=None):
    n = blk.shape[0] if n is None else n
    ms = jnp.sum(blk * blk, axis=0, keepdims=True) / n
    return blk * lax.rsqrt(ms + EPS) * g_col


def _rope_rows(blk, lo, half, cos_t, sin_t):
    x1 = blk[lo:lo + half]
    x2 = blk[lo + half:lo + 2 * half]
    parts = []
    if lo:
        parts.append(blk[:lo])
    parts += [x1 * cos_t - x2 * sin_t, x2 * cos_t + x1 * sin_t]
    if lo + 2 * half < blk.shape[0]:
        parts.append(blk[lo + 2 * half:])
    return jnp.concatenate(parts, axis=0)


def _token_major(blocks):
    rows = sum(blk.shape[0] for blk in blocks)
    if rows < LANES:
        blocks = list(blocks) + [jnp.zeros((LANES - rows, blocks[0].shape[1]), F32)]
    return jnp.concatenate(blocks, axis=0).T.astype(BF16)


COL_GQA, COL_GKA, COL_GQB, COL_GKB, COL_GQM, COL_INVA, COL_INVB, N_COLS = 0, 1, 2, 3, 4, 5, 6, 7
ROW_GN, ROW_GCQ, ROW_GCKV, N_ROWS = 0, 1, 2, 3


def _proj_kernel(x_ref, posr_ref, rows_ref, cols_ref,
                 wqa_ref, wqi_ref, wwi_ref, wka_ref, wva_ref, wki_ref,
                 wcq_ref, wckv_ref, wkr_ref, wqm_ref, wuq_ref, wuk_ref, wuv_ref,
                 qat_ref, qib_ref, wt_ref, ka_ref, vat_ref, ki_ref, qbt_ref, kb_ref, vbt_ref, qmt_ref):
    rows, cols = rows_ref[...], cols_ref[...]
    col = lambda j, n: cols[0:n, j:j + 1]
    h = _rms_lanes(x_ref[...], rows[ROW_GN:ROW_GN + 1, :]).astype(BF16)
    pos = posr_ref[...].astype(F32)
    half_a, half_b = A_ROT // 2, B_ROPE // 2
    ang_a = col(COL_INVA, half_a) * pos
    cos_a, sin_a = jnp.cos(ang_a), jnp.sin(ang_a)
    ang_b = col(COL_INVB, half_b) * pos
    cos_b, sin_b = jnp.cos(ang_b), jnp.sin(ang_b)

    cq = _mm(h, wcq_ref[...])
    ckv = _mm(h, wckv_ref[...])
    qa = _nt(wqa_ref[...], h)
    cq = _rms_lanes(cq, rows[ROW_GCQ:ROW_GCQ + 1, 0:B_Q_RANK]).astype(BF16)
    ckv = _rms_lanes(ckv, rows[ROW_GCKV:ROW_GCKV + 1, 0:B_KV_RANK]).astype(BF16)
    qb = _nt(wuq_ref[...], cq)
    kn = _nt(wuk_ref[...], ckv)
    kr = _nt(wkr_ref[...], h)

    gq = col(COL_GQA, A_HEAD_DIM)
    for hh in range(A_HEADS):
        blk = _rms_rows(qa[hh * A_HEAD_DIM:(hh + 1) * A_HEAD_DIM], gq)
        blk = _rope_rows(blk, 0, half_a, cos_a, sin_a) * (A_HEAD_DIM ** -0.5 * LOG2E)
        own = hh * LANES + (hh % 2) * A_HEAD_DIM
        other = hh * LANES + (1 - hh % 2) * A_HEAD_DIM
        qat_ref[own:own + A_HEAD_DIM, :] = blk.astype(BF16)
        qat_ref[other:other + A_HEAD_DIM, :] = jnp.zeros((A_HEAD_DIM, TM), BF16)

    ka = _nt(wka_ref[...], h)
    qm = _nt(wqm_ref[...], h)

    gq = col(COL_GQB, LANES)
    for hh in range(B_HEADS):
        blk = _rms_rows(qb[hh * LANES:(hh + 1) * LANES], gq, n=B_QK)
        blk = _rope_rows(blk, B_NOPE, half_b, cos_b, sin_b) * (B_QK ** -0.5 * LOG2E)
        qbt_ref[hh * LANES:(hh + 1) * LANES, :] = blk.astype(BF16)
    gk = col(COL_GKB, LANES)
    pad = jnp.zeros((LANES - B_QK, TM), F32)
    for hh in range(B_HEADS):
        blk = jnp.concatenate([kn[hh * B_NOPE:(hh + 1) * B_NOPE], kr, pad], axis=0)
        blk = _rope_rows(_rms_rows(blk, gk, n=B_QK), B_NOPE, half_b, cos_b, sin_b)
        kb_ref[:, hh * LANES:(hh + 1) * LANES] = _token_major([blk])

    vbt_ref[...] = _nt(wuv_ref[...], ckv).astype(BF16)
    qi = _nt(wqi_ref[...], h)

    gk = col(COL_GKA, A_HEAD_DIM)
    for c in range(A_HEADS // 2):
        pair = [_rope_rows(_rms_rows(ka[hh * A_HEAD_DIM:(hh + 1) * A_HEAD_DIM], gk), 0, half_a, cos_a, sin_a)
                for hh in (2 * c, 2 * c + 1)]
        ka_ref[c, :, :] = _token_major(pair)
    gm = col(COL_GQM, M_HEAD_DIM)
    for hh in range(M_HEADS):
        blk = _rms_rows(qm[hh * M_HEAD_DIM:(hh + 1) * M_HEAD_DIM], gm) * (M_HEAD_DIM ** -0.5 * LOG2E)
        qmt_ref[hh * M_HEAD_DIM:(hh + 1) * M_HEAD_DIM, :] = blk.astype(BF16)

    vat_ref[...] = _nt(wva_ref[...], h).astype(BF16)
    wt_ref[...] = _nt(wwi_ref[...], h)[0:IDX_HEADS] * (IDX_HEADS ** -0.5)
    ki = _nt(wki_ref[...], h)

    for hh in range(IDX_HEADS):
        blk = _rope_rows(qi[hh * IDX_DIM:(hh + 1) * IDX_DIM], 0, half_a, cos_a, sin_a)
        blk = (blk * (IDX_DIM ** -0.5)).astype(BF16)
        for j in range(TM // TQ):
            qib_ref[j, 0:IDX_DIM, hh * TQ:(hh + 1) * TQ] = blk[:, j * TQ:(j + 1) * TQ]
    qib_ref[:, IDX_DIM:, :] = jnp.zeros((TM // TQ, LANES - IDX_DIM, IDX_HEADS * TQ), BF16)
    ki_ref[...] = _token_major([_rope_rows(ki, 0, half_a, cos_a, sin_a)])


def _mem_kv(mem_ref, gains_ref, wk_ref, wvt_ref, km_ref, vmt_ref):
    hm = _rms_lanes(mem_ref[...], gains_ref[0:1, :]).astype(BF16)
    k = _mm(hm, wk_ref[...])
    gk = gains_ref[1:2, 0:M_HEAD_DIM]
    for hh in range(M_HEADS):
        kc = _rms_lanes(k[:, hh * M_HEAD_DIM:(hh + 1) * M_HEAD_DIM], gk)
        km_ref[:, hh * M_HEAD_DIM:(hh + 1) * M_HEAD_DIM] = kc.astype(BF16)
    vmt_ref[...] = _nt(wvt_ref[...], hm).astype(BF16)


def _attend(nk, n_heads, dv, q_of, k_of, v_of, bias_of, bounded, s_ref, p_ref, ot_ref):
    nq = ot_ref.shape[1]
    chunks = [slice(c * KC, (c + 1) * KC) for c in range(nk // KC)]

    def scores(hh, q, c):
        s = _mm(k_of(hh, chunks[c]), q)
        b = bias_of(c)
        return s if b is None else s + b

    depth = max(1, min(n_heads - 1, 8 // len(chunks)))
    slots = depth + 1

    def p_rows(hh, sl):
        base = (hh % slots) * nk
        return slice(base + sl.start, base + sl.stop)

    def probabilities(hh):
        q = q_of(hh)
        if bounded:
            l8 = jnp.zeros((8, nq), F32)
            for c in range(len(chunks)):
                p = jnp.exp2(scores(hh, q, c))
                l8 = l8 + p.reshape(KC // 8, 8, nq).sum(axis=0)
                p_ref[p_rows(hh, chunks[c]), :] = p.astype(BF16)
            return jnp.sum(l8, axis=0, keepdims=True)
        m = jnp.full((1, nq), -jnp.inf, F32)
        for c in range(len(chunks)):
            s = scores(hh, q, c)
            s_ref[chunks[c], :] = s
            m = jnp.maximum(m, jnp.max(s, axis=0, keepdims=True))
        l = jnp.zeros((1, nq), F32)
        for c in range(len(chunks)):
            p = jnp.exp2(s_ref[chunks[c], :] - m)
            l = l + jnp.sum(p, axis=0, keepdims=True)
            p_ref[p_rows(hh, chunks[c]), :] = p.astype(BF16)
        return l

    def weighted_values(hh, l):
        o = _mm(v_of(hh, slice(0, nk)), p_ref[p_rows(hh, slice(0, nk)), :])
        ot_ref[hh * dv:(hh + 1) * dv, :] = o / l

    sums = {}
    for step in range(n_heads + depth):
        if step < n_heads:
            sums[step] = probabilities(step)
        if step >= depth:
            weighted_values(step - depth, sums.pop(step - depth))


def _count(score_ref, nk, pred):
    cnt = jnp.zeros((COUNT_ROWS, TQ), I32)
    for r in range(0, nk, COUNT_ROWS):
        cnt = jnp.where(pred(score_ref[r:r + COUNT_ROWS, :]), cnt + 1, cnt)
    return jnp.sum(cnt, axis=0, keepdims=True)


def _ordered_to_bits(u, magnitude_mask):
    k = u ^ INT_MIN
    return k ^ ((k >> 31) & magnitude_mask)


def _ordered_pattern_to_float(u):
    return pltpu.bitcast(_ordered_to_bits(u, 0x7FFFFFFF), F32)


def _count_rounded(round_ref, nk, cand):
    assert nk // COUNT_ROWS <= 256
    one, zero = jnp.ones((), BF16), jnp.zeros((), BF16)
    cnt = jnp.zeros((COUNT_ROWS, TQ), BF16)
    for r in range(0, nk, COUNT_ROWS):
        cnt = cnt + jnp.where(round_ref[r:r + COUNT_ROWS, :] >= cand, one, zero)
    return jnp.sum(cnt.astype(F32), axis=0, keepdims=True)


def _select_topk(nk, q_pos, row, chunks, qib_ref, wt_ref, ki_ref, score_ref, round_ref, emit):
    for c, sl in enumerate(chunks):
        ki_c = ki_ref[sl, :]
        acc = jnp.zeros((KC, TQ), F32)
        for hh in range(IDX_HEADS):
            d = _mm(ki_c, qib_ref[0, :, hh * TQ:(hh + 1) * TQ])
            acc = acc + jnp.maximum(d, 0.0) * wt_ref[hh:hh + 1, :]
        score = jnp.where(row + c * KC <= q_pos, acc, NEG)
        score_ref[sl, :] = score
        round_ref[sl, :] = score.astype(BF16)

    def coarse(i, c_u):
        cand_u = c_u | jnp.left_shift(jnp.int32(1), 31 - i)
        cand = pltpu.bitcast(_ordered_to_bits(cand_u, 0x7FFF0000), F32).astype(BF16)
        return jnp.where(_count_rounded(round_ref, nk, cand) >= TOPK_MAX, cand_u, c_u)

    c_u = lax.fori_loop(0, 16, coarse, jnp.zeros((1, TQ), I32))
    pred_bits = _ordered_to_bits(c_u - (1 << 16), 0x7FFF0000)
    base_u = (pred_bits ^ ((pred_bits >> 31) & 0x7FFFFFFF)) ^ INT_MIN

    def fine(i, carry):
        off, cnt_t = carry
        cand_off = off | jnp.left_shift(jnp.int32(1), 16 - i)
        cand = _ordered_pattern_to_float(base_u + cand_off)
        cnt = _count(score_ref, nk, lambda x: x >= cand)
        ok = cnt >= TOPK_MAX
        return jnp.where(ok, cand_off, off), jnp.where(ok, cnt, cnt_t)

    off, cnt_t = lax.fori_loop(0, 17, fine, (jnp.zeros((1, TQ), I32), jnp.full((1, TQ), nk, I32)))
    thr = _ordered_pattern_to_float(base_u + off)
    split_ties = jnp.max(jnp.where(cnt_t > TOPK_MAX, 1, 0)) > 0

    @pl.when(jnp.logical_not(split_ties))
    def _():
        for c, sl in enumerate(chunks):
            emit(sl, (score_ref[sl, :] >= thr) & (row + c * KC <= q_pos))

    @pl.when(split_ties)
    def _():
        room = (TOPK_MAX - _count(score_ref, nk, lambda x: x > thr)).astype(F32)
        tri = lax.broadcasted_iota(I32, (KC, KC), 0) >= lax.broadcasted_iota(I32, (KC, KC), 1)
        tri = jnp.where(tri, 1.0, 0.0).astype(BF16)
        running = jnp.zeros((1, TQ), F32)
        for c, sl in enumerate(chunks):
            x = score_ref[sl, :]
            tie = x == thr
            rank = _mm(tri, jnp.where(tie, 1.0, 0.0).astype(BF16)) + running
            running = rank[KC - 1:KC, :]
            emit(sl, ((x > thr) | (tie & (rank <= room))) & (row + c * KC <= q_pos))


def _dsa_body(nk, start, bounded, qat_ref, qib_ref, wt_ref, ka_ref, vat_ref, ki_ref, oa_ref,
              score_ref, bias_ref, s_ref, p_ref, ot_ref):
    q_pos = start + lax.broadcasted_iota(I32, (1, TQ), 1)
    row = lax.broadcasted_iota(I32, (KC, TQ), 0)
    chunks = [slice(c * KC, (c + 1) * KC) for c in range(nk // KC)]

    def emit_bias(sl, keep):
        bias_ref[sl, :] = jnp.where(keep, 0.0, NEG)

    if nk <= TOPK_MAX:
        for c, sl in enumerate(chunks):
            emit_bias(sl, row + c * KC <= q_pos)
    else:
        _select_topk(nk, q_pos, row, chunks, qib_ref, wt_ref, ki_ref, score_ref, p_ref, emit_bias)

    def q_of(hh):
        return qat_ref[hh * LANES:(hh + 1) * LANES, :]

    def k_of(hh, sl):
        return ka_ref[hh // 2, sl, :]

    def v_of(hh, sl):
        return vat_ref[hh * A_HEAD_DIM:(hh + 1) * A_HEAD_DIM, sl]

    _attend(nk, A_HEADS, A_HEAD_DIM, q_of, k_of, v_of, lambda c: bias_ref[chunks[c], :], bounded,
            s_ref, p_ref, ot_ref)
    oa_ref[...] = ot_ref[...].T


def _dsa_kernel(qat_ref, qib_ref, wt_ref, ka_ref, vat_ref, ki_ref, oa_ref,
                score_ref, bias_ref, s_ref, p_ref, ot_ref, *, seq, bounded):
    qb = pl.program_id(1)
    for cls in range(seq // TQ):
        @pl.when(qb == cls)
        def _():
            _dsa_body(TQ * (cls + 1), cls * TQ, bounded, qat_ref, qib_ref, wt_ref, ka_ref, vat_ref,
                      ki_ref, oa_ref, score_ref, bias_ref, s_ref, p_ref, ot_ref)


def _mla_body(nk, start, bounded, qbt_ref, kb_ref, vbt_ref, ob_ref, bias_ref, s_ref, p_ref, ot_ref):
    last = nk // KC - 1
    q_pos = start + lax.broadcasted_iota(I32, (1, TQ), 1)
    row = lax.broadcasted_iota(I32, (KC, TQ), 0)
    bias_ref[0:KC, :] = jnp.where(row + last * KC <= q_pos, 0.0, NEG)

    def q_of(hh):
        return qbt_ref[hh * LANES:(hh + 1) * LANES, :]

    def k_of(hh, sl):
        return kb_ref[sl, hh * LANES:(hh + 1) * LANES]

    def v_of(hh, sl):
        return vbt_ref[hh * B_VDIM:(hh + 1) * B_VDIM, sl]

    _attend(nk, B_HEADS, B_VDIM, q_of, k_of, v_of, lambda c: bias_ref[0:KC, :] if c == last else None,
            bounded, s_ref, p_ref, ot_ref)
    ob_ref[...] = ot_ref[...].T


def _mla_kernel(qbt_ref, kb_ref, vbt_ref, ob_ref, bias_ref, s_ref, p_ref, ot_ref, *, seq, bounded):
    for cls in range(seq // TQ):
        cols = slice(cls * TQ, (cls + 1) * TQ)
        _mla_body(TQ * (cls + 1), cls * TQ, bounded, qbt_ref.at[:, cols], kb_ref, vbt_ref, ob_ref.at[cols, :],
                  bias_ref.at[cls], s_ref, p_ref.at[cls % 2], ot_ref.at[cls])


def _mem_attn_kernel(qmt_ref, mem_ref, gains_ref, wk_ref, wvt_ref, om_ref, km_ref, vmt_ref, s_ref, p_ref, ot_ref,
                     *, mem_len, bounded):
    @pl.when(pl.program_id(1) == 0)
    def _():
        _mem_kv(mem_ref, gains_ref, wk_ref, wvt_ref, km_ref, vmt_ref)

    def q_of(hh):
        return qmt_ref[hh * M_HEAD_DIM:(hh + 1) * M_HEAD_DIM, :]

    def k_of(hh, sl):
        return km_ref[sl, hh * M_HEAD_DIM:(hh + 1) * M_HEAD_DIM]

    def v_of(hh, sl):
        return vmt_ref[hh * M_HEAD_DIM:(hh + 1) * M_HEAD_DIM, sl]

    _attend(mem_len, M_HEADS, M_HEAD_DIM, q_of, k_of, v_of, lambda c: None, bounded, s_ref, p_ref, ot_ref)
    om_ref[...] = ot_ref[...].T


def _final_kernel(x_ref, oa_ref, ob_ref, om_ref, rows_ref, wz_ref, wg_ref, wb_ref, wo_ref, out_ref):
    x = x_ref[...]
    h = _rms_lanes(x, rows_ref[ROW_GN:ROW_GN + 1, :]).astype(BF16)
    zs = [_mm(h, wz_ref[n]) for n in range(N_BRANCH)]
    gate_logits = [_mm(h, wg_ref[:, n * D_MODEL:(n + 1) * D_MODEL]) for n in range(N_BRANCH)]
    merged = jnp.zeros((TM, D_MODEL), F32)
    for n, o_ref in enumerate((oa_ref, ob_ref, om_ref)):
        y = (o_ref[...] * (zs[n] * jax.nn.sigmoid(zs[n]))).astype(BF16)
        branch = _mm(y, wb_ref[n])
        merged = merged + jax.nn.sigmoid(gate_logits[n]) * branch
    out_ref[...] = x + _mm(merged.astype(BF16), wo_ref[...])


def _full(shape):
    return pl.BlockSpec(shape, lambda *_: (0,) * len(shape), pipeline_mode=pl.Buffered(1))


def _params(n_axes):
    return pltpu.CompilerParams(dimension_semantics=("arbitrary",) * n_axes,
                                vmem_limit_bytes=VMEM_LIMIT)


def kernel(x, mem, positions, g_norm, w_in, g_qn_a, g_kn_a, g_cq, g_ckv, w_uq, w_ukv, g_qn_b, g_kn_b,
           g_mem, w_mem_kv, g_qn_m, g_kn_m, w_branch, w_out):
    b, s, d = x.shape
    m_len = mem.shape[1]
    n = b * s
    nq = s // TQ
    assert d == D_MODEL and s % TQ == 0 and TQ == KC and TM % TQ == 0 and n % TM == 0 and m_len % KC == 0
    assert g_norm.shape[0] == 1, "single-layer block"

    w = w_in[0]
    off = np.cumsum([0, 512, 512, 512, 512, IDX_DIM, IDX_HEADS, BRANCH_WIDTH, B_Q_RANK, B_KV_RANK, B_ROPE,
                     BRANCH_WIDTH, M_HEADS * M_HEAD_DIM, BRANCH_WIDTH, N_BRANCH * D_MODEL])
    seg = [w[:, off[i]:off[i + 1]] for i in range(14)]
    (w_qa, w_ka, w_va, w_qi, w_ki, w_wi, w_za, w_cq, w_ckv, w_kr, w_zb, w_qm, w_zm, w_gate) = seg
    bf = lambda a: a.astype(BF16)
    wqa_t, wqi_t, wva_t, wqm_t = bf(w_qa.T), bf(w_qi.T), bf(w_va.T), bf(w_qm.T)
    wka_t, wki_t, wkr_t = bf(w_ka.T), bf(w_ki.T), bf(w_kr.T)
    wwi_t = bf(jnp.pad(w_wi.T, ((0, 16 - IDX_HEADS), (0, 0))))
    wuq_t = bf(jnp.pad(w_uq[0].reshape(B_Q_RANK, B_HEADS, B_QK), ((0, 0), (0, 0), (0, LANES - B_QK)))
               .reshape(B_Q_RANK, B_HEADS * LANES).T)
    ukv = w_ukv[0].reshape(B_KV_RANK, B_HEADS, B_NOPE + B_VDIM)
    wuk_t = bf(ukv[:, :, :B_NOPE].reshape(B_KV_RANK, B_HEADS * B_NOPE).T)
    wuv_t = bf(ukv[:, :, B_NOPE:].reshape(B_KV_RANK, B_HEADS * B_VDIM).T)
    wmk = bf(w_mem_kv[0][:, :M_HEADS * M_HEAD_DIM])
    wmv_t = bf(w_mem_kv[0][:, M_HEADS * M_HEAD_DIM:].T)
    wz = bf(jnp.stack([w_za, w_zb, w_zm]))
    wg = bf(w_gate)
    wb = bf(w_branch[0])
    wo = bf(w_out[0])

    pad_to = lambda v, size: jnp.pad(v, (0, size - v.shape[0]))
    inv_a = ROPE_THETA ** (-(jnp.arange(0, A_ROT, 2, dtype=F32) / A_ROT))
    inv_b = ROPE_THETA ** (-(jnp.arange(0, B_ROPE, 2, dtype=F32) / B_ROPE))
    col_vectors = [None] * N_COLS
    col_vectors[COL_GQA], col_vectors[COL_GKA] = g_qn_a[0], g_kn_a[0]
    col_vectors[COL_GQB], col_vectors[COL_GKB] = g_qn_b[0], g_kn_b[0]
    col_vectors[COL_GQM], col_vectors[COL_INVA], col_vectors[COL_INVB] = g_qn_m[0], inv_a, inv_b
    cols = jnp.stack([pad_to(v, LANES) for v in col_vectors], axis=1)
    row_vectors = [None] * N_ROWS
    row_vectors[ROW_GN], row_vectors[ROW_GCQ], row_vectors[ROW_GCKV] = g_norm[0], g_cq[0], g_ckv[0]
    rows = jnp.stack([pad_to(v, D_MODEL) for v in row_vectors])
    mem_gains = jnp.stack([g_mem[0], pad_to(g_kn_m[0], D_MODEL)])

    x2 = x.reshape(n, d)
    pos_r = positions.reshape(1, n)
    tile = lambda width: pl.BlockSpec((TM, width), lambda i: (i, 0))
    tile_t = lambda rows: pl.BlockSpec((rows, TM), lambda i: (0, i))
    pos_spec = pl.BlockSpec((1, TM), lambda i: (0, i))

    a_w = [wqa_t, wqi_t, wwi_t, wka_t, wva_t, wki_t]
    b_w = [bf(w_cq), bf(w_ckv), wkr_t, wqm_t, wuq_t, wuk_t, wuv_t]
    qat, qib, wt, ka, vat, ki, qbt, kb, vbt, qmt = pl.pallas_call(
        _proj_kernel,
        grid=(n // TM,),
        in_specs=[tile(d), pos_spec, _full(rows.shape), _full(cols.shape)] + [_full(a.shape) for a in a_w + b_w],
        out_specs=[tile_t(A_HEADS * LANES), pl.BlockSpec((TM // TQ, LANES, IDX_HEADS * TQ), lambda i: (i, 0, 0)),
                   tile_t(IDX_HEADS), pl.BlockSpec((512 // LANES, TM, LANES), lambda i: (0, i, 0)),
                   tile_t(512), tile(LANES),
                   tile_t(B_HEADS * LANES), tile(B_HEADS * LANES), tile_t(512), tile_t(512)],
        out_shape=[jax.ShapeDtypeStruct((A_HEADS * LANES, n), BF16),
                   jax.ShapeDtypeStruct((n // TQ, LANES, IDX_HEADS * TQ), BF16),
                   jax.ShapeDtypeStruct((IDX_HEADS, n), F32),
                   jax.ShapeDtypeStruct((512 // LANES, n, LANES), BF16),
                   jax.ShapeDtypeStruct((512, n), BF16),
                   jax.ShapeDtypeStruct((n, LANES), BF16),
                   jax.ShapeDtypeStruct((B_HEADS * LANES, n), BF16),
                   jax.ShapeDtypeStruct((n, B_HEADS * LANES), BF16),
                   jax.ShapeDtypeStruct((512, n), BF16),
                   jax.ShapeDtypeStruct((512, n), BF16)],
        compiler_params=_params(1), name="proj",
    )(x2, pos_r, rows, cols, *a_w, *b_w)

    qcol = lambda rows: pl.BlockSpec((rows, TQ), lambda bi, qi: (0, bi * nq + qi))
    seq_rows = lambda width: pl.BlockSpec((s, width), lambda bi, qi: (bi, 0))
    seq_cols = lambda rows: pl.BlockSpec((rows, s), lambda bi, qi: (0, bi))
    o_spec = pl.BlockSpec((TQ, 512), lambda bi, qi: (bi * nq + qi, 0))
    o_shape = jax.ShapeDtypeStruct((n, 512), F32)
    attn_scratch = [pltpu.VMEM((s, TQ), F32), pltpu.VMEM((s, TQ), F32), pltpu.VMEM((2 * s, TQ), BF16),
                    pltpu.VMEM((512, TQ), F32)]

    def dsa(bounded):
        return pl.pallas_call(
            functools.partial(_dsa_kernel, seq=s, bounded=bounded),
            grid=(b, nq),
            in_specs=[qcol(A_HEADS * LANES),
                      pl.BlockSpec((1, LANES, IDX_HEADS * TQ), lambda bi, qi: (bi * nq + qi, 0, 0)),
                      qcol(IDX_HEADS), pl.BlockSpec((512 // LANES, s, LANES), lambda bi, qi: (0, bi, 0)),
                      seq_cols(512), seq_rows(LANES)],
            out_specs=o_spec, out_shape=o_shape,
            scratch_shapes=[pltpu.VMEM((s, TQ), F32)] + attn_scratch,
            compiler_params=_params(2), name="dsa" if bounded else "dsa_general")

    def mla(bounded):
        return pl.pallas_call(
            functools.partial(_mla_kernel, seq=s, bounded=bounded),
            grid=(b,),
            in_specs=[pl.BlockSpec((B_HEADS * LANES, s), lambda bi: (0, bi)),
                      pl.BlockSpec((s, B_HEADS * LANES), lambda bi: (bi, 0)),
                      pl.BlockSpec((512, s), lambda bi: (0, bi))],
            out_specs=pl.BlockSpec((s, 512), lambda bi: (bi, 0)), out_shape=o_shape,
            scratch_shapes=[pltpu.VMEM((nq, KC, TQ), F32), pltpu.VMEM((s, TQ), F32),
                            pltpu.VMEM((2, 2 * s, TQ), BF16), pltpu.VMEM((nq, 512, TQ), F32)],
            compiler_params=_params(1), name="mla" if bounded else "mla_general")

    nq_mem = s // TQ_MEM

    def mem_attn(bounded):
        return pl.pallas_call(
            functools.partial(_mem_attn_kernel, mem_len=m_len, bounded=bounded),
            grid=(b, nq_mem),
            in_specs=[pl.BlockSpec((512, TQ_MEM), lambda bi, qi: (0, bi * nq_mem + qi)),
                      pl.BlockSpec((m_len, d), lambda bi, qi: (bi, 0)),
                      _full(mem_gains.shape), _full(wmk.shape), _full(wmv_t.shape)],
            out_specs=pl.BlockSpec((TQ_MEM, 512), lambda bi, qi: (bi * nq_mem + qi, 0)), out_shape=o_shape,
            scratch_shapes=[pltpu.VMEM((m_len, 512), BF16), pltpu.VMEM((512, m_len), BF16),
                            pltpu.VMEM((m_len, TQ_MEM), F32), pltpu.VMEM((M_HEADS * m_len, TQ_MEM), BF16),
                            pltpu.VMEM((512, TQ_MEM), F32)],
            compiler_params=_params(2), name="mem_attn" if bounded else "mem_attn_general")

    gain_max = jnp.max(jnp.abs(cols), axis=0)

    def score_bound(cq_, ck_, dim):
        return dim ** 0.5 * LOG2E * 1.02 * gain_max[cq_] * gain_max[ck_]

    def attention(bounded):
        def run(*ops):
            return (dsa(bounded)(*ops[:6]), mla(bounded)(*ops[6:9]), mem_attn(bounded)(*ops[9:]))
        return run

    k_mem_gain = jnp.max(jnp.abs(g_kn_m[0]))
    worst = jnp.maximum(jnp.maximum(score_bound(COL_GQA, COL_GKA, A_HEAD_DIM), score_bound(COL_GQB, COL_GKB, B_QK)),
                        M_HEAD_DIM ** 0.5 * LOG2E * 1.02 * gain_max[COL_GQM] * k_mem_gain)
    oa, ob, om = lax.cond(worst <= BOUNDED_SCORE_LIMIT, attention(True), attention(False),
                          qat, qib, wt, ka, vat, ki, qbt, kb, vbt, qmt, mem.reshape(b * m_len, d), mem_gains,
                          wmk, wmv_t)

    out = pl.pallas_call(
        _final_kernel,
        grid=(n // TM,),
        in_specs=[tile(d), tile(512), tile(512), tile(512), _full(rows.shape), _full(wz.shape), _full(wg.shape),
                  _full(wb.shape), _full(wo.shape)],
        out_specs=tile(d), out_shape=jax.ShapeDtypeStruct((n, d), x.dtype),
        compiler_params=_params(1), name="final",
    )(x2, oa, ob, om, rows, wz, wg, wb, wo)
    return out.reshape(b, s, d)
```

```python
import functools

import numpy as np
import jax
import jax.numpy as jnp
from jax import lax
from jax.experimental import pallas as pl
from jax.experimental.pallas import tpu as pltpu

F32 = jnp.float32
BF16 = jnp.bfloat16
I32 = jnp.int32

D_MODEL = 1024
ROPE_THETA = 500000.0
EPS = 1e-6
NEG = -1e30
N_BRANCH = 3
BRANCH_WIDTH = 512
A_HEADS = 8
A_HEAD_DIM = 64
A_ROT = A_HEAD_DIM // 4
IDX_HEADS = 8
IDX_DIM = 64
TOPK_MAX = 256
B_HEADS = 8
B_NOPE = 64
B_ROPE = 32
B_VDIM = 64
B_QK = B_NOPE + B_ROPE
B_Q_RANK = 384
B_KV_RANK = 256
M_HEADS = 4
M_HEAD_DIM = 128

LANES = 128
TM = 512
TQ = 256
TQ_MEM = 512
KC = 256
COUNT_ROWS = 64
VMEM_LIMIT = 56 * 1024 * 1024
INT_MIN = -2 ** 31
LOG2E = 1.4426950408889634
BOUNDED_SCORE_LIMIT = 32.0


def _nt(a, b):
    return lax.dot_general(a, b, (((1,), (1,)), ((), ())), preferred_element_type=F32)


def _mm(a, b):
    return jnp.dot(a, b, preferred_element_type=F32)


def _rms_lanes(xf, g_row, n=None):
    n = xf.shape[-1] if n is None else n
    ms = jnp.sum(xf * xf, axis=-1, keepdims=True) / n
    return xf * lax.rsqrt(ms + EPS) * g_row


def _rms_rows(blk, g_col, n=None):
    n = blk.shape[0] if n is None else n
    ms = jnp.sum(blk * blk, axis=0, keepdims=True) / n
    return blk * lax.rsqrt(ms + EPS) * g_col


def _rope_rows(blk, lo, half, cos_t, sin_t):
    x1 = blk[lo:lo + half]
    x2 = blk[lo + half:lo + 2 * half]
    parts = []
    if lo:
        parts.append(blk[:lo])
    parts += [x1 * cos_t - x2 * sin_t, x2 * cos_t + x1 * sin_t]
    if lo + 2 * half < blk.shape[0]:
        parts.append(blk[lo + 2 * half:])
    return jnp.concatenate(parts, axis=0)


def _token_major(blocks):
    rows = sum(blk.shape[0] for blk in blocks)
    if rows < LANES:
        blocks = list(blocks) + [jnp.zeros((LANES - rows, blocks[0].shape[1]), F32)]
    return jnp.concatenate(blocks, axis=0).T.astype(BF16)


COL_GQA, COL_GKA, COL_GQB, COL_GKB, COL_GQM, COL_INVA, COL_INVB, N_COLS = 0, 1, 2, 3, 4, 5, 6, 7
ROW_GN, ROW_GCQ, ROW_GCKV, N_ROWS = 0, 1, 2, 3


def _proj_kernel(x_ref, posr_ref, rows_ref, cols_ref,
                 wqa_ref, wqi_ref, wwi_ref, wka_ref, wva_ref, wki_ref,
                 wcq_ref, wckv_ref, wkr_ref, wqm_ref, wuq_ref, wuk_ref, wuv_ref,
                 qat_ref, qib_ref, wt_ref, ka_ref, vat_ref, ki_ref, qbt_ref, kb_ref, vbt_ref, qmt_ref):
    rows, cols = rows_ref[...], cols_ref[...]
    col = lambda j, n: cols[0:n, j:j + 1]
    h = _rms_lanes(x_ref[...], rows[ROW_GN:ROW_GN + 1, :]).astype(BF16)
    pos = posr_ref[...].astype(F32)
    half_a, half_b = A_ROT // 2, B_ROPE // 2
    ang_a = col(COL_INVA, half_a) * pos
    cos_a, sin_a = jnp.cos(ang_a), jnp.sin(ang_a)
    ang_b = col(COL_INVB, half_b) * pos
    cos_b, sin_b = jnp.cos(ang_b), jnp.sin(ang_b)

    cq = _mm(h, wcq_ref[...])
    ckv = _mm(h, wckv_ref[...])
    qa = _nt(wqa_ref[...], h)
    cq = _rms_lanes(cq, rows[ROW_GCQ:ROW_GCQ + 1, 0:B_Q_RANK]).astype(BF16)
    ckv = _rms_lanes(ckv, rows[ROW_GCKV:ROW_GCKV + 1, 0:B_KV_RANK]).astype(BF16)
    qb = _nt(wuq_ref[...], cq)
    kn = _nt(wuk_ref[...], ckv)
    kr = _nt(wkr_ref[...], h)

    gq = col(COL_GQA, A_HEAD_DIM)
    for hh in range(A_HEADS):
        blk = _rms_rows(qa[hh * A_HEAD_DIM:(hh + 1) * A_HEAD_DIM], gq)
        blk = _rope_rows(blk, 0, half_a, cos_a, sin_a) * (A_HEAD_DIM ** -0.5 * LOG2E)
        own = hh * LANES + (hh % 2) * A_HEAD_DIM
        other = hh * LANES + (1 - hh % 2) * A_HEAD_DIM
        qat_ref[own:own + A_HEAD_DIM, :] = blk.astype(BF16)
        qat_ref[other:other + A_HEAD_DIM, :] = jnp.zeros((A_HEAD_DIM, TM), BF16)

    ka = _nt(wka_ref[...], h)
    qm = _nt(wqm_ref[...], h)

    gq = col(COL_GQB, LANES)
    for hh in range(B_HEADS):
        blk = _rms_rows(qb[hh * LANES:(hh + 1) * LANES], gq, n=B_QK)
        blk = _rope_rows(blk, B_NOPE, half_b, cos_b, sin_b) * (B_QK ** -0.5 * LOG2E)
        qbt_ref[hh * LANES:(hh + 1) * LANES, :] = blk.astype(BF16)
    gk = col(COL_GKB, LANES)
    pad = jnp.zeros((LANES - B_QK, TM), F32)
    for hh in range(B_HEADS):
        blk = jnp.concatenate([kn[hh * B_NOPE:(hh + 1) * B_NOPE], kr, pad], axis=0)
        blk = _rope_rows(_rms_rows(blk, gk, n=B_QK), B_NOPE, half_b, cos_b, sin_b)
        kb_ref[:, hh * LANES:(hh + 1) * LANES] = _token_major([blk])

    vbt_ref[...] = _nt(wuv_ref[...], ckv).astype(BF16)
    qi = _nt(wqi_ref[...], h)

    gk = col(COL_GKA, A_HEAD_DIM)
    for c in range(A_HEADS // 2):
        pair = [_rope_rows(_rms_rows(ka[hh * A_HEAD_DIM:(hh + 1) * A_HEAD_DIM], gk), 0, half_a, cos_a, sin_a)
                for hh in (2 * c, 2 * c + 1)]
        ka_ref[c, :, :] = _token_major(pair)
    gm = col(COL_GQM, M_HEAD_DIM)
    for hh in range(M_HEADS):
        blk = _rms_rows(qm[hh * M_HEAD_DIM:(hh + 1) * M_HEAD_DIM], gm) * (M_HEAD_DIM ** -0.5 * LOG2E)
        qmt_ref[hh * M_HEAD_DIM:(hh + 1) * M_HEAD_DIM, :] = blk.astype(BF16)

    vat_ref[...] = _nt(wva_ref[...], h).astype(BF16)
    wt_ref[...] = _nt(wwi_ref[...], h)[0:IDX_HEADS] * (IDX_HEADS ** -0.5)
    ki = _nt(wki_ref[...], h)

    for hh in range(IDX_HEADS):
        blk = _rope_rows(qi[hh * IDX_DIM:(hh + 1) * IDX_DIM], 0, half_a, cos_a, sin_a)
        blk = (blk * (IDX_DIM ** -0.5)).astype(BF16)
        for j in range(TM // TQ):
            qib_ref[j, 0:IDX_DIM, hh * TQ:(hh + 1) * TQ] = blk[:, j * TQ:(j + 1) * TQ]
    qib_ref[:, IDX_DIM:, :] = jnp.zeros((TM // TQ, LANES - IDX_DIM, IDX_HEADS * TQ), BF16)
    ki_ref[...] = _token_major([_rope_rows(ki, 0, half_a, cos_a, sin_a)])


def _mem_kv(mem_ref, gains_ref, wk_ref, wvt_ref, km_ref, vmt_ref):
    hm = _rms_lanes(mem_ref[...], gains_ref[0:1, :]).astype(BF16)
    k = _mm(hm, wk_ref[...])
    gk = gains_ref[1:2, 0:M_HEAD_DIM]
    for hh in range(M_HEADS):
        kc = _rms_lanes(k[:, hh * M_HEAD_DIM:(hh + 1) * M_HEAD_DIM], gk)
        km_ref[:, hh * M_HEAD_DIM:(hh + 1) * M_HEAD_DIM] = kc.astype(BF16)
    vmt_ref[...] = _nt(wvt_ref[...], hm).astype(BF16)


def _attend(nk, n_heads, dv, q_of, k_of, v_of, bias_of, bounded, s_ref, p_ref, ot_ref):
    nq = ot_ref.shape[1]
    chunks = [slice(c * KC, (c + 1) * KC) for c in range(nk // KC)]

    def scores(hh, q, c):
        s = _mm(k_of(hh, chunks[c]), q)
        b = bias_of(c)
        return s if b is None else s + b

    depth = max(1, min(n_heads - 1, 8 // len(chunks)))
    slots = depth + 1

    def p_rows(hh, sl):
        base = (hh % slots) * nk
        return slice(base + sl.start, base + sl.stop)

    def probabilities(hh):
        q = q_of(hh)
        if bounded:
            l8 = jnp.zeros((8, nq), F32)
            for c in range(len(chunks)):
                p = jnp.exp2(scores(hh, q, c))
                l8 = l8 + p.reshape(KC // 8, 8, nq).sum(axis=0)
                p_ref[p_rows(hh, chunks[c]), :] = p.astype(BF16)
            return jnp.sum(l8, axis=0, keepdims=True)
        m = jnp.full((1, nq), -jnp.inf, F32)
        for c in range(len(chunks)):
            s = scores(hh, q, c)
            s_ref[chunks[c], :] = s
            m = jnp.maximum(m, jnp.max(s, axis=0, keepdims=True))
        l = jnp.zeros((1, nq), F32)
        for c in range(len(chunks)):
            p = jnp.exp2(s_ref[chunks[c], :] - m)
            l = l + jnp.sum(p, axis=0, keepdims=True)
            p_ref[p_rows(hh, chunks[c]), :] = p.astype(BF16)
        return l

    def weighted_values(hh, l):
        o = _mm(v_of(hh, slice(0, nk)), p_ref[p_rows(hh, slice(0, nk)), :])
        ot_ref[hh * dv:(hh + 1) * dv, :] = o / l

    sums = {}
    for step in range(n_heads + depth):
        if step < n_heads:
            sums[step] = probabilities(step)
        if step >= depth:
            weighted_values(step - depth, sums.pop(step - depth))


def _count(score_ref, nk, pred):
    cnt = jnp.zeros((COUNT_ROWS, TQ), I32)
    for r in range(0, nk, COUNT_ROWS):
        cnt = jnp.where(pred(score_ref[r:r + COUNT_ROWS, :]), cnt + 1, cnt)
    return jnp.sum(cnt, axis=0, keepdims=True)


def _ordered_to_bits(u, magnitude_mask):
    k = u ^ INT_MIN
    return k ^ ((k >> 31) & magnitude_mask)


def _ordered_pattern_to_float(u):
    return pltpu.bitcast(_ordered_to_bits(u, 0x7FFFFFFF), F32)


def _count_rounded(round_ref, nk, cand):
    assert nk // COUNT_ROWS <= 256
    one, zero = jnp.ones((), BF16), jnp.zeros((), BF16)
    cnt = jnp.zeros((COUNT_ROWS, TQ), BF16)
    for r in range(0, nk, COUNT_ROWS):
        cnt = cnt + jnp.where(round_ref[r:r + COUNT_ROWS, :] >= cand, one, zero)
    return jnp.sum(cnt.astype(F32), axis=0, keepdims=True)


def _select_topk(nk, q_pos, row, chunks, qib_ref, wt_ref, ki_ref, score_ref, round_ref, emit):
    for c, sl in enumerate(chunks):
        ki_c = ki_ref[sl, :]
        acc = jnp.zeros((KC, TQ), F32)
        for hh in range(IDX_HEADS):
            d = _mm(ki_c, qib_ref[0, :, hh * TQ:(hh + 1) * TQ])
            acc = acc + jnp.maximum(d, 0.0) * wt_ref[hh:hh + 1, :]
        score = jnp.where(row + c * KC <= q_pos, acc, NEG)
        score_ref[sl, :] = score
        round_ref[sl, :] = score.astype(BF16)

    def coarse(i, c_u):
        cand_u = c_u | jnp.left_shift(jnp.int32(1), 31 - i)
        cand = pltpu.bitcast(_ordered_to_bits(cand_u, 0x7FFF0000), F32).astype(BF16)
        return jnp.where(_count_rounded(round_ref, nk, cand) >= TOPK_MAX, cand_u, c_u)

    c_u = lax.fori_loop(0, 16, coarse, jnp.zeros((1, TQ), I32))
    pred_bits = _ordered_to_bits(c_u - (1 << 16), 0x7FFF0000)
    base_u = (pred_bits ^ ((pred_bits >> 31) & 0x7FFFFFFF)) ^ INT_MIN

    def fine(i, carry):
        off, cnt_t = carry
        cand_off = off | jnp.left_shift(jnp.int32(1), 16 - i)
        cand = _ordered_pattern_to_float(base_u + cand_off)
        cnt = _count(score_ref, nk, lambda x: x >= cand)
        ok = cnt >= TOPK_MAX
        return jnp.where(ok, cand_off, off), jnp.where(ok, cnt, cnt_t)

    off, cnt_t = lax.fori_loop(0, 17, fine, (jnp.zeros((1, TQ), I32), jnp.full((1, TQ), nk, I32)))
    thr = _ordered_pattern_to_float(base_u + off)
    split_ties = jnp.max(jnp.where(cnt_t > TOPK_MAX, 1, 0)) > 0

    @pl.when(jnp.logical_not(split_ties))
    def _():
        for c, sl in enumerate(chunks):
            emit(sl, (score_ref[sl, :] >= thr) & (row + c * KC <= q_pos))

    @pl.when(split_ties)
    def _():
        room = (TOPK_MAX - _count(score_ref, nk, lambda x: x > thr)).astype(F32)
        tri = lax.broadcasted_iota(I32, (KC, KC), 0) >= lax.broadcasted_iota(I32, (KC, KC), 1)
        tri = jnp.where(tri, 1.0, 0.0).astype(BF16)
        running = jnp.zeros((1, TQ), F32)
        for c, sl in enumerate(chunks):
            x = score_ref[sl, :]
            tie = x == thr
            rank = _mm(tri, jnp.where(tie, 1.0, 0.0).astype(BF16)) + running
            running = rank[KC - 1:KC, :]
            emit(sl, ((x > thr) | (tie & (rank <= room))) & (row + c * KC <= q_pos))


def _dsa_body(nk, start, bounded, qat_ref, qib_ref, wt_ref, ka_ref, vat_ref, ki_ref, oa_ref,
              score_ref, bias_ref, s_ref, p_ref, ot_ref):
    q_pos = start + lax.broadcasted_iota(I32, (1, TQ), 1)
    row = lax.broadcasted_iota(I32, (KC, TQ), 0)
    chunks = [slice(c * KC, (c + 1) * KC) for c in range(nk // KC)]

    def emit_bias(sl, keep):
        bias_ref[sl, :] = jnp.where(keep, 0.0, NEG)

    if nk <= TOPK_MAX:
        for c, sl in enumerate(chunks):
            emit_bias(sl, row + c * KC <= q_pos)
    else:
        _select_topk(nk, q_pos, row, chunks, qib_ref, wt_ref, ki_ref, score_ref, p_ref, emit_bias)

    def q_of(hh):
        return qat_ref[hh * LANES:(hh + 1) * LANES, :]

    def k_of(hh, sl):
        return ka_ref[hh // 2, sl, :]

    def v_of(hh, sl):
        return vat_ref[hh * A_HEAD_DIM:(hh + 1) * A_HEAD_DIM, sl]

    _attend(nk, A_HEADS, A_HEAD_DIM, q_of, k_of, v_of, lambda c: bias_ref[chunks[c], :], bounded,
            s_ref, p_ref, ot_ref)
    oa_ref[...] = ot_ref[...].T


def _dsa_kernel(qat_ref, qib_ref, wt_ref, ka_ref, vat_ref, ki_ref, oa_ref,
                score_ref, bias_ref, s_ref, p_ref, ot_ref, *, seq, bounded):
    for cls in range(seq // TQ):
        cols = slice(cls * TQ, (cls + 1) * TQ)
        _dsa_body(TQ * (cls + 1), cls * TQ, bounded, qat_ref.at[:, cols], qib_ref.at[pl.ds(cls, 1)],
                  wt_ref.at[:, cols], ka_ref, vat_ref, ki_ref, oa_ref.at[cols, :],
                  score_ref, bias_ref, s_ref, p_ref, ot_ref)


def _mla_body(nk, start, bounded, qbt_ref, kb_ref, vbt_ref, ob_ref, bias_ref, s_ref, p_ref, ot_ref):
    last = nk // KC - 1
    q_pos = start + lax.broadcasted_iota(I32, (1, TQ), 1)
    row = lax.broadcasted_iota(I32, (KC, TQ), 0)
    bias_ref[0:KC, :] = jnp.where(row + last * KC <= q_pos, 0.0, NEG)

    def q_of(hh):
        return qbt_ref[hh * LANES:(hh + 1) * LANES, :]

    def k_of(hh, sl):
        return kb_ref[sl, hh * LANES:(hh + 1) * LANES]

    def v_of(hh, sl):
        return vbt_ref[hh * B_VDIM:(hh + 1) * B_VDIM, sl]

    _attend(nk, B_HEADS, B_VDIM, q_of, k_of, v_of, lambda c: bias_ref[0:KC, :] if c == last else None,
            bounded, s_ref, p_ref, ot_ref)
    ob_ref[...] = ot_ref[...].T


def _mla_kernel(qbt_ref, kb_ref, vbt_ref, ob_ref, bias_ref, s_ref, p_ref, ot_ref, *, seq, bounded):
    for cls in range(seq // TQ):
        cols = slice(cls * TQ, (cls + 1) * TQ)
        _mla_body(TQ * (cls + 1), cls * TQ, bounded, qbt_ref.at[:, cols], kb_ref, vbt_ref, ob_ref.at[cols, :],
                  bias_ref.at[cls], s_ref, p_ref.at[cls % 2], ot_ref.at[cls])


def _mem_attn_kernel(qmt_ref, mem_ref, gains_ref, wk_ref, wvt_ref, om_ref, km_ref, vmt_ref, s_ref, p_ref, ot_ref,
                     *, mem_len, bounded):
    @pl.when(pl.program_id(1) == 0)
    def _():
        _mem_kv(mem_ref, gains_ref, wk_ref, wvt_ref, km_ref, vmt_ref)

    def q_of(hh):
        return qmt_ref[hh * M_HEAD_DIM:(hh + 1) * M_HEAD_DIM, :]

    def k_of(hh, sl):
        return km_ref[sl, hh * M_HEAD_DIM:(hh + 1) * M_HEAD_DIM]

    def v_of(hh, sl):
        return vmt_ref[hh * M_HEAD_DIM:(hh + 1) * M_HEAD_DIM, sl]

    _attend(mem_len, M_HEADS, M_HEAD_DIM, q_of, k_of, v_of, lambda c: None, bounded, s_ref, p_ref, ot_ref)
    om_ref[...] = ot_ref[...].T


def _final_kernel(x_ref, oa_ref, ob_ref, om_ref, rows_ref, wz_ref, wg_ref, wb_ref, wo_ref, out_ref):
    x = x_ref[...]
    h = _rms_lanes(x, rows_ref[ROW_GN:ROW_GN + 1, :]).astype(BF16)
    zs = [_mm(h, wz_ref[n]) for n in range(N_BRANCH)]
    gate_logits = [_mm(h, wg_ref[:, n * D_MODEL:(n + 1) * D_MODEL]) for n in range(N_BRANCH)]
    merged = jnp.zeros((TM, D_MODEL), F32)
    for n, o_ref in enumerate((oa_ref, ob_ref, om_ref)):
        y = (o_ref[...] * (zs[n] * jax.nn.sigmoid(zs[n]))).astype(BF16)
        branch = _mm(y, wb_ref[n])
        merged = merged + jax.nn.sigmoid(gate_logits[n]) * branch
    out_ref[...] = x + _mm(merged.astype(BF16), wo_ref[...])


def _full(shape):
    return pl.BlockSpec(shape, lambda *_: (0,) * len(shape), pipeline_mode=pl.Buffered(1))


def _params(n_axes):
    return pltpu.CompilerParams(dimension_semantics=("arbitrary",) * n_axes,
                                vmem_limit_bytes=VMEM_LIMIT)


def kernel(x, mem, positions, g_norm, w_in, g_qn_a, g_kn_a, g_cq, g_ckv, w_uq, w_ukv, g_qn_b, g_kn_b,
           g_mem, w_mem_kv, g_qn_m, g_kn_m, w_branch, w_out):
    b, s, d = x.shape
    m_len = mem.shape[1]
    n = b * s
    nq = s // TQ
    assert d == D_MODEL and s % TQ == 0 and TQ == KC and TM % TQ == 0 and n % TM == 0 and m_len % KC == 0
    assert g_norm.shape[0] == 1, "single-layer block"

    w = w_in[0]
    off = np.cumsum([0, 512, 512, 512, 512, IDX_DIM, IDX_HEADS, BRANCH_WIDTH, B_Q_RANK, B_KV_RANK, B_ROPE,
                     BRANCH_WIDTH, M_HEADS * M_HEAD_DIM, BRANCH_WIDTH, N_BRANCH * D_MODEL])
    seg = [w[:, off[i]:off[i + 1]] for i in range(14)]
    (w_qa, w_ka, w_va, w_qi, w_ki, w_wi, w_za, w_cq, w_ckv, w_kr, w_zb, w_qm, w_zm, w_gate) = seg
    bf = lambda a: a.astype(BF16)
    wqa_t, wqi_t, wva_t, wqm_t = bf(w_qa.T), bf(w_qi.T), bf(w_va.T), bf(w_qm.T)
    wka_t, wki_t, wkr_t = bf(w_ka.T), bf(w_ki.T), bf(w_kr.T)
    wwi_t = bf(jnp.pad(w_wi.T, ((0, 16 - IDX_HEADS), (0, 0))))
    wuq_t = bf(jnp.pad(w_uq[0].reshape(B_Q_RANK, B_HEADS, B_QK), ((0, 0), (0, 0), (0, LANES - B_QK)))
               .reshape(B_Q_RANK, B_HEADS * LANES).T)
    ukv = w_ukv[0].reshape(B_KV_RANK, B_HEADS, B_NOPE + B_VDIM)
    wuk_t = bf(ukv[:, :, :B_NOPE].reshape(B_KV_RANK, B_HEADS * B_NOPE).T)
    wuv_t = bf(ukv[:, :, B_NOPE:].reshape(B_KV_RANK, B_HEADS * B_VDIM).T)
    wmk = bf(w_mem_kv[0][:, :M_HEADS * M_HEAD_DIM])
    wmv_t = bf(w_mem_kv[0][:, M_HEADS * M_HEAD_DIM:].T)
    wz = bf(jnp.stack([w_za, w_zb, w_zm]))
    wg = bf(w_gate)
    wb = bf(w_branch[0])
    wo = bf(w_out[0])

    pad_to = lambda v, size: jnp.pad(v, (0, size - v.shape[0]))
    inv_a = ROPE_THETA ** (-(jnp.arange(0, A_ROT, 2, dtype=F32) / A_ROT))
    inv_b = ROPE_THETA ** (-(jnp.arange(0, B_ROPE, 2, dtype=F32) / B_ROPE))
    col_vectors = [None] * N_COLS
    col_vectors[COL_GQA], col_vectors[COL_GKA] = g_qn_a[0], g_kn_a[0]
    col_vectors[COL_GQB], col_vectors[COL_GKB] = g_qn_b[0], g_kn_b[0]
    col_vectors[COL_GQM], col_vectors[COL_INVA], col_vectors[COL_INVB] = g_qn_m[0], inv_a, inv_b
    cols = jnp.stack([pad_to(v, LANES) for v in col_vectors], axis=1)
    row_vectors = [None] * N_ROWS
    row_vectors[ROW_GN], row_vectors[ROW_GCQ], row_vectors[ROW_GCKV] = g_norm[0], g_cq[0], g_ckv[0]
    rows = jnp.stack([pad_to(v, D_MODEL) for v in row_vectors])
    mem_gains = jnp.stack([g_mem[0], pad_to(g_kn_m[0], D_MODEL)])

    x2 = x.reshape(n, d)
    pos_r = positions.reshape(1, n)
    tile = lambda width: pl.BlockSpec((TM, width), lambda i: (i, 0))
    tile_t = lambda rows: pl.BlockSpec((rows, TM), lambda i: (0, i))
    pos_spec = pl.BlockSpec((1, TM), lambda i: (0, i))

    a_w = [wqa_t, wqi_t, wwi_t, wka_t, wva_t, wki_t]
    b_w = [bf(w_cq), bf(w_ckv), wkr_t, wqm_t, wuq_t, wuk_t, wuv_t]
    qat, qib, wt, ka, vat, ki, qbt, kb, vbt, qmt = pl.pallas_call(
        _proj_kernel,
        grid=(n // TM,),
        in_specs=[tile(d), pos_spec, _full(rows.shape), _full(cols.shape)] + [_full(a.shape) for a in a_w + b_w],
        out_specs=[tile_t(A_HEADS * LANES), pl.BlockSpec((TM // TQ, LANES, IDX_HEADS * TQ), lambda i: (i, 0, 0)),
                   tile_t(IDX_HEADS), pl.BlockSpec((512 // LANES, TM, LANES), lambda i: (0, i, 0)),
                   tile_t(512), tile(LANES),
                   tile_t(B_HEADS * LANES), tile(B_HEADS * LANES), tile_t(512), tile_t(512)],
        out_shape=[jax.ShapeDtypeStruct((A_HEADS * LANES, n), BF16),
                   jax.ShapeDtypeStruct((n // TQ, LANES, IDX_HEADS * TQ), BF16),
                   jax.ShapeDtypeStruct((IDX_HEADS, n), F32),
                   jax.ShapeDtypeStruct((512 // LANES, n, LANES), BF16),
                   jax.ShapeDtypeStruct((512, n), BF16),
                   jax.ShapeDtypeStruct((n, LANES), BF16),
                   jax.ShapeDtypeStruct((B_HEADS * LANES, n), BF16),
                   jax.ShapeDtypeStruct((n, B_HEADS * LANES), BF16),
                   jax.ShapeDtypeStruct((512, n), BF16),
                   jax.ShapeDtypeStruct((512, n), BF16)],
        compiler_params=_params(1), name="proj",
    )(x2, pos_r, rows, cols, *a_w, *b_w)

    o_shape = jax.ShapeDtypeStruct((n, 512), F32)

    def dsa(bounded):
        return pl.pallas_call(
            functools.partial(_dsa_kernel, seq=s, bounded=bounded),
            grid=(b,),
            in_specs=[pl.BlockSpec((A_HEADS * LANES, s), lambda bi: (0, bi)),
                      pl.BlockSpec((nq, LANES, IDX_HEADS * TQ), lambda bi: (bi, 0, 0)),
                      pl.BlockSpec((IDX_HEADS, s), lambda bi: (0, bi)),
                      pl.BlockSpec((512 // LANES, s, LANES), lambda bi: (0, bi, 0)),
                      pl.BlockSpec((512, s), lambda bi: (0, bi)), pl.BlockSpec((s, LANES), lambda bi: (bi, 0))],
            out_specs=pl.BlockSpec((s, 512), lambda bi: (bi, 0)), out_shape=o_shape,
            scratch_shapes=[pltpu.VMEM((s, TQ), F32), pltpu.VMEM((s, TQ), F32), pltpu.VMEM((s, TQ), F32),
                            pltpu.VMEM((2 * s, TQ), BF16), pltpu.VMEM((512, TQ), F32)],
            compiler_params=_params(1), name="dsa" if bounded else "dsa_general")

    def mla(bounded):
        return pl.pallas_call(
            functools.partial(_mla_kernel, seq=s, bounded=bounded),
            grid=(b,),
            in_specs=[pl.BlockSpec((B_HEADS * LANES, s), lambda bi: (0, bi)),
                      pl.BlockSpec((s, B_HEADS * LANES), lambda bi: (bi, 0)),
                      pl.BlockSpec((512, s), lambda bi: (0, bi))],
            out_specs=pl.BlockSpec((s, 512), lambda bi: (bi, 0)), out_shape=o_shape,
            scratch_shapes=[pltpu.VMEM((nq, KC, TQ), F32), pltpu.VMEM((s, TQ), F32),
                            pltpu.VMEM((2, 2 * s, TQ), BF16), pltpu.VMEM((nq, 512, TQ), F32)],
            compiler_params=_params(1), name="mla" if bounded else "mla_general")

    nq_mem = s // TQ_MEM

    def mem_attn(bounded):
        return pl.pallas_call(
            functools.partial(_mem_attn_kernel, mem_len=m_len, bounded=bounded),
            grid=(b, nq_mem),
            in_specs=[pl.BlockSpec((512, TQ_MEM), lambda bi, qi: (0, bi * nq_mem + qi)),
                      pl.BlockSpec((m_len, d), lambda bi, qi: (bi, 0)),
                      _full(mem_gains.shape), _full(wmk.shape), _full(wmv_t.shape)],
            out_specs=pl.BlockSpec((TQ_MEM, 512), lambda bi, qi: (bi * nq_mem + qi, 0)), out_shape=o_shape,
            scratch_shapes=[pltpu.VMEM((m_len, 512), BF16), pltpu.VMEM((512, m_len), BF16),
                            pltpu.VMEM((m_len, TQ_MEM), F32), pltpu.VMEM((M_HEADS * m_len, TQ_MEM), BF16),
                            pltpu.VMEM((512, TQ_MEM), F32)],
            compiler_params=_params(2), name="mem_attn" if bounded else "mem_attn_general")

    gain_max = jnp.max(jnp.abs(cols), axis=0)

    def score_bound(cq_, ck_, dim):
        return dim ** 0.5 * LOG2E * 1.02 * gain_max[cq_] * gain_max[ck_]

    def attention(bounded):
        def run(*ops):
            return (dsa(bounded)(*ops[:6]), mla(bounded)(*ops[6:9]), mem_attn(bounded)(*ops[9:]))
        return run

    k_mem_gain = jnp.max(jnp.abs(g_kn_m[0]))
    worst = jnp.maximum(jnp.maximum(score_bound(COL_GQA, COL_GKA, A_HEAD_DIM), score_bound(COL_GQB, COL_GKB, B_QK)),
                        M_HEAD_DIM ** 0.5 * LOG2E * 1.02 * gain_max[COL_GQM] * k_mem_gain)
    oa, ob, om = lax.cond(worst <= BOUNDED_SCORE_LIMIT, attention(True), attention(False),
                          qat, qib, wt, ka, vat, ki, qbt, kb, vbt, qmt, mem.reshape(b * m_len, d), mem_gains,
                          wmk, wmv_t)

    out = pl.pallas_call(
        _final_kernel,
        grid=(n // TM,),
        in_specs=[tile(d), tile(512), tile(512), tile(512), _full(rows.shape), _full(wz.shape), _full(wg.shape),
                  _full(wb.shape), _full(wo.shape)],
        out_specs=tile(d), out_shape=jax.ShapeDtypeStruct((n, d), x.dtype),
        compiler_params=_params(1), name="final",
    )(x2, oa, ob, om, rows, wz, wg, wb, wo)
    return out.reshape(b, s, d)
```

```python
import functools

import numpy as np
import jax
import jax.numpy as jnp
from jax import lax
from jax.experimental import pallas as pl
from jax.experimental.pallas import tpu as pltpu

F32 = jnp.float32
BF16 = jnp.bfloat16
I32 = jnp.int32

D_MODEL = 1024
ROPE_THETA = 500000.0
EPS = 1e-6
NEG = -1e30
N_BRANCH = 3
BRANCH_WIDTH = 512
A_HEADS = 8
A_HEAD_DIM = 64
A_ROT = A_HEAD_DIM // 4
IDX_HEADS = 8
IDX_DIM = 64
TOPK_MAX = 256
B_HEADS = 8
B_NOPE = 64
B_ROPE = 32
B_VDIM = 64
B_QK = B_NOPE + B_ROPE
B_Q_RANK = 384
B_KV_RANK = 256
M_HEADS = 4
M_HEAD_DIM = 128

LANES = 128
TM = 512
TQ = 256
TQ_MEM = 512
KC = 256
COUNT_ROWS = 64
VMEM_LIMIT = 56 * 1024 * 1024
INT_MIN = -2 ** 31
LOG2E = 1.4426950408889634
BOUNDED_SCORE_LIMIT = 32.0


def _nt(a, b):
    return lax.dot_general(a, b, (((1,), (1,)), ((), ())), preferred_element_type=F32)


def _mm(a, b):
    return jnp.dot(a, b, preferred_element_type=F32)


def _rms_lanes(xf, g_row, n=None):
    n = xf.shape[-1] if n is None else n
    ms = jnp.sum(xf * xf, axis=-1, keepdims=True) / n
    return xf * lax.rsqrt(ms + EPS) * g_row


def _rms_rows(blk, g_col, n=None):
    n = blk.shape[0] if n is None else n
    ms = jnp.sum(blk * blk, axis=0, keepdims=True) / n
    return blk * lax.rsqrt(ms + EPS) * g_col


def _rope_rows(blk, lo, half, cos_t, sin_t):
    x1 = blk[lo:lo + half]
    x2 = blk[lo + half:lo + 2 * half]
    parts = []
    if lo:
        parts.append(blk[:lo])
    parts += [x1 * cos_t - x2 * sin_t, x2 * cos_t + x1 * sin_t]
    if lo + 2 * half < blk.shape[0]:
        parts.append(blk[lo + 2 * half:])
    return jnp.concatenate(parts, axis=0)


def _token_major(blocks):
    rows = sum(blk.shape[0] for blk in blocks)
    if rows < LANES:
        blocks = list(blocks) + [jnp.zeros((LANES - rows, blocks[0].shape[1]), F32)]
    return jnp.concatenate(blocks, axis=0).T.astype(BF16)


COL_GQA, COL_GKA, COL_GQB, COL_GKB, COL_GQM, COL_INVA, COL_INVB, N_COLS = 0, 1, 2, 3, 4, 5, 6, 7
ROW_GN, ROW_GCQ, ROW_GCKV, N_ROWS = 0, 1, 2, 3


def _proj_kernel(x_ref, posr_ref, rows_ref, cols_ref,
                 wqa_ref, wqi_ref, wwi_ref, wka_ref, wva_ref, wki_ref,
                 wcq_ref, wckv_ref, wkr_ref, wqm_ref, wuq_ref, wuk_ref, wuv_ref,
                 qat_ref, qib_ref, wt_ref, ka_ref, vat_ref, ki_ref, qbt_ref, kb_ref, vbt_ref, qmt_ref):
    rows, cols = rows_ref[...], cols_ref[...]
    col = lambda j, n: cols[0:n, j:j + 1]
    h = _rms_lanes(x_ref[...], rows[ROW_GN:ROW_GN + 1, :]).astype(BF16)
    pos = posr_ref[...].astype(F32)
    half_a, half_b = A_ROT // 2, B_ROPE // 2
    ang_a = col(COL_INVA, half_a) * pos
    cos_a, sin_a = jnp.cos(ang_a), jnp.sin(ang_a)
    ang_b = col(COL_INVB, half_b) * pos
    cos_b, sin_b = jnp.cos(ang_b), jnp.sin(ang_b)

    cq = _mm(h, wcq_ref[...])
    ckv = _mm(h, wckv_ref[...])
    qa = _nt(wqa_ref[...], h)
    cq = _rms_lanes(cq, rows[ROW_GCQ:ROW_GCQ + 1, 0:B_Q_RANK]).astype(BF16)
    ckv = _rms_lanes(ckv, rows[ROW_GCKV:ROW_GCKV + 1, 0:B_KV_RANK]).astype(BF16)
    qb = _nt(wuq_ref[...], cq)
    kn = _nt(wuk_ref[...], ckv)
    kr = _nt(wkr_ref[...], h)

    gq = col(COL_GQA, A_HEAD_DIM)
    for hh in range(A_HEADS):
        blk = _rms_rows(qa[hh * A_HEAD_DIM:(hh + 1) * A_HEAD_DIM], gq)
        blk = _rope_rows(blk, 0, half_a, cos_a, sin_a) * (A_HEAD_DIM ** -0.5 * LOG2E)
        own = hh * LANES + (hh % 2) * A_HEAD_DIM
        other = hh * LANES + (1 - hh % 2) * A_HEAD_DIM
        qat_ref[own:own + A_HEAD_DIM, :] = blk.astype(BF16)
        qat_ref[other:other + A_HEAD_DIM, :] = jnp.zeros((A_HEAD_DIM, TM), BF16)

    ka = _nt(wka_ref[...], h)
    qm = _nt(wqm_ref[...], h)

    gq = col(COL_GQB, LANES)
    for hh in range(B_HEADS):
        blk = _rms_rows(qb[hh * LANES:(hh + 1) * LANES], gq, n=B_QK)
        blk = _rope_rows(blk, B_NOPE, half_b, cos_b, sin_b) * (B_QK ** -0.5 * LOG2E)
        qbt_ref[hh * LANES:(hh + 1) * LANES, :] = blk.astype(BF16)
    gk = col(COL_GKB, LANES)
    pad = jnp.zeros((LANES - B_QK, TM), F32)
    for hh in range(B_HEADS):
        blk = jnp.concatenate([kn[hh * B_NOPE:(hh + 1) * B_NOPE], kr, pad], axis=0)
        blk = _rope_rows(_rms_rows(blk, gk, n=B_QK), B_NOPE, half_b, cos_b, sin_b)
        kb_ref[:, hh * LANES:(hh + 1) * LANES] = _token_major([blk])

    qi = _nt(wqi_ref[...], h)
    ki = _nt(wki_ref[...], h)
    wt_ref[...] = _nt(wwi_ref[...], h)[0:IDX_HEADS] * (IDX_HEADS ** -0.5)

    gk = col(COL_GKA, A_HEAD_DIM)
    for c in range(A_HEADS // 2):
        pair = [_rope_rows(_rms_rows(ka[hh * A_HEAD_DIM:(hh + 1) * A_HEAD_DIM], gk), 0, half_a, cos_a, sin_a)
                for hh in (2 * c, 2 * c + 1)]
        ka_ref[c, :, :] = _token_major(pair)
    gm = col(COL_GQM, M_HEAD_DIM)
    for hh in range(M_HEADS):
        blk = _rms_rows(qm[hh * M_HEAD_DIM:(hh + 1) * M_HEAD_DIM], gm) * (M_HEAD_DIM ** -0.5 * LOG2E)
        qmt_ref[hh * M_HEAD_DIM:(hh + 1) * M_HEAD_DIM, :] = blk.astype(BF16)

    vbt_ref[...] = _nt(wuv_ref[...], ckv).astype(BF16)
    vat_ref[...] = _nt(wva_ref[...], h).astype(BF16)

    for hh in range(IDX_HEADS):
        blk = _rope_rows(qi[hh * IDX_DIM:(hh + 1) * IDX_DIM], 0, half_a, cos_a, sin_a)
        blk = (blk * (IDX_DIM ** -0.5)).astype(BF16)
        for j in range(TM // TQ):
            qib_ref[j, 0:IDX_DIM, hh * TQ:(hh + 1) * TQ] = blk[:, j * TQ:(j + 1) * TQ]
    qib_ref[:, IDX_DIM:, :] = jnp.zeros((TM // TQ, LANES - IDX_DIM, IDX_HEADS * TQ), BF16)
    ki_ref[...] = _token_major([_rope_rows(ki, 0, half_a, cos_a, sin_a)])


def _mem_kv(mem_ref, gains_ref, wk_ref, wvt_ref, km_ref, vmt_ref):
    hm = _rms_lanes(mem_ref[...], gains_ref[0:1, :]).astype(BF16)
    k = _mm(hm, wk_ref[...])
    gk = gains_ref[1:2, 0:M_HEAD_DIM]
    for hh in range(M_HEADS):
        kc = _rms_lanes(k[:, hh * M_HEAD_DIM:(hh + 1) * M_HEAD_DIM], gk)
        km_ref[:, hh * M_HEAD_DIM:(hh + 1) * M_HEAD_DIM] = kc.astype(BF16)
    vmt_ref[...] = _nt(wvt_ref[...], hm).astype(BF16)


def _attend(nk, n_heads, dv, q_of, k_of, v_of, bias_of, bounded, s_ref, p_ref, ot_ref):
    nq = ot_ref.shape[1]
    chunks = [slice(c * KC, (c + 1) * KC) for c in range(nk // KC)]

    def scores(hh, q, c):
        s = _mm(k_of(hh, chunks[c]), q)
        b = bias_of(c)
        return s if b is None else s + b

    depth = max(1, min(n_heads - 1, 8 // len(chunks)))
    slots = depth + 1

    def p_rows(hh, sl):
        base = (hh % slots) * nk
        return slice(base + sl.start, base + sl.stop)

    def probabilities(hh):
        q = q_of(hh)
        if bounded:
            l8 = jnp.zeros((8, nq), F32)
            for c in range(len(chunks)):
                p = jnp.exp2(scores(hh, q, c))
                l8 = l8 + p.reshape(KC // 8, 8, nq).sum(axis=0)
                p_ref[p_rows(hh, chunks[c]), :] = p.astype(BF16)
            return jnp.sum(l8, axis=0, keepdims=True)
        m = jnp.full((1, nq), -jnp.inf, F32)
        for c in range(len(chunks)):
            s = scores(hh, q, c)
            s_ref[chunks[c], :] = s
            m = jnp.maximum(m, jnp.max(s, axis=0, keepdims=True))
        l = jnp.zeros((1, nq), F32)
        for c in range(len(chunks)):
            p = jnp.exp2(s_ref[chunks[c], :] - m)
            l = l + jnp.sum(p, axis=0, keepdims=True)
            p_ref[p_rows(hh, chunks[c]), :] = p.astype(BF16)
        return l

    def weighted_values(hh, l):
        o = _mm(v_of(hh, slice(0, nk)), p_ref[p_rows(hh, slice(0, nk)), :])
        ot_ref[hh * dv:(hh + 1) * dv, :] = o / l

    sums = {}
    for step in range(n_heads + depth):
        if step < n_heads:
            sums[step] = probabilities(step)
        if step >= depth:
            weighted_values(step - depth, sums.pop(step - depth))


def _count(score_ref, nk, pred):
    cnt = jnp.zeros((COUNT_ROWS, TQ), I32)
    for r in range(0, nk, COUNT_ROWS):
        cnt = jnp.where(pred(score_ref[r:r + COUNT_ROWS, :]), cnt + 1, cnt)
    return jnp.sum(cnt, axis=0, keepdims=True)


def _ordered_to_bits(u, magnitude_mask):
    k = u ^ INT_MIN
    return k ^ ((k >> 31) & magnitude_mask)


def _ordered_pattern_to_float(u):
    return pltpu.bitcast(_ordered_to_bits(u, 0x7FFFFFFF), F32)


def _count_rounded(round_ref, nk, cand):
    assert nk // COUNT_ROWS <= 256
    one, zero = jnp.ones((), BF16), jnp.zeros((), BF16)
    cnt = jnp.zeros((COUNT_ROWS, TQ), BF16)
    for r in range(0, nk, COUNT_ROWS):
        cnt = cnt + jnp.where(round_ref[r:r + COUNT_ROWS, :] >= cand, one, zero)
    return jnp.sum(cnt.astype(F32), axis=0, keepdims=True)


def _select_topk(nk, q_pos, row, chunks, qib_ref, wt_ref, ki_ref, score_ref, round_ref, emit):
    for c, sl in enumerate(chunks):
        ki_c = ki_ref[sl, :]
        acc = jnp.zeros((KC, TQ), F32)
        for hh in range(IDX_HEADS):
            d = _mm(ki_c, qib_ref[0, :, hh * TQ:(hh + 1) * TQ])
            acc = acc + jnp.maximum(d, 0.0) * wt_ref[hh:hh + 1, :]
        score = jnp.where(row + c * KC <= q_pos, acc, NEG)
        score_ref[sl, :] = score
        round_ref[sl, :] = score.astype(BF16)

    def coarse(i, c_u):
        cand_u = c_u | jnp.left_shift(jnp.int32(1), 31 - i)
        cand = pltpu.bitcast(_ordered_to_bits(cand_u, 0x7FFF0000), F32).astype(BF16)
        return jnp.where(_count_rounded(round_ref, nk, cand) >= TOPK_MAX, cand_u, c_u)

    c_u = lax.fori_loop(0, 16, coarse, jnp.zeros((1, TQ), I32))
    pred_bits = _ordered_to_bits(c_u - (1 << 16), 0x7FFF0000)
    base_u = (pred_bits ^ ((pred_bits >> 31) & 0x7FFFFFFF)) ^ INT_MIN

    def fine(i, carry):
        off, cnt_t = carry
        cand_off = off | jnp.left_shift(jnp.int32(1), 16 - i)
        cand = _ordered_pattern_to_float(base_u + cand_off)
        cnt = _count(score_ref, nk, lambda x: x >= cand)
        ok = cnt >= TOPK_MAX
        return jnp.where(ok, cand_off, off), jnp.where(ok, cnt, cnt_t)

    off, cnt_t = lax.fori_loop(0, 17, fine, (jnp.zeros((1, TQ), I32), jnp.full((1, TQ), nk, I32)))
    thr = _ordered_pattern_to_float(base_u + off)
    split_ties = jnp.max(jnp.where(cnt_t > TOPK_MAX, 1, 0)) > 0

    @pl.when(jnp.logical_not(split_ties))
    def _():
        for c, sl in enumerate(chunks):
            emit(sl, (score_ref[sl, :] >= thr) & (row + c * KC <= q_pos))

    @pl.when(split_ties)
    def _():
        room = (TOPK_MAX - _count(score_ref, nk, lambda x: x > thr)).astype(F32)
        tri = lax.broadcasted_iota(I32, (KC, KC), 0) >= lax.broadcasted_iota(I32, (KC, KC), 1)
        tri = jnp.where(tri, 1.0, 0.0).astype(BF16)
        running = jnp.zeros((1, TQ), F32)
        for c, sl in enumerate(chunks):
            x = score_ref[sl, :]
            tie = x == thr
            rank = _mm(tri, jnp.where(tie, 1.0, 0.0).astype(BF16)) + running
            running = rank[KC - 1:KC, :]
            emit(sl, ((x > thr) | (tie & (rank <= room))) & (row + c * KC <= q_pos))


def _dsa_body(nk, start, bounded, qat_ref, qib_ref, wt_ref, ka_ref, vat_ref, ki_ref, oa_ref,
              score_ref, round_ref, bias_ref, s_ref, p_ref, ot_ref):
    q_pos = start + lax.broadcasted_iota(I32, (1, TQ), 1)
    row = lax.broadcasted_iota(I32, (KC, TQ), 0)
    chunks = [slice(c * KC, (c + 1) * KC) for c in range(nk // KC)]

    def emit_bias(sl, keep):
        bias_ref[sl, :] = jnp.where(keep, 0.0, NEG)

    if nk <= TOPK_MAX:
        for c, sl in enumerate(chunks):
            emit_bias(sl, row + c * KC <= q_pos)
    else:
        _select_topk(nk, q_pos, row, chunks, qib_ref, wt_ref, ki_ref, score_ref, round_ref, emit_bias)

    def q_of(hh):
        return qat_ref[hh * LANES:(hh + 1) * LANES, :]

    def k_of(hh, sl):
        return ka_ref[hh // 2, sl, :]

    def v_of(hh, sl):
        return vat_ref[hh * A_HEAD_DIM:(hh + 1) * A_HEAD_DIM, sl]

    _attend(nk, A_HEADS, A_HEAD_DIM, q_of, k_of, v_of, lambda c: bias_ref[chunks[c], :], bounded,
            s_ref, p_ref, ot_ref)
    oa_ref[...] = ot_ref[...].T


def _dsa_kernel(qat_ref, qib_ref, wt_ref, ka_ref, vat_ref, ki_ref, oa_ref,
                score_ref, round_ref, bias_ref, s_ref, p_ref, ot_ref, *, seq, bounded):
    for cls in range(seq // TQ):
        cols = slice(cls * TQ, (cls + 1) * TQ)
        _dsa_body(TQ * (cls + 1), cls * TQ, bounded, qat_ref.at[:, cols], qib_ref.at[pl.ds(cls, 1)],
                  wt_ref.at[:, cols], ka_ref, vat_ref, ki_ref, oa_ref.at[cols, :],
                  score_ref, round_ref, bias_ref, s_ref, p_ref, ot_ref)


def _mla_body(nk, start, bounded, qbt_ref, kb_ref, vbt_ref, ob_ref, bias_ref, s_ref, p_ref, ot_ref):
    last = nk // KC - 1
    q_pos = start + lax.broadcasted_iota(I32, (1, TQ), 1)
    row = lax.broadcasted_iota(I32, (KC, TQ), 0)
    bias_ref[0:KC, :] = jnp.where(row + last * KC <= q_pos, 0.0, NEG)

    def q_of(hh):
        return qbt_ref[hh * LANES:(hh + 1) * LANES, :]

    def k_of(hh, sl):
        return kb_ref[sl, hh * LANES:(hh + 1) * LANES]

    def v_of(hh, sl):
        return vbt_ref[hh * B_VDIM:(hh + 1) * B_VDIM, sl]

    _attend(nk, B_HEADS, B_VDIM, q_of, k_of, v_of, lambda c: bias_ref[0:KC, :] if c == last else None,
            bounded, s_ref, p_ref, ot_ref)
    ob_ref[...] = ot_ref[...].T


def _mla_kernel(qbt_ref, kb_ref, vbt_ref, ob_ref, bias_ref, s_ref, p_ref, ot_ref, *, seq, bounded):
    for cls in range(seq // TQ):
        cols = slice(cls * TQ, (cls + 1) * TQ)
        _mla_body(TQ * (cls + 1), cls * TQ, bounded, qbt_ref.at[:, cols], kb_ref, vbt_ref, ob_ref.at[cols, :],
                  bias_ref.at[cls], s_ref, p_ref.at[cls % 2], ot_ref.at[cls])


def _mem_attn_kernel(qmt_ref, mem_ref, gains_ref, wk_ref, wvt_ref, om_ref, km_ref, vmt_ref, s_ref, p_ref, ot_ref,
                     *, mem_len, bounded):
    _mem_kv(mem_ref, gains_ref, wk_ref, wvt_ref, km_ref, vmt_ref)

    def k_of(hh, sl):
        return km_ref[sl, hh * M_HEAD_DIM:(hh + 1) * M_HEAD_DIM]

    def v_of(hh, sl):
        return vmt_ref[hh * M_HEAD_DIM:(hh + 1) * M_HEAD_DIM, sl]

    for blk in range(qmt_ref.shape[1] // TQ_MEM):
        cols = slice(blk * TQ_MEM, (blk + 1) * TQ_MEM)

        def q_of(hh):
            return qmt_ref[hh * M_HEAD_DIM:(hh + 1) * M_HEAD_DIM, cols]

        _attend(mem_len, M_HEADS, M_HEAD_DIM, q_of, k_of, v_of, lambda c: None, bounded, s_ref,
                p_ref.at[blk % 2], ot_ref.at[blk % 2])
        om_ref[cols, :] = ot_ref[blk % 2].T


def _final_kernel(x_ref, oa_ref, ob_ref, om_ref, rows_ref, wz_ref, wg_ref, wb_ref, wo_ref, out_ref):
    x = x_ref[...]
    h = _rms_lanes(x, rows_ref[ROW_GN:ROW_GN + 1, :]).astype(BF16)
    zs = [_mm(h, wz_ref[n]) for n in range(N_BRANCH)]
    gate_logits = [_mm(h, wg_ref[:, n * D_MODEL:(n + 1) * D_MODEL]) for n in range(N_BRANCH)]
    merged = jnp.zeros((TM, D_MODEL), F32)
    for n, o_ref in enumerate((oa_ref, ob_ref, om_ref)):
        y = (o_ref[...] * (zs[n] * jax.nn.sigmoid(zs[n]))).astype(BF16)
        branch = _mm(y, wb_ref[n])
        merged = merged + jax.nn.sigmoid(gate_logits[n]) * branch
    out_ref[...] = x + _mm(merged.astype(BF16), wo_ref[...])


def _full(shape):
    return pl.BlockSpec(shape, lambda *_: (0,) * len(shape), pipeline_mode=pl.Buffered(1))


def _params(n_axes):
    return pltpu.CompilerParams(dimension_semantics=("arbitrary",) * n_axes,
                                vmem_limit_bytes=VMEM_LIMIT)


def kernel(x, mem, positions, g_norm, w_in, g_qn_a, g_kn_a, g_cq, g_ckv, w_uq, w_ukv, g_qn_b, g_kn_b,
           g_mem, w_mem_kv, g_qn_m, g_kn_m, w_branch, w_out):
    b, s, d = x.shape
    m_len = mem.shape[1]
    n = b * s
    nq = s // TQ
    assert d == D_MODEL and s % TQ == 0 and TQ == KC and TM % TQ == 0 and n % TM == 0 and m_len % KC == 0
    assert g_norm.shape[0] == 1, "single-layer block"

    w = w_in[0]
    off = np.cumsum([0, 512, 512, 512, 512, IDX_DIM, IDX_HEADS, BRANCH_WIDTH, B_Q_RANK, B_KV_RANK, B_ROPE,
                     BRANCH_WIDTH, M_HEADS * M_HEAD_DIM, BRANCH_WIDTH, N_BRANCH * D_MODEL])
    seg = [w[:, off[i]:off[i + 1]] for i in range(14)]
    (w_qa, w_ka, w_va, w_qi, w_ki, w_wi, w_za, w_cq, w_ckv, w_kr, w_zb, w_qm, w_zm, w_gate) = seg
    bf = lambda a: a.astype(BF16)
    wqa_t, wqi_t, wva_t, wqm_t = bf(w_qa.T), bf(w_qi.T), bf(w_va.T), bf(w_qm.T)
    wka_t, wki_t, wkr_t = bf(w_ka.T), bf(w_ki.T), bf(w_kr.T)
    wwi_t = bf(jnp.pad(w_wi.T, ((0, 16 - IDX_HEADS), (0, 0))))
    wuq_t = bf(jnp.pad(w_uq[0].reshape(B_Q_RANK, B_HEADS, B_QK), ((0, 0), (0, 0), (0, LANES - B_QK)))
               .reshape(B_Q_RANK, B_HEADS * LANES).T)
    ukv = w_ukv[0].reshape(B_KV_RANK, B_HEADS, B_NOPE + B_VDIM)
    wuk_t = bf(ukv[:, :, :B_NOPE].reshape(B_KV_RANK, B_HEADS * B_NOPE).T)
    wuv_t = bf(ukv[:, :, B_NOPE:].reshape(B_KV_RANK, B_HEADS * B_VDIM).T)
    wmk = bf(w_mem_kv[0][:, :M_HEADS * M_HEAD_DIM])
    wmv_t = bf(w_mem_kv[0][:, M_HEADS * M_HEAD_DIM:].T)
    wz = bf(jnp.stack([w_za, w_zb, w_zm]))
    wg = bf(w_gate)
    wb = bf(w_branch[0])
    wo = bf(w_out[0])

    pad_to = lambda v, size: jnp.pad(v, (0, size - v.shape[0]))
    inv_a = ROPE_THETA ** (-(jnp.arange(0, A_ROT, 2, dtype=F32) / A_ROT))
    inv_b = ROPE_THETA ** (-(jnp.arange(0, B_ROPE, 2, dtype=F32) / B_ROPE))
    col_vectors = [None] * N_COLS
    col_vectors[COL_GQA], col_vectors[COL_GKA] = g_qn_a[0], g_kn_a[0]
    col_vectors[COL_GQB], col_vectors[COL_GKB] = g_qn_b[0], g_kn_b[0]
    col_vectors[COL_GQM], col_vectors[COL_INVA], col_vectors[COL_INVB] = g_qn_m[0], inv_a, inv_b
    cols = jnp.stack([pad_to(v, LANES) for v in col_vectors], axis=1)
    row_vectors = [None] * N_ROWS
    row_vectors[ROW_GN], row_vectors[ROW_GCQ], row_vectors[ROW_GCKV] = g_norm[0], g_cq[0], g_ckv[0]
    rows = jnp.stack([pad_to(v, D_MODEL) for v in row_vectors])
    mem_gains = jnp.stack([g_mem[0], pad_to(g_kn_m[0], D_MODEL)])

    x2 = x.reshape(n, d)
    pos_r = positions.reshape(1, n)
    tile = lambda width: pl.BlockSpec((TM, width), lambda i: (i, 0))
    tile_t = lambda rows: pl.BlockSpec((rows, TM), lambda i: (0, i))
    pos_spec = pl.BlockSpec((1, TM), lambda i: (0, i))

    a_w = [wqa_t, wqi_t, wwi_t, wka_t, wva_t, wki_t]
    b_w = [bf(w_cq), bf(w_ckv), wkr_t, wqm_t, wuq_t, wuk_t, wuv_t]
    qat, qib, wt, ka, vat, ki, qbt, kb, vbt, qmt = pl.pallas_call(
        _proj_kernel,
        grid=(n // TM,),
        in_specs=[tile(d), pos_spec, _full(rows.shape), _full(cols.shape)] + [_full(a.shape) for a in a_w + b_w],
        out_specs=[tile_t(A_HEADS * LANES), pl.BlockSpec((TM // TQ, LANES, IDX_HEADS * TQ), lambda i: (i, 0, 0)),
                   tile_t(IDX_HEADS), pl.BlockSpec((512 // LANES, TM, LANES), lambda i: (0, i, 0)),
                   tile_t(512), tile(LANES),
                   tile_t(B_HEADS * LANES), tile(B_HEADS * LANES), tile_t(512), tile_t(512)],
        out_shape=[jax.ShapeDtypeStruct((A_HEADS * LANES, n), BF16),
                   jax.ShapeDtypeStruct((n // TQ, LANES, IDX_HEADS * TQ), BF16),
                   jax.ShapeDtypeStruct((IDX_HEADS, n), F32),
                   jax.ShapeDtypeStruct((512 // LANES, n, LANES), BF16),
                   jax.ShapeDtypeStruct((512, n), BF16),
                   jax.ShapeDtypeStruct((n, LANES), BF16),
                   jax.ShapeDtypeStruct((B_HEADS * LANES, n), BF16),
                   jax.ShapeDtypeStruct((n, B_HEADS * LANES), BF16),
                   jax.ShapeDtypeStruct((512, n), BF16),
                   jax.ShapeDtypeStruct((512, n), BF16)],
        compiler_params=_params(1), name="proj",
    )(x2, pos_r, rows, cols, *a_w, *b_w)

    o_shape = jax.ShapeDtypeStruct((n, 512), F32)

    def dsa(bounded):
        return pl.pallas_call(
            functools.partial(_dsa_kernel, seq=s, bounded=bounded),
            grid=(b,),
            in_specs=[pl.BlockSpec((A_HEADS * LANES, s), lambda bi: (0, bi)),
                      pl.BlockSpec((nq, LANES, IDX_HEADS * TQ), lambda bi: (bi, 0, 0)),
                      pl.BlockSpec((IDX_HEADS, s), lambda bi: (0, bi)),
                      pl.BlockSpec((512 // LANES, s, LANES), lambda bi: (0, bi, 0)),
                      pl.BlockSpec((512, s), lambda bi: (0, bi)), pl.BlockSpec((s, LANES), lambda bi: (bi, 0))],
            out_specs=pl.BlockSpec((s, 512), lambda bi: (bi, 0)), out_shape=o_shape,
            scratch_shapes=[pltpu.VMEM((s, TQ), F32), pltpu.VMEM((s, TQ), BF16), pltpu.VMEM((s, TQ), F32),
                            pltpu.VMEM((s, TQ), F32),
                            pltpu.VMEM((2 * s, TQ), BF16), pltpu.VMEM((512, TQ), F32)],
            compiler_params=_params(1), name="dsa" if bounded else "dsa_general")

    def mla(bounded):
        return pl.pallas_call(
            functools.partial(_mla_kernel, seq=s, bounded=bounded),
            grid=(b,),
            in_specs=[pl.BlockSpec((B_HEADS * LANES, s), lambda bi: (0, bi)),
                      pl.BlockSpec((s, B_HEADS * LANES), lambda bi: (bi, 0)),
                      pl.BlockSpec((512, s), lambda bi: (0, bi))],
            out_specs=pl.BlockSpec((s, 512), lambda bi: (bi, 0)), out_shape=o_shape,
            scratch_shapes=[pltpu.VMEM((nq, KC, TQ), F32), pltpu.VMEM((s, TQ), F32),
                            pltpu.VMEM((2, 2 * s, TQ), BF16), pltpu.VMEM((nq, 512, TQ), F32)],
            compiler_params=_params(1), name="mla" if bounded else "mla_general")

    def mem_attn(bounded):
        return pl.pallas_call(
            functools.partial(_mem_attn_kernel, mem_len=m_len, bounded=bounded),
            grid=(b,),
            in_specs=[pl.BlockSpec((512, s), lambda bi: (0, bi)), pl.BlockSpec((m_len, d), lambda bi: (bi, 0)),
                      _full(mem_gains.shape), _full(wmk.shape), _full(wmv_t.shape)],
            out_specs=pl.BlockSpec((s, 512), lambda bi: (bi, 0)), out_shape=o_shape,
            scratch_shapes=[pltpu.VMEM((m_len, 512), BF16), pltpu.VMEM((512, m_len), BF16),
                            pltpu.VMEM((m_len, TQ_MEM), F32), pltpu.VMEM((2, M_HEADS * m_len, TQ_MEM), BF16),
                            pltpu.VMEM((2, 512, TQ_MEM), F32)],
            compiler_params=_params(1), name="mem_attn" if bounded else "mem_attn_general")

    gain_max = jnp.max(jnp.abs(cols), axis=0)

    def score_bound(cq_, ck_, dim):
        return dim ** 0.5 * LOG2E * 1.02 * gain_max[cq_] * gain_max[ck_]

    def attention(bounded):
        def run(*ops):
            return (dsa(bounded)(*ops[:6]), mla(bounded)(*ops[6:9]), mem_attn(bounded)(*ops[9:]))
        return run

    k_mem_gain = jnp.max(jnp.abs(g_kn_m[0]))
    worst = jnp.maximum(jnp.maximum(score_bound(COL_GQA, COL_GKA, A_HEAD_DIM), score_bound(COL_GQB, COL_GKB, B_QK)),
                        M_HEAD_DIM ** 0.5 * LOG2E * 1.02 * gain_max[COL_GQM] * k_mem_gain)
    oa, ob, om = lax.cond(worst <= BOUNDED_SCORE_LIMIT, attention(True), attention(False),
                          qat, qib, wt, ka, vat, ki, qbt, kb, vbt, qmt, mem.reshape(b * m_len, d), mem_gains,
                          wmk, wmv_t)

    out = pl.pallas_call(
        _final_kernel,
        grid=(n // TM,),
        in_specs=[tile(d), tile(512), tile(512), tile(512), _full(rows.shape), _full(wz.shape), _full(wg.shape),
                  _full(wb.shape), _full(wo.shape)],
        out_specs=tile(d), out_shape=jax.ShapeDtypeStruct((n, d), x.dtype),
        compiler_params=_params(1), name="final",
    )(x2, oa, ob, om, rows, wz, wg, wb, wo)
    return out.reshape(b, s, d)
```

```python
import functools

import numpy as np
import jax
import jax.numpy as jnp
from jax import lax
from jax.experimental import pallas as pl
from jax.experimental.pallas import tpu as pltpu

F32 = jnp.float32
BF16 = jnp.bfloat16
I32 = jnp.int32

D_MODEL = 1024
ROPE_THETA = 500000.0
EPS = 1e-6
NEG = -1e30
N_BRANCH = 3
BRANCH_WIDTH = 512
A_HEADS = 8
A_HEAD_DIM = 64
A_ROT = A_HEAD_DIM // 4
IDX_HEADS = 8
IDX_DIM = 64
TOPK_MAX = 256
B_HEADS = 8
B_NOPE = 64
B_ROPE = 32
B_VDIM = 64
B_QK = B_NOPE + B_ROPE
B_Q_RANK = 384
B_KV_RANK = 256
M_HEADS = 4
M_HEAD_DIM = 128

LANES = 128
TM = 512
TQ = 256
TQ_MEM = 512
KC = 256
COUNT_ROWS = 64
VMEM_LIMIT = 56 * 1024 * 1024
INT_MIN = -2 ** 31
LOG2E = 1.4426950408889634
BOUNDED_SCORE_LIMIT = 32.0


def _nt(a, b):
    return lax.dot_general(a, b, (((1,), (1,)), ((), ())), preferred_element_type=F32)


def _mm(a, b):
    return jnp.dot(a, b, preferred_element_type=F32)


def _rms_lanes(xf, g_row, n=None):
    n = xf.shape[-1] if n is None else n
    ms = jnp.sum(xf * xf, axis=-1, keepdims=True) / n
    return xf * lax.rsqrt(ms + EPS) * g_row


def _rms_rows(blk, g_col, n=None):
    n = blk.shape[0] if n is None else n
    ms = jnp.sum(blk * blk, axis=0, keepdims=True) / n
    return blk * lax.rsqrt(ms + EPS) * g_col


def _rope_rows(blk, lo, half, cos_t, sin_t):
    x1 = blk[lo:lo + half]
    x2 = blk[lo + half:lo + 2 * half]
    parts = []
    if lo:
        parts.append(blk[:lo])
    parts += [x1 * cos_t - x2 * sin_t, x2 * cos_t + x1 * sin_t]
    if lo + 2 * half < blk.shape[0]:
        parts.append(blk[lo + 2 * half:])
    return jnp.concatenate(parts, axis=0)


def _token_major(blocks):
    rows = sum(blk.shape[0] for blk in blocks)
    if rows < LANES:
        blocks = list(blocks) + [jnp.zeros((LANES - rows, blocks[0].shape[1]), F32)]
    return jnp.concatenate(blocks, axis=0).T.astype(BF16)


W_ROWS = 128


def _weights_kernel(w_ref, wqa_ref, wka_ref, wva_ref, wqi_ref, wqm_ref, wki_ref, wwi_ref, wcq_ref, wckv_ref,
                    wz_ref, wg_ref, *, off):
    seg = lambda i: w_ref[:, off[i]:off[i + 1]]
    for i, o_ref in ((0, wqa_ref), (1, wka_ref), (2, wva_ref), (3, wqi_ref), (11, wqm_ref)):
        o_ref[...] = seg(i).T.astype(BF16)
    slab = w_ref[:, off[4]:off[4] + LANES].T
    wki_ref[...] = slab[:IDX_DIM].astype(BF16)
    head_row = lax.broadcasted_iota(I32, (16, W_ROWS), 0)
    wwi_ref[...] = jnp.where(head_row < IDX_HEADS, slab[IDX_DIM:IDX_DIM + 16], 0.0).astype(BF16)
    wcq_ref[...] = seg(7).astype(BF16)
    wckv_ref[...] = seg(8).astype(BF16)
    for n, i in enumerate((6, 10, 12)):
        wz_ref[n] = seg(i).astype(BF16)
    wg_ref[...] = seg(13).astype(BF16)


COL_GQA, COL_GKA, COL_GQB, COL_GKB, COL_GQM, COL_INVA, COL_INVB, N_COLS = 0, 1, 2, 3, 4, 5, 6, 7
ROW_GN, ROW_GCQ, ROW_GCKV, N_ROWS = 0, 1, 2, 3


def _proj_kernel(x_ref, posr_ref, rows_ref, cols_ref,
                 wqa_ref, wqi_ref, wwi_ref, wka_ref, wva_ref, wki_ref,
                 wcq_ref, wckv_ref, wkr_ref, wqm_ref, wuq_ref, wuk_ref, wuv_ref,
                 qat_ref, qib_ref, wt_ref, ka_ref, vat_ref, ki_ref, qbt_ref, kb_ref, vbt_ref, qmt_ref):
    rows, cols = rows_ref[...], cols_ref[...]
    col = lambda j, n: cols[0:n, j:j + 1]
    h = _rms_lanes(x_ref[...], rows[ROW_GN:ROW_GN + 1, :]).astype(BF16)
    pos = posr_ref[...].astype(F32)
    half_a, half_b = A_ROT // 2, B_ROPE // 2
    ang_a = col(COL_INVA, half_a) * pos
    cos_a, sin_a = jnp.cos(ang_a), jnp.sin(ang_a)
    ang_b = col(COL_INVB, half_b) * pos
    cos_b, sin_b = jnp.cos(ang_b), jnp.sin(ang_b)

    cq = _mm(h, wcq_ref[...])
    ckv = _mm(h, wckv_ref[...])
    qa = _nt(wqa_ref[...], h)
    cq = _rms_lanes(cq, rows[ROW_GCQ:ROW_GCQ + 1, 0:B_Q_RANK]).astype(BF16)
    ckv = _rms_lanes(ckv, rows[ROW_GCKV:ROW_GCKV + 1, 0:B_KV_RANK]).astype(BF16)
    qb = _nt(wuq_ref[...], cq)
    kn = _nt(wuk_ref[...], ckv)
    kr = _nt(wkr_ref[...], h)

    gq = col(COL_GQA, A_HEAD_DIM)
    for hh in range(A_HEADS):
        blk = _rms_rows(qa[hh * A_HEAD_DIM:(hh + 1) * A_HEAD_DIM], gq)
        blk = _rope_rows(blk, 0, half_a, cos_a, sin_a) * (A_HEAD_DIM ** -0.5 * LOG2E)
        own = hh * LANES + (hh % 2) * A_HEAD_DIM
        other = hh * LANES + (1 - hh % 2) * A_HEAD_DIM
        qat_ref[own:own + A_HEAD_DIM, :] = blk.astype(BF16)
        qat_ref[other:other + A_HEAD_DIM, :] = jnp.zeros((A_HEAD_DIM, TM), BF16)

    ka = _nt(wka_ref[...], h)
    qm = _nt(wqm_ref[...], h)

    gq = col(COL_GQB, LANES)
    for hh in range(B_HEADS):
        blk = _rms_rows(qb[hh * LANES:(hh + 1) * LANES], gq, n=B_QK)
        blk = _rope_rows(blk, B_NOPE, half_b, cos_b, sin_b) * (B_QK ** -0.5 * LOG2E)
        qbt_ref[hh * LANES:(hh + 1) * LANES, :] = blk.astype(BF16)
    gk = col(COL_GKB, LANES)
    pad = jnp.zeros((LANES - B_QK, TM), F32)
    for hh in range(B_HEADS):
        blk = jnp.concatenate([kn[hh * B_NOPE:(hh + 1) * B_NOPE], kr, pad], axis=0)
        blk = _rope_rows(_rms_rows(blk, gk, n=B_QK), B_NOPE, half_b, cos_b, sin_b)
        kb_ref[:, hh * LANES:(hh + 1) * LANES] = _token_major([blk])

    qi = _nt(wqi_ref[...], h)
    ki = _nt(wki_ref[...], h)
    wt_ref[...] = _nt(wwi_ref[...], h)[0:IDX_HEADS] * (IDX_HEADS ** -0.5)

    gk = col(COL_GKA, A_HEAD_DIM)
    for c in range(A_HEADS // 2):
        pair = [_rope_rows(_rms_rows(ka[hh * A_HEAD_DIM:(hh + 1) * A_HEAD_DIM], gk), 0, half_a, cos_a, sin_a)
                for hh in (2 * c, 2 * c + 1)]
        ka_ref[c, :, :] = _token_major(pair)
    gm = col(COL_GQM, M_HEAD_DIM)
    for hh in range(M_HEADS):
        blk = _rms_rows(qm[hh * M_HEAD_DIM:(hh + 1) * M_HEAD_DIM], gm) * (M_HEAD_DIM ** -0.5 * LOG2E)
        qmt_ref[hh * M_HEAD_DIM:(hh + 1) * M_HEAD_DIM, :] = blk.astype(BF16)

    vbt_ref[...] = _nt(wuv_ref[...], ckv).astype(BF16)
    vat_ref[...] = _nt(wva_ref[...], h).astype(BF16)

    for hh in range(IDX_HEADS):
        blk = _rope_rows(qi[hh * IDX_DIM:(hh + 1) * IDX_DIM], 0, half_a, cos_a, sin_a)
        blk = (blk * (IDX_DIM ** -0.5)).astype(BF16)
        for j in range(TM // TQ):
            qib_ref[j, 0:IDX_DIM, hh * TQ:(hh + 1) * TQ] = blk[:, j * TQ:(j + 1) * TQ]
    qib_ref[:, IDX_DIM:, :] = jnp.zeros((TM // TQ, LANES - IDX_DIM, IDX_HEADS * TQ), BF16)
    ki_ref[...] = _token_major([_rope_rows(ki, 0, half_a, cos_a, sin_a)])


def _mem_kv(mem_ref, gains_ref, wk_ref, wvt_ref, km_ref, vmt_ref):
    hm = _rms_lanes(mem_ref[...], gains_ref[0:1, :]).astype(BF16)
    k = _mm(hm, wk_ref[...])
    gk = gains_ref[1:2, 0:M_HEAD_DIM]
    for hh in range(M_HEADS):
        kc = _rms_lanes(k[:, hh * M_HEAD_DIM:(hh + 1) * M_HEAD_DIM], gk)
        km_ref[:, hh * M_HEAD_DIM:(hh + 1) * M_HEAD_DIM] = kc.astype(BF16)
    vmt_ref[...] = _nt(wvt_ref[...], hm).astype(BF16)


def _attend(nk, n_heads, dv, q_of, k_of, v_of, bias_of, bounded, s_ref, p_ref, ot_ref):
    nq = ot_ref.shape[1]
    chunks = [slice(c * KC, (c + 1) * KC) for c in range(nk // KC)]

    def scores(hh, q, c):
        s = _mm(k_of(hh, chunks[c]), q)
        b = bias_of(c)
        return s if b is None else s + b

    depth = max(1, min(n_heads - 1, 8 // len(chunks)))
    slots = depth + 1

    def p_rows(hh, sl):
        base = (hh % slots) * nk
        return slice(base + sl.start, base + sl.stop)

    def probabilities(hh):
        q = q_of(hh)
        if bounded:
            l8 = jnp.zeros((8, nq), F32)
            for c in range(len(chunks)):
                p = jnp.exp2(scores(hh, q, c))
                l8 = l8 + p.reshape(KC // 8, 8, nq).sum(axis=0)
                p_ref[p_rows(hh, chunks[c]), :] = p.astype(BF16)
            return jnp.sum(l8, axis=0, keepdims=True)
        m = jnp.full((1, nq), -jnp.inf, F32)
        for c in range(len(chunks)):
            s = scores(hh, q, c)
            s_ref[chunks[c], :] = s
            m = jnp.maximum(m, jnp.max(s, axis=0, keepdims=True))
        l = jnp.zeros((1, nq), F32)
        for c in range(len(chunks)):
            p = jnp.exp2(s_ref[chunks[c], :] - m)
            l = l + jnp.sum(p, axis=0, keepdims=True)
            p_ref[p_rows(hh, chunks[c]), :] = p.astype(BF16)
        return l

    def weighted_values(hh, l):
        o = _mm(v_of(hh, slice(0, nk)), p_ref[p_rows(hh, slice(0, nk)), :])
        ot_ref[hh * dv:(hh + 1) * dv, :] = o / l

    sums = {}
    for step in range(n_heads + depth):
        if step < n_heads:
            sums[step] = probabilities(step)
        if step >= depth:
            weighted_values(step - depth, sums.pop(step - depth))


def _count(score_ref, nk, pred):
    cnt = jnp.zeros((COUNT_ROWS, TQ), I32)
    for r in range(0, nk, COUNT_ROWS):
        cnt = jnp.where(pred(score_ref[r:r + COUNT_ROWS, :]), cnt + 1, cnt)
    return jnp.sum(cnt, axis=0, keepdims=True)


def _ordered_to_bits(u, magnitude_mask):
    k = u ^ INT_MIN
    return k ^ ((k >> 31) & magnitude_mask)


def _ordered_pattern_to_float(u):
    return pltpu.bitcast(_ordered_to_bits(u, 0x7FFFFFFF), F32)


def _count_rounded(round_ref, nk, cand):
    assert nk // COUNT_ROWS <= 256
    one, zero = jnp.ones((), BF16), jnp.zeros((), BF16)
    cnt = jnp.zeros((COUNT_ROWS, TQ), BF16)
    for r in range(0, nk, COUNT_ROWS):
        cnt = cnt + jnp.where(round_ref[r:r + COUNT_ROWS, :] >= cand, one, zero)
    return jnp.sum(cnt.astype(F32), axis=0, keepdims=True)


def _select_topk(nk, q_pos, row, chunks, qib_ref, wt_ref, ki_ref, score_ref, round_ref, emit):
    for c, sl in enumerate(chunks):
        ki_c = ki_ref[sl, :]
        acc = jnp.zeros((KC, TQ), F32)
        for hh in range(IDX_HEADS):
            d = _mm(ki_c, qib_ref[0, :, hh * TQ:(hh + 1) * TQ])
            acc = acc + jnp.maximum(d, 0.0) * wt_ref[hh:hh + 1, :]
        score = jnp.where(row + c * KC <= q_pos, acc, NEG)
        score_ref[sl, :] = score
        round_ref[sl, :] = score.astype(BF16)

    def coarse(i, c_u):
        cand_u = c_u | jnp.left_shift(jnp.int32(1), 31 - i)
        cand = pltpu.bitcast(_ordered_to_bits(cand_u, 0x7FFF0000), F32).astype(BF16)
        return jnp.where(_count_rounded(round_ref, nk, cand) >= TOPK_MAX, cand_u, c_u)

    c_u = lax.fori_loop(0, 16, coarse, jnp.zeros((1, TQ), I32))
    pred_bits = _ordered_to_bits(c_u - (1 << 16), 0x7FFF0000)
    base_u = (pred_bits ^ ((pred_bits >> 31) & 0x7FFFFFFF)) ^ INT_MIN

    def fine(i, carry):
        off, cnt_t = carry
        cand_off = off | jnp.left_shift(jnp.int32(1), 16 - i)
        cand = _ordered_pattern_to_float(base_u + cand_off)
        cnt = _count(score_ref, nk, lambda x: x >= cand)
        ok = cnt >= TOPK_MAX
        return jnp.where(ok, cand_off, off), jnp.where(ok, cnt, cnt_t)

    off, cnt_t = lax.fori_loop(0, 17, fine, (jnp.zeros((1, TQ), I32), jnp.full((1, TQ), nk, I32)))
    thr = _ordered_pattern_to_float(base_u + off)
    split_ties = jnp.max(jnp.where(cnt_t > TOPK_MAX, 1, 0)) > 0

    @pl.when(jnp.logical_not(split_ties))
    def _():
        for c, sl in enumerate(chunks):
            emit(sl, (score_ref[sl, :] >= thr) & (row + c * KC <= q_pos))

    @pl.when(split_ties)
    def _():
        room = (TOPK_MAX - _count(score_ref, nk, lambda x: x > thr)).astype(F32)
        tri = lax.broadcasted_iota(I32, (KC, KC), 0) >= lax.broadcasted_iota(I32, (KC, KC), 1)
        tri = jnp.where(tri, 1.0, 0.0).astype(BF16)
        running = jnp.zeros((1, TQ), F32)
        for c, sl in enumerate(chunks):
            x = score_ref[sl, :]
            tie = x == thr
            rank = _mm(tri, jnp.where(tie, 1.0, 0.0).astype(BF16)) + running
            running = rank[KC - 1:KC, :]
            emit(sl, ((x > thr) | (tie & (rank <= room))) & (row + c * KC <= q_pos))


def _dsa_body(nk, start, bounded, qat_ref, qib_ref, wt_ref, ka_ref, vat_ref, ki_ref, oa_ref,
              score_ref, round_ref, bias_ref, s_ref, p_ref, ot_ref):
    q_pos = start + lax.broadcasted_iota(I32, (1, TQ), 1)
    row = lax.broadcasted_iota(I32, (KC, TQ), 0)
    chunks = [slice(c * KC, (c + 1) * KC) for c in range(nk // KC)]

    def emit_bias(sl, keep):
        bias_ref[sl, :] = jnp.where(keep, 0.0, NEG)

    if nk <= TOPK_MAX:
        for c, sl in enumerate(chunks):
            emit_bias(sl, row + c * KC <= q_pos)
    else:
        _select_topk(nk, q_pos, row, chunks, qib_ref, wt_ref, ki_ref, score_ref, round_ref, emit_bias)

    def q_of(hh):
        return qat_ref[hh * LANES:(hh + 1) * LANES, :]

    def k_of(hh, sl):
        return ka_ref[hh // 2, sl, :]

    def v_of(hh, sl):
        return vat_ref[hh * A_HEAD_DIM:(hh + 1) * A_HEAD_DIM, sl]

    _attend(nk, A_HEADS, A_HEAD_DIM, q_of, k_of, v_of, lambda c: bias_ref[chunks[c], :], bounded,
            s_ref, p_ref, ot_ref)
    oa_ref[...] = ot_ref[...].T


def _dsa_kernel(qat_ref, qib_ref, wt_ref, ka_ref, vat_ref, ki_ref, oa_ref,
                score_ref, round_ref, bias_ref, s_ref, p_ref, ot_ref, *, seq, bounded):
    for cls in range(seq // TQ):
        cols = slice(cls * TQ, (cls + 1) * TQ)
        _dsa_body(TQ * (cls + 1), cls * TQ, bounded, qat_ref.at[:, cols], qib_ref.at[pl.ds(cls, 1)],
                  wt_ref.at[:, cols], ka_ref, vat_ref, ki_ref, oa_ref.at[cols, :],
                  score_ref, round_ref, bias_ref, s_ref, p_ref, ot_ref)


def _mla_body(nk, start, bounded, qbt_ref, kb_ref, vbt_ref, ob_ref, bias_ref, s_ref, p_ref, ot_ref):
    last = nk // KC - 1
    q_pos = start + lax.broadcasted_iota(I32, (1, TQ), 1)
    row = lax.broadcasted_iota(I32, (KC, TQ), 0)
    bias_ref[0:KC, :] = jnp.where(row + last * KC <= q_pos, 0.0, NEG)

    def q_of(hh):
        return qbt_ref[hh * LANES:(hh + 1) * LANES, :]

    def k_of(hh, sl):
        return kb_ref[sl, hh * LANES:(hh + 1) * LANES]

    def v_of(hh, sl):
        return vbt_ref[hh * B_VDIM:(hh + 1) * B_VDIM, sl]

    _attend(nk, B_HEADS, B_VDIM, q_of, k_of, v_of, lambda c: bias_ref[0:KC, :] if c == last else None,
            bounded, s_ref, p_ref, ot_ref)
    ob_ref[...] = ot_ref[...].T


def _mla_kernel(qbt_ref, kb_ref, vbt_ref, ob_ref, bias_ref, s_ref, p_ref, ot_ref, *, seq, bounded):
    for cls in range(seq // TQ):
        cols = slice(cls * TQ, (cls + 1) * TQ)
        _mla_body(TQ * (cls + 1), cls * TQ, bounded, qbt_ref.at[:, cols], kb_ref, vbt_ref, ob_ref.at[cols, :],
                  bias_ref.at[cls], s_ref, p_ref.at[cls % 2], ot_ref.at[cls])


def _mem_attn_kernel(qmt_ref, mem_ref, gains_ref, wk_ref, wvt_ref, om_ref, km_ref, vmt_ref, s_ref, p_ref, ot_ref,
                     *, mem_len, bounded):
    _mem_kv(mem_ref, gains_ref, wk_ref, wvt_ref, km_ref, vmt_ref)

    def k_of(hh, sl):
        return km_ref[sl, hh * M_HEAD_DIM:(hh + 1) * M_HEAD_DIM]

    def v_of(hh, sl):
        return vmt_ref[hh * M_HEAD_DIM:(hh + 1) * M_HEAD_DIM, sl]

    for blk in range(qmt_ref.shape[1] // TQ_MEM):
        cols = slice(blk * TQ_MEM, (blk + 1) * TQ_MEM)

        def q_of(hh):
            return qmt_ref[hh * M_HEAD_DIM:(hh + 1) * M_HEAD_DIM, cols]

        _attend(mem_len, M_HEADS, M_HEAD_DIM, q_of, k_of, v_of, lambda c: None, bounded, s_ref,
                p_ref.at[blk % 2], ot_ref.at[blk % 2])
        om_ref[cols, :] = ot_ref[blk % 2].T


def _final_kernel(x_ref, oa_ref, ob_ref, om_ref, rows_ref, wz_ref, wg_ref, wb_ref, wo_ref, out_ref):
    x = x_ref[...]
    h = _rms_lanes(x, rows_ref[ROW_GN:ROW_GN + 1, :]).astype(BF16)
    zs = [_mm(h, wz_ref[n]) for n in range(N_BRANCH)]
    gate_logits = [_mm(h, wg_ref[:, n * D_MODEL:(n + 1) * D_MODEL]) for n in range(N_BRANCH)]
    merged = jnp.zeros((TM, D_MODEL), F32)
    for n, o_ref in enumerate((oa_ref, ob_ref, om_ref)):
        y = (o_ref[...] * (zs[n] * jax.nn.sigmoid(zs[n]))).astype(BF16)
        branch = _mm(y, wb_ref[n])
        merged = merged + jax.nn.sigmoid(gate_logits[n]) * branch
    out_ref[...] = x + _mm(merged.astype(BF16), wo_ref[...])


def _full(shape):
    return pl.BlockSpec(shape, lambda *_: (0,) * len(shape), pipeline_mode=pl.Buffered(1))


def _params(n_axes):
    return pltpu.CompilerParams(dimension_semantics=("arbitrary",) * n_axes,
                                vmem_limit_bytes=VMEM_LIMIT)


def kernel(x, mem, positions, g_norm, w_in, g_qn_a, g_kn_a, g_cq, g_ckv, w_uq, w_ukv, g_qn_b, g_kn_b,
           g_mem, w_mem_kv, g_qn_m, g_kn_m, w_branch, w_out):
    b, s, d = x.shape
    m_len = mem.shape[1]
    n = b * s
    nq = s // TQ
    assert d == D_MODEL and s % TQ == 0 and TQ == KC and TM % TQ == 0 and n % TM == 0 and m_len % KC == 0
    assert g_norm.shape[0] == 1, "single-layer block"

    w = w_in[0]
    off = np.cumsum([0, 512, 512, 512, 512, IDX_DIM, IDX_HEADS, BRANCH_WIDTH, B_Q_RANK, B_KV_RANK, B_ROPE,
                     BRANCH_WIDTH, M_HEADS * M_HEAD_DIM, BRANCH_WIDTH, N_BRANCH * D_MODEL])
    off = tuple(int(o) for o in off)
    assert w.shape == (d, off[14]) and off[4] % LANES == 0 and IDX_DIM + 16 <= LANES and IDX_HEADS <= 16
    bf = lambda a: a.astype(BF16)
    t_block = lambda rows: pl.BlockSpec((rows, W_ROWS), lambda i: (0, i))
    r_block = lambda width: pl.BlockSpec((W_ROWS, width), lambda i: (i, 0))
    t_shape = lambda rows: jax.ShapeDtypeStruct((rows, d), BF16)
    r_shape = lambda width: jax.ShapeDtypeStruct((d, width), BF16)
    wqa_t, wka_t, wva_t, wqi_t, wqm_t, wki_t, wwi_t, wcq, wckv, wz, wg = pl.pallas_call(
        functools.partial(_weights_kernel, off=off),
        grid=(d // W_ROWS,),
        in_specs=[pl.BlockSpec((W_ROWS, off[14]), lambda i: (i, 0))],
        out_specs=[t_block(512), t_block(512), t_block(512), t_block(512), t_block(M_HEADS * M_HEAD_DIM),
                   t_block(IDX_DIM), t_block(16), r_block(B_Q_RANK), r_block(B_KV_RANK),
                   pl.BlockSpec((N_BRANCH, W_ROWS, BRANCH_WIDTH), lambda i: (0, i, 0)), r_block(N_BRANCH * D_MODEL)],
        out_shape=[t_shape(512), t_shape(512), t_shape(512), t_shape(512), t_shape(M_HEADS * M_HEAD_DIM),
                   t_shape(IDX_DIM), t_shape(16), r_shape(B_Q_RANK), r_shape(B_KV_RANK),
                   jax.ShapeDtypeStruct((N_BRANCH, d, BRANCH_WIDTH), BF16), r_shape(N_BRANCH * D_MODEL)],
        compiler_params=_params(1), name="weights",
    )(w)
    wkr_t = bf(w[:, off[9]:off[10]].T)
    wuq_t = bf(jnp.pad(w_uq[0].reshape(B_Q_RANK, B_HEADS, B_QK), ((0, 0), (0, 0), (0, LANES - B_QK)))
               .reshape(B_Q_RANK, B_HEADS * LANES).T)
    ukv = w_ukv[0].reshape(B_KV_RANK, B_HEADS, B_NOPE + B_VDIM)
    wuk_t = bf(ukv[:, :, :B_NOPE].reshape(B_KV_RANK, B_HEADS * B_NOPE).T)
    wuv_t = bf(ukv[:, :, B_NOPE:].reshape(B_KV_RANK, B_HEADS * B_VDIM).T)
    wmk = bf(w_mem_kv[0][:, :M_HEADS * M_HEAD_DIM])
    wmv_t = bf(w_mem_kv[0][:, M_HEADS * M_HEAD_DIM:].T)
    wb = bf(w_branch[0])
    wo = bf(w_out[0])

    pad_to = lambda v, size: jnp.pad(v, (0, size - v.shape[0]))
    inv_a = ROPE_THETA ** (-(jnp.arange(0, A_ROT, 2, dtype=F32) / A_ROT))
    inv_b = ROPE_THETA ** (-(jnp.arange(0, B_ROPE, 2, dtype=F32) / B_ROPE))
    col_vectors = [None] * N_COLS
    col_vectors[COL_GQA], col_vectors[COL_GKA] = g_qn_a[0], g_kn_a[0]
    col_vectors[COL_GQB], col_vectors[COL_GKB] = g_qn_b[0], g_kn_b[0]
    col_vectors[COL_GQM], col_vectors[COL_INVA], col_vectors[COL_INVB] = g_qn_m[0], inv_a, inv_b
    cols = jnp.stack([pad_to(v, LANES) for v in col_vectors], axis=1)
    row_vectors = [None] * N_ROWS
    row_vectors[ROW_GN], row_vectors[ROW_GCQ], row_vectors[ROW_GCKV] = g_norm[0], g_cq[0], g_ckv[0]
    rows = jnp.stack([pad_to(v, D_MODEL) for v in row_vectors])
    mem_gains = jnp.stack([g_mem[0], pad_to(g_kn_m[0], D_MODEL)])

    x2 = x.reshape(n, d)
    pos_r = positions.reshape(1, n)
    tile = lambda width: pl.BlockSpec((TM, width), lambda i: (i, 0))
    tile_t = lambda rows: pl.BlockSpec((rows, TM), lambda i: (0, i))
    pos_spec = pl.BlockSpec((1, TM), lambda i: (0, i))

    a_w = [wqa_t, wqi_t, wwi_t, wka_t, wva_t, wki_t]
    b_w = [wcq, wckv, wkr_t, wqm_t, wuq_t, wuk_t, wuv_t]
    qat, qib, wt, ka, vat, ki, qbt, kb, vbt, qmt = pl.pallas_call(
        _proj_kernel,
        grid=(n // TM,),
        in_specs=[tile(d), pos_spec, _full(rows.shape), _full(cols.shape)] + [_full(a.shape) for a in a_w + b_w],
        out_specs=[tile_t(A_HEADS * LANES), pl.BlockSpec((TM // TQ, LANES, IDX_HEADS * TQ), lambda i: (i, 0, 0)),
                   tile_t(IDX_HEADS), pl.BlockSpec((512 // LANES, TM, LANES), lambda i: (0, i, 0)),
                   tile_t(512), tile(LANES),
                   tile_t(B_HEADS * LANES), tile(B_HEADS * LANES), tile_t(512), tile_t(512)],
        out_shape=[jax.ShapeDtypeStruct((A_HEADS * LANES, n), BF16),
                   jax.ShapeDtypeStruct((n // TQ, LANES, IDX_HEADS * TQ), BF16),
                   jax.ShapeDtypeStruct((IDX_HEADS, n), F32),
                   jax.ShapeDtypeStruct((512 // LANES, n, LANES), BF16),
                   jax.ShapeDtypeStruct((512, n), BF16),
                   jax.ShapeDtypeStruct((n, LANES), BF16),
                   jax.ShapeDtypeStruct((B_HEADS * LANES, n), BF16),
                   jax.ShapeDtypeStruct((n, B_HEADS * LANES), BF16),
                   jax.ShapeDtypeStruct((512, n), BF16),
                   jax.ShapeDtypeStruct((512, n), BF16)],
        compiler_params=_params(1), name="proj",
    )(x2, pos_r, rows, cols, *a_w, *b_w)

    o_shape = jax.ShapeDtypeStruct((n, 512), F32)

    def dsa(bounded):
        return pl.pallas_call(
            functools.partial(_dsa_kernel, seq=s, bounded=bounded),
            grid=(b,),
            in_specs=[pl.BlockSpec((A_HEADS * LANES, s), lambda bi: (0, bi)),
                      pl.BlockSpec((nq, LANES, IDX_HEADS * TQ), lambda bi: (bi, 0, 0)),
                      pl.BlockSpec((IDX_HEADS, s), lambda bi: (0, bi)),
                      pl.BlockSpec((512 // LANES, s, LANES), lambda bi: (0, bi, 0)),
                      pl.BlockSpec((512, s), lambda bi: (0, bi)), pl.BlockSpec((s, LANES), lambda bi: (bi, 0))],
            out_specs=pl.BlockSpec((s, 512), lambda bi: (bi, 0)), out_shape=o_shape,
            scratch_shapes=[pltpu.VMEM((s, TQ), F32), pltpu.VMEM((s, TQ), BF16), pltpu.VMEM((s, TQ), F32),
                            pltpu.VMEM((s, TQ), F32),
                            pltpu.VMEM((2 * s, TQ), BF16), pltpu.VMEM((512, TQ), F32)],
            compiler_params=_params(1), name="dsa" if bounded else "dsa_general")

    def mla(bounded):
        return pl.pallas_call(
            functools.partial(_mla_kernel, seq=s, bounded=bounded),
            grid=(b,),
            in_specs=[pl.BlockSpec((B_HEADS * LANES, s), lambda bi: (0, bi)),
                      pl.BlockSpec((s, B_HEADS * LANES), lambda bi: (bi, 0)),
                      pl.BlockSpec((512, s), lambda bi: (0, bi))],
            out_specs=pl.BlockSpec((s, 512), lambda bi: (bi, 0)), out_shape=o_shape,
            scratch_shapes=[pltpu.VMEM((nq, KC, TQ), F32), pltpu.VMEM((s, TQ), F32),
                            pltpu.VMEM((2, 2 * s, TQ), BF16), pltpu.VMEM((nq, 512, TQ), F32)],
            compiler_params=_params(1), name="mla" if bounded else "mla_general")

    def mem_attn(bounded):
        return pl.pallas_call(
            functools.partial(_mem_attn_kernel, mem_len=m_len, bounded=bounded),
            grid=(b,),
            in_specs=[pl.BlockSpec((512, s), lambda bi: (0, bi)), pl.BlockSpec((m_len, d), lambda bi: (bi, 0)),
                      _full(mem_gains.shape), _full(wmk.shape), _full(wmv_t.shape)],
            out_specs=pl.BlockSpec((s, 512), lambda bi: (bi, 0)), out_shape=o_shape,
            scratch_shapes=[pltpu.VMEM((m_len, 512), BF16), pltpu.VMEM((512, m_len), BF16),
                            pltpu.VMEM((m_len, TQ_MEM), F32), pltpu.VMEM((2, M_HEADS * m_len, TQ_MEM), BF16),
                            pltpu.VMEM((2, 512, TQ_MEM), F32)],
            compiler_params=_params(1), name="mem_attn" if bounded else "mem_attn_general")

    gain_max = jnp.max(jnp.abs(cols), axis=0)

    def score_bound(cq_, ck_, dim):
        return dim ** 0.5 * LOG2E * 1.02 * gain_max[cq_] * gain_max[ck_]

    def attention(bounded):
        def run(*ops):
            return (dsa(bounded)(*ops[:6]), mla(bounded)(*ops[6:9]), mem_attn(bounded)(*ops[9:]))
        return run

    k_mem_gain = jnp.max(jnp.abs(g_kn_m[0]))
    worst = jnp.maximum(jnp.maximum(score_bound(COL_GQA, COL_GKA, A_HEAD_DIM), score_bound(COL_GQB, COL_GKB, B_QK)),
                        M_HEAD_DIM ** 0.5 * LOG2E * 1.02 * gain_max[COL_GQM] * k_mem_gain)
    oa, ob, om = lax.cond(worst <= BOUNDED_SCORE_LIMIT, attention(True), attention(False),
                          qat, qib, wt, ka, vat, ki, qbt, kb, vbt, qmt, mem.reshape(b * m_len, d), mem_gains,
                          wmk, wmv_t)

    out = pl.pallas_call(
        _final_kernel,
        grid=(n // TM,),
        in_specs=[tile(d), tile(512), tile(512), tile(512), _full(rows.shape), _full(wz.shape), _full(wg.shape),
                  _full(wb.shape), _full(wo.shape)],
        out_specs=tile(d), out_shape=jax.ShapeDtypeStruct((n, d), x.dtype),
        compiler_params=_params(1), name="final",
    )(x2, oa, ob, om, rows, wz, wg, wb, wo)
    return out.reshape(b, s, d)
```

```python
import functools

import numpy as np
import jax
import jax.numpy as jnp
from jax import lax
from jax.experimental import pallas as pl
from jax.experimental.pallas import tpu as pltpu

F32 = jnp.float32
BF16 = jnp.bfloat16
I32 = jnp.int32

D_MODEL = 1024
ROPE_THETA = 500000.0
EPS = 1e-6
NEG = -1e30
N_BRANCH = 3
BRANCH_WIDTH = 512
A_HEADS = 8
A_HEAD_DIM = 64
A_ROT = A_HEAD_DIM // 4
IDX_HEADS = 8
IDX_DIM = 64
TOPK_MAX = 256
B_HEADS = 8
B_NOPE = 64
B_ROPE = 32
B_VDIM = 64
B_QK = B_NOPE + B_ROPE
B_Q_RANK = 384
B_KV_RANK = 256
M_HEADS = 4
M_HEAD_DIM = 128

LANES = 128
TM = 512
TQ = 256
TQ_MEM = 512
KC = 256
COUNT_ROWS = 64
VMEM_LIMIT = 56 * 1024 * 1024
INT_MIN = -2 ** 31
LOG2E = 1.4426950408889634
BOUNDED_SCORE_LIMIT = 32.0


def _nt(a, b):
    return lax.dot_general(a, b, (((1,), (1,)), ((), ())), preferred_element_type=F32)


def _mm(a, b):
    return jnp.dot(a, b, preferred_element_type=F32)


def _rms_lanes(xf, g_row, n=None):
    n = xf.shape[-1] if n is None else n
    ms = jnp.sum(xf * xf, axis=-1, keepdims=True) / n
    return xf * lax.rsqrt(ms + EPS) * g_row


def _rms_rows(blk, g_col, n=None):
    n = blk.shape[0] if n is None else n
    ms = jnp.sum(blk * blk, axis=0, keepdims=True) / n
    return blk * lax.rsqrt(ms + EPS) * g_col


def _rope_rows(blk, lo, half, cos_t, sin_t):
    x1 = blk[lo:lo + half]
    x2 = blk[lo + half:lo + 2 * half]
    parts = []
    if lo:
        parts.append(blk[:lo])
    parts += [x1 * cos_t - x2 * sin_t, x2 * cos_t + x1 * sin_t]
    if lo + 2 * half < blk.shape[0]:
        parts.append(blk[lo + 2 * half:])
    return jnp.concatenate(parts, axis=0)


def _token_major(blocks):
    rows = sum(blk.shape[0] for blk in blocks)
    if rows < LANES:
        blocks = list(blocks) + [jnp.zeros((LANES - rows, blocks[0].shape[1]), F32)]
    return jnp.concatenate(blocks, axis=0).T.astype(BF16)


W_COLS = 128


def _weights_kernel(wt_ref, wqa_ref, wka_ref, wva_ref, wqi_ref, wqm_ref, wki_ref, wkr_ref, wwi_ref,
                    wcq_ref, wckv_ref, wz_ref, wg_ref, *, off):
    seg = lambda i: wt_ref[off[i]:off[i + 1], :]
    for i, o_ref in ((0, wqa_ref), (1, wka_ref), (2, wva_ref), (3, wqi_ref), (11, wqm_ref), (4, wki_ref),
                     (9, wkr_ref)):
        o_ref[...] = seg(i).astype(BF16)
    head_row = lax.broadcasted_iota(I32, (16, W_COLS), 0)
    wwi_ref[...] = jnp.where(head_row < IDX_HEADS, wt_ref[off[5]:off[5] + 16, :], 0.0).astype(BF16)
    wcq_ref[...] = seg(7).T.astype(BF16)
    wckv_ref[...] = seg(8).T.astype(BF16)
    for n, i in enumerate((6, 10, 12)):
        wz_ref[n] = seg(i).T.astype(BF16)
    wg_ref[...] = seg(13).T.astype(BF16)


COL_GQA, COL_GKA, COL_GQB, COL_GKB, COL_GQM, COL_INVA, COL_INVB, N_COLS = 0, 1, 2, 3, 4, 5, 6, 7
ROW_GN, ROW_GCQ, ROW_GCKV, N_ROWS = 0, 1, 2, 3


def _proj_kernel(x_ref, posr_ref, rows_ref, cols_ref,
                 wqa_ref, wqi_ref, wwi_ref, wka_ref, wva_ref, wki_ref,
                 wcq_ref, wckv_ref, wkr_ref, wqm_ref, wuq_ref, wuk_ref, wuv_ref,
                 qat_ref, qib_ref, wt_ref, ka_ref, vat_ref, ki_ref, qbt_ref, kb_ref, vbt_ref, qmt_ref):
    rows, cols = rows_ref[...], cols_ref[...]
    col = lambda j, n: cols[0:n, j:j + 1]
    h = _rms_lanes(x_ref[...], rows[ROW_GN:ROW_GN + 1, :]).astype(BF16)
    pos = posr_ref[...].astype(F32)
    half_a, half_b = A_ROT // 2, B_ROPE // 2
    ang_a = col(COL_INVA, half_a) * pos
    cos_a, sin_a = jnp.cos(ang_a), jnp.sin(ang_a)
    ang_b = col(COL_INVB, half_b) * pos
    cos_b, sin_b = jnp.cos(ang_b), jnp.sin(ang_b)

    cq = _mm(h, wcq_ref[...])
    ckv = _mm(h, wckv_ref[...])
    qa = _nt(wqa_ref[...], h)
    cq = _rms_lanes(cq, rows[ROW_GCQ:ROW_GCQ + 1, 0:B_Q_RANK]).astype(BF16)
    ckv = _rms_lanes(ckv, rows[ROW_GCKV:ROW_GCKV + 1, 0:B_KV_RANK]).astype(BF16)
    qb = _nt(wuq_ref[...], cq)
    kn = _nt(wuk_ref[...], ckv)
    kr = _nt(wkr_ref[...], h)

    gq = col(COL_GQA, A_HEAD_DIM)
    for hh in range(A_HEADS):
        blk = _rms_rows(qa[hh * A_HEAD_DIM:(hh + 1) * A_HEAD_DIM], gq)
        blk = _rope_rows(blk, 0, half_a, cos_a, sin_a) * (A_HEAD_DIM ** -0.5 * LOG2E)
        own = hh * LANES + (hh % 2) * A_HEAD_DIM
        other = hh * LANES + (1 - hh % 2) * A_HEAD_DIM
        qat_ref[own:own + A_HEAD_DIM, :] = blk.astype(BF16)
        qat_ref[other:other + A_HEAD_DIM, :] = jnp.zeros((A_HEAD_DIM, TM), BF16)

    ka = _nt(wka_ref[...], h)
    qm = _nt(wqm_ref[...], h)

    gq = col(COL_GQB, LANES)
    for hh in range(B_HEADS):
        blk = _rms_rows(qb[hh * LANES:(hh + 1) * LANES], gq, n=B_QK)
        blk = _rope_rows(blk, B_NOPE, half_b, cos_b, sin_b) * (B_QK ** -0.5 * LOG2E)
        qbt_ref[hh * LANES:(hh + 1) * LANES, :] = blk.astype(BF16)
    gk = col(COL_GKB, LANES)
    pad = jnp.zeros((LANES - B_QK, TM), F32)
    for hh in range(B_HEADS):
        blk = jnp.concatenate([kn[hh * B_NOPE:(hh + 1) * B_NOPE], kr, pad], axis=0)
        blk = _rope_rows(_rms_rows(blk, gk, n=B_QK), B_NOPE, half_b, cos_b, sin_b)
        kb_ref[:, hh * LANES:(hh + 1) * LANES] = _token_major([blk])

    qi = _nt(wqi_ref[...], h)
    ki = _nt(wki_ref[...], h)
    wt_ref[...] = _nt(wwi_ref[...], h)[0:IDX_HEADS] * (IDX_HEADS ** -0.5)

    gk = col(COL_GKA, A_HEAD_DIM)
    for c in range(A_HEADS // 2):
        pair = [_rope_rows(_rms_rows(ka[hh * A_HEAD_DIM:(hh + 1) * A_HEAD_DIM], gk), 0, half_a, cos_a, sin_a)
                for hh in (2 * c, 2 * c + 1)]
        ka_ref[c, :, :] = _token_major(pair)
    gm = col(COL_GQM, M_HEAD_DIM)
    for hh in range(M_HEADS):
        blk = _rms_rows(qm[hh * M_HEAD_DIM:(hh + 1) * M_HEAD_DIM], gm) * (M_HEAD_DIM ** -0.5 * LOG2E)
        qmt_ref[hh * M_HEAD_DIM:(hh + 1) * M_HEAD_DIM, :] = blk.astype(BF16)

    vbt_ref[...] = _nt(wuv_ref[...], ckv).astype(BF16)
    vat_ref[...] = _nt(wva_ref[...], h).astype(BF16)

    for hh in range(IDX_HEADS):
        blk = _rope_rows(qi[hh * IDX_DIM:(hh + 1) * IDX_DIM], 0, half_a, cos_a, sin_a)
        blk = (blk * (IDX_DIM ** -0.5)).astype(BF16)
        for j in range(TM // TQ):
            qib_ref[j, 0:IDX_DIM, hh * TQ:(hh + 1) * TQ] = blk[:, j * TQ:(j + 1) * TQ]
    qib_ref[:, IDX_DIM:, :] = jnp.zeros((TM // TQ, LANES - IDX_DIM, IDX_HEADS * TQ), BF16)
    ki_ref[...] = _token_major([_rope_rows(ki, 0, half_a, cos_a, sin_a)])


def _mem_kv(mem_ref, gains_ref, wk_ref, wvt_ref, km_ref, vmt_ref):
    hm = _rms_lanes(mem_ref[...], gains_ref[0:1, :]).astype(BF16)
    k = _mm(hm, wk_ref[...])
    gk = gains_ref[1:2, 0:M_HEAD_DIM]
    for hh in range(M_HEADS):
        kc = _rms_lanes(k[:, hh * M_HEAD_DIM:(hh + 1) * M_HEAD_DIM], gk)
        km_ref[:, hh * M_HEAD_DIM:(hh + 1) * M_HEAD_DIM] = kc.astype(BF16)
    vmt_ref[...] = _nt(wvt_ref[...], hm).astype(BF16)


def _attend(nk, n_heads, dv, q_of, k_of, v_of, bias_of, bounded, s_ref, p_ref, ot_ref):
    nq = ot_ref.shape[1]
    chunks = [slice(c * KC, (c + 1) * KC) for c in range(nk // KC)]

    def scores(hh, q, c):
        s = _mm(k_of(hh, chunks[c]), q)
        b = bias_of(c)
        return s if b is None else s + b

    depth = max(1, min(n_heads - 1, 8 // len(chunks)))
    slots = depth + 1

    def p_rows(hh, sl):
        base = (hh % slots) * nk
        return slice(base + sl.start, base + sl.stop)

    def probabilities(hh):
        q = q_of(hh)
        if bounded:
            l8 = jnp.zeros((8, nq), F32)
            for c in range(len(chunks)):
                p = jnp.exp2(scores(hh, q, c))
                l8 = l8 + p.reshape(KC // 8, 8, nq).sum(axis=0)
                p_ref[p_rows(hh, chunks[c]), :] = p.astype(BF16)
            return jnp.sum(l8, axis=0, keepdims=True)
        m = jnp.full((1, nq), -jnp.inf, F32)
        for c in range(len(chunks)):
            s = scores(hh, q, c)
            s_ref[chunks[c], :] = s
            m = jnp.maximum(m, jnp.max(s, axis=0, keepdims=True))
        l = jnp.zeros((1, nq), F32)
        for c in range(len(chunks)):
            p = jnp.exp2(s_ref[chunks[c], :] - m)
            l = l + jnp.sum(p, axis=0, keepdims=True)
            p_ref[p_rows(hh, chunks[c]), :] = p.astype(BF16)
        return l

    def weighted_values(hh, l):
        o = _mm(v_of(hh, slice(0, nk)), p_ref[p_rows(hh, slice(0, nk)), :])
        ot_ref[hh * dv:(hh + 1) * dv, :] = o / l

    sums = {}
    for step in range(n_heads + depth):
        if step < n_heads:
            sums[step] = probabilities(step)
        if step >= depth:
            weighted_values(step - depth, sums.pop(step - depth))


def _count(score_ref, nk, pred):
    cnt = jnp.zeros((COUNT_ROWS, TQ), I32)
    for r in range(0, nk, COUNT_ROWS):
        cnt = jnp.where(pred(score_ref[r:r + COUNT_ROWS, :]), cnt + 1, cnt)
    return jnp.sum(cnt, axis=0, keepdims=True)


def _ordered_to_bits(u, magnitude_mask):
    k = u ^ INT_MIN
    return k ^ ((k >> 31) & magnitude_mask)


def _ordered_pattern_to_float(u):
    return pltpu.bitcast(_ordered_to_bits(u, 0x7FFFFFFF), F32)


def _count_rounded(round_ref, nk, cand):
    assert nk // COUNT_ROWS <= 256
    one, zero = jnp.ones((), BF16), jnp.zeros((), BF16)
    cnt = jnp.zeros((COUNT_ROWS, TQ), BF16)
    for r in range(0, nk, COUNT_ROWS):
        cnt = cnt + jnp.where(round_ref[r:r + COUNT_ROWS, :] >= cand, one, zero)
    return jnp.sum(cnt.astype(F32), axis=0, keepdims=True)


def _select_topk(nk, q_pos, row, chunks, qib_ref, wt_ref, ki_ref, score_ref, round_ref, emit):
    for c, sl in enumerate(chunks):
        ki_c = ki_ref[sl, :]
        acc = jnp.zeros((KC, TQ), F32)
        for hh in range(IDX_HEADS):
            d = _mm(ki_c, qib_ref[0, :, hh * TQ:(hh + 1) * TQ])
            acc = acc + jnp.maximum(d, 0.0) * wt_ref[hh:hh + 1, :]
        score = jnp.where(row + c * KC <= q_pos, acc, NEG)
        score_ref[sl, :] = score
        round_ref[sl, :] = score.astype(BF16)

    def coarse(i, c_u):
        cand_u = c_u | jnp.left_shift(jnp.int32(1), 31 - i)
        cand = pltpu.bitcast(_ordered_to_bits(cand_u, 0x7FFF0000), F32).astype(BF16)
        return jnp.where(_count_rounded(round_ref, nk, cand) >= TOPK_MAX, cand_u, c_u)

    c_u = lax.fori_loop(0, 16, coarse, jnp.zeros((1, TQ), I32))
    pred_bits = _ordered_to_bits(c_u - (1 << 16), 0x7FFF0000)
    base_u = (pred_bits ^ ((pred_bits >> 31) & 0x7FFFFFFF)) ^ INT_MIN

    def fine(i, carry):
        off, cnt_t = carry
        cand_off = off | jnp.left_shift(jnp.int32(1), 16 - i)
        cand = _ordered_pattern_to_float(base_u + cand_off)
        cnt = _count(score_ref, nk, lambda x: x >= cand)
        ok = cnt >= TOPK_MAX
        return jnp.where(ok, cand_off, off), jnp.where(ok, cnt, cnt_t)

    off, cnt_t = lax.fori_loop(0, 17, fine, (jnp.zeros((1, TQ), I32), jnp.full((1, TQ), nk, I32)))
    thr = _ordered_pattern_to_float(base_u + off)
    split_ties = jnp.max(jnp.where(cnt_t > TOPK_MAX, 1, 0)) > 0

    @pl.when(jnp.logical_not(split_ties))
    def _():
        for c, sl in enumerate(chunks):
            emit(sl, (score_ref[sl, :] >= thr) & (row + c * KC <= q_pos))

    @pl.when(split_ties)
    def _():
        room = (TOPK_MAX - _count(score_ref, nk, lambda x: x > thr)).astype(F32)
        tri = lax.broadcasted_iota(I32, (KC, KC), 0) >= lax.broadcasted_iota(I32, (KC, KC), 1)
        tri = jnp.where(tri, 1.0, 0.0).astype(BF16)
        running = jnp.zeros((1, TQ), F32)
        for c, sl in enumerate(chunks):
            x = score_ref[sl, :]
            tie = x == thr
            rank = _mm(tri, jnp.where(tie, 1.0, 0.0).astype(BF16)) + running
            running = rank[KC - 1:KC, :]
            emit(sl, ((x > thr) | (tie & (rank <= room))) & (row + c * KC <= q_pos))


def _dsa_body(nk, start, bounded, qat_ref, qib_ref, wt_ref, ka_ref, vat_ref, ki_ref, oa_ref,
              score_ref, round_ref, bias_ref, s_ref, p_ref, ot_ref):
    q_pos = start + lax.broadcasted_iota(I32, (1, TQ), 1)
    row = lax.broadcasted_iota(I32, (KC, TQ), 0)
    chunks = [slice(c * KC, (c + 1) * KC) for c in range(nk // KC)]

    def emit_bias(sl, keep):
        bias_ref[sl, :] = jnp.where(keep, 0.0, NEG)

    if nk <= TOPK_MAX:
        for c, sl in enumerate(chunks):
            emit_bias(sl, row + c * KC <= q_pos)
    else:
        _select_topk(nk, q_pos, row, chunks, qib_ref, wt_ref, ki_ref, score_ref, round_ref, emit_bias)

    def q_of(hh):
        return qat_ref[hh * LANES:(hh + 1) * LANES, :]

    def k_of(hh, sl):
        return ka_ref[hh // 2, sl, :]

    def v_of(hh, sl):
        return vat_ref[hh * A_HEAD_DIM:(hh + 1) * A_HEAD_DIM, sl]

    _attend(nk, A_HEADS, A_HEAD_DIM, q_of, k_of, v_of, lambda c: bias_ref[chunks[c], :], bounded,
            s_ref, p_ref, ot_ref)
    oa_ref[...] = ot_ref[...].T


def _dsa_kernel(qat_ref, qib_ref, wt_ref, ka_ref, vat_ref, ki_ref, oa_ref,
                score_ref, round_ref, bias_ref, s_ref, p_ref, ot_ref, *, seq, bounded):
    for cls in range(seq // TQ):
        cols = slice(cls * TQ, (cls + 1) * TQ)
        _dsa_body(TQ * (cls + 1), cls * TQ, bounded, qat_ref.at[:, cols], qib_ref.at[pl.ds(cls, 1)],
                  wt_ref.at[:, cols], ka_ref, vat_ref, ki_ref, oa_ref.at[cols, :],
                  score_ref, round_ref, bias_ref, s_ref, p_ref, ot_ref)


def _mla_body(nk, start, bounded, qbt_ref, kb_ref, vbt_ref, ob_ref, bias_ref, s_ref, p_ref, ot_ref):
    last = nk // KC - 1
    q_pos = start + lax.broadcasted_iota(I32, (1, TQ), 1)
    row = lax.broadcasted_iota(I32, (KC, TQ), 0)
    bias_ref[0:KC, :] = jnp.where(row + last * KC <= q_pos, 0.0, NEG)

    def q_of(hh):
        return qbt_ref[hh * LANES:(hh + 1) * LANES, :]

    def k_of(hh, sl):
        return kb_ref[sl, hh * LANES:(hh + 1) * LANES]

    def v_of(hh, sl):
        return vbt_ref[hh * B_VDIM:(hh + 1) * B_VDIM, sl]

    _attend(nk, B_HEADS, B_VDIM, q_of, k_of, v_of, lambda c: bias_ref[0:KC, :] if c == last else None,
            bounded, s_ref, p_ref, ot_ref)
    ob_ref[...] = ot_ref[...].T


def _mla_kernel(qbt_ref, kb_ref, vbt_ref, ob_ref, bias_ref, s_ref, p_ref, ot_ref, *, seq, bounded):
    for cls in range(seq // TQ):
        cols = slice(cls * TQ, (cls + 1) * TQ)
        _mla_body(TQ * (cls + 1), cls * TQ, bounded, qbt_ref.at[:, cols], kb_ref, vbt_ref, ob_ref.at[cols, :],
                  bias_ref.at[cls], s_ref, p_ref.at[cls % 2], ot_ref.at[cls])


def _mem_attn_kernel(qmt_ref, mem_ref, gains_ref, wk_ref, wvt_ref, om_ref, km_ref, vmt_ref, s_ref, p_ref, ot_ref,
                     *, mem_len, bounded):
    _mem_kv(mem_ref, gains_ref, wk_ref, wvt_ref, km_ref, vmt_ref)

    def k_of(hh, sl):
        return km_ref[sl, hh * M_HEAD_DIM:(hh + 1) * M_HEAD_DIM]

    def v_of(hh, sl):
        return vmt_ref[hh * M_HEAD_DIM:(hh + 1) * M_HEAD_DIM, sl]

    for blk in range(qmt_ref.shape[1] // TQ_MEM):
        cols = slice(blk * TQ_MEM, (blk + 1) * TQ_MEM)

        def q_of(hh):
            return qmt_ref[hh * M_HEAD_DIM:(hh + 1) * M_HEAD_DIM, cols]

        _attend(mem_len, M_HEADS, M_HEAD_DIM, q_of, k_of, v_of, lambda c: None, bounded, s_ref,
                p_ref.at[blk % 2], ot_ref.at[blk % 2])
        om_ref[cols, :] = ot_ref[blk % 2].T


def _final_kernel(x_ref, oa_ref, ob_ref, om_ref, rows_ref, wz_ref, wg_ref, wb_ref, wo_ref, out_ref):
    x = x_ref[...]
    h = _rms_lanes(x, rows_ref[ROW_GN:ROW_GN + 1, :]).astype(BF16)
    zs = [_mm(h, wz_ref[n]) for n in range(N_BRANCH)]
    gate_logits = [_mm(h, wg_ref[:, n * D_MODEL:(n + 1) * D_MODEL]) for n in range(N_BRANCH)]
    merged = jnp.zeros((TM, D_MODEL), F32)
    for n, o_ref in enumerate((oa_ref, ob_ref, om_ref)):
        y = (o_ref[...] * (zs[n] * jax.nn.sigmoid(zs[n]))).astype(BF16)
        branch = _mm(y, wb_ref[n])
        merged = merged + jax.nn.sigmoid(gate_logits[n]) * branch
    out_ref[...] = x + _mm(merged.astype(BF16), wo_ref[...])


def _full(shape):
    return pl.BlockSpec(shape, lambda *_: (0,) * len(shape), pipeline_mode=pl.Buffered(1))


def _params(n_axes):
    return pltpu.CompilerParams(dimension_semantics=("arbitrary",) * n_axes,
                                vmem_limit_bytes=VMEM_LIMIT)


def kernel(x, mem, positions, g_norm, w_in, g_qn_a, g_kn_a, g_cq, g_ckv, w_uq, w_ukv, g_qn_b, g_kn_b,
           g_mem, w_mem_kv, g_qn_m, g_kn_m, w_branch, w_out):
    b, s, d = x.shape
    m_len = mem.shape[1]
    n = b * s
    nq = s // TQ
    assert d == D_MODEL and s % TQ == 0 and TQ == KC and TM % TQ == 0 and n % TM == 0 and m_len % KC == 0
    assert g_norm.shape[0] == 1, "single-layer block"

    w = w_in[0]
    off = np.cumsum([0, 512, 512, 512, 512, IDX_DIM, IDX_HEADS, BRANCH_WIDTH, B_Q_RANK, B_KV_RANK, B_ROPE,
                     BRANCH_WIDTH, M_HEADS * M_HEAD_DIM, BRANCH_WIDTH, N_BRANCH * D_MODEL])
    off = tuple(int(o) for o in off)
    assert w.shape == (d, off[14]) and all(o % 16 == 0 for o in off[:5]) and all(o % 8 == 0 for o in off)
    assert IDX_HEADS <= 16 and off[5] + 16 <= off[14] and d % W_COLS == 0
    bf = lambda a: a.astype(BF16)
    t_block = lambda rows: pl.BlockSpec((rows, W_COLS), lambda i: (0, i))
    r_block = lambda width: pl.BlockSpec((W_COLS, width), lambda i: (i, 0))
    t_shape = lambda rows: jax.ShapeDtypeStruct((rows, d), BF16)
    r_shape = lambda width: jax.ShapeDtypeStruct((d, width), BF16)
    wqa_t, wka_t, wva_t, wqi_t, wqm_t, wki_t, wkr_t, wwi_t, wcq, wckv, wz, wg = pl.pallas_call(
        functools.partial(_weights_kernel, off=off),
        grid=(d // W_COLS,),
        in_specs=[pl.BlockSpec((off[14], W_COLS), lambda i: (0, i))],
        out_specs=[t_block(512), t_block(512), t_block(512), t_block(512), t_block(M_HEADS * M_HEAD_DIM),
                   t_block(IDX_DIM), t_block(B_ROPE), t_block(16), r_block(B_Q_RANK), r_block(B_KV_RANK),
                   pl.BlockSpec((N_BRANCH, W_COLS, BRANCH_WIDTH), lambda i: (0, i, 0)), r_block(N_BRANCH * D_MODEL)],
        out_shape=[t_shape(512), t_shape(512), t_shape(512), t_shape(512), t_shape(M_HEADS * M_HEAD_DIM),
                   t_shape(IDX_DIM), t_shape(B_ROPE), t_shape(16), r_shape(B_Q_RANK), r_shape(B_KV_RANK),
                   jax.ShapeDtypeStruct((N_BRANCH, d, BRANCH_WIDTH), BF16), r_shape(N_BRANCH * D_MODEL)],
        compiler_params=_params(1), name="weights",
    )(w.T)
    wuq_t = bf(jnp.pad(w_uq[0].reshape(B_Q_RANK, B_HEADS, B_QK), ((0, 0), (0, 0), (0, LANES - B_QK)))
               .reshape(B_Q_RANK, B_HEADS * LANES).T)
    ukv = w_ukv[0].reshape(B_KV_RANK, B_HEADS, B_NOPE + B_VDIM)
    wuk_t = bf(ukv[:, :, :B_NOPE].reshape(B_KV_RANK, B_HEADS * B_NOPE).T)
    wuv_t = bf(ukv[:, :, B_NOPE:].reshape(B_KV_RANK, B_HEADS * B_VDIM).T)
    wmk = bf(w_mem_kv[0][:, :M_HEADS * M_HEAD_DIM])
    wmv_t = bf(w_mem_kv[0][:, M_HEADS * M_HEAD_DIM:].T)
    wb = bf(w_branch[0])
    wo = bf(w_out[0])

    pad_to = lambda v, size: jnp.pad(v, (0, size - v.shape[0]))
    inv_a = ROPE_THETA ** (-(jnp.arange(0, A_ROT, 2, dtype=F32) / A_ROT))
    inv_b = ROPE_THETA ** (-(jnp.arange(0, B_ROPE, 2, dtype=F32) / B_ROPE))
    col_vectors = [None] * N_COLS
    col_vectors[COL_GQA], col_vectors[COL_GKA] = g_qn_a[0], g_kn_a[0]
    col_vectors[COL_GQB], col_vectors[COL_GKB] = g_qn_b[0], g_kn_b[0]
    col_vectors[COL_GQM], col_vectors[COL_INVA], col_vectors[COL_INVB] = g_qn_m[0], inv_a, inv_b
    cols = jnp.stack([pad_to(v, LANES) for v in col_vectors], axis=1)
    row_vectors = [None] * N_ROWS
    row_vectors[ROW_GN], row_vectors[ROW_GCQ], row_vectors[ROW_GCKV] = g_norm[0], g_cq[0], g_ckv[0]
    rows = jnp.stack([pad_to(v, D_MODEL) for v in row_vectors])
    mem_gains = jnp.stack([g_mem[0], pad_to(g_kn_m[0], D_MODEL)])

    x2 = x.reshape(n, d)
    pos_r = positions.reshape(1, n)
    tile = lambda width: pl.BlockSpec((TM, width), lambda i: (i, 0))
    tile_t = lambda rows: pl.BlockSpec((rows, TM), lambda i: (0, i))
    pos_spec = pl.BlockSpec((1, TM), lambda i: (0, i))

    a_w = [wqa_t, wqi_t, wwi_t, wka_t, wva_t, wki_t]
    b_w = [wcq, wckv, wkr_t, wqm_t, wuq_t, wuk_t, wuv_t]
    qat, qib, wt, ka, vat, ki, qbt, kb, vbt, qmt = pl.pallas_call(
        _proj_kernel,
        grid=(n // TM,),
        in_specs=[tile(d), pos_spec, _full(rows.shape), _full(cols.shape)] + [_full(a.shape) for a in a_w + b_w],
        out_specs=[tile_t(A_HEADS * LANES), pl.BlockSpec((TM // TQ, LANES, IDX_HEADS * TQ), lambda i: (i, 0, 0)),
                   tile_t(IDX_HEADS), pl.BlockSpec((512 // LANES, TM, LANES), lambda i: (0, i, 0)),
                   tile_t(512), tile(LANES),
                   tile_t(B_HEADS * LANES), tile(B_HEADS * LANES), tile_t(512), tile_t(512)],
        out_shape=[jax.ShapeDtypeStruct((A_HEADS * LANES, n), BF16),
                   jax.ShapeDtypeStruct((n // TQ, LANES, IDX_HEADS * TQ), BF16),
                   jax.ShapeDtypeStruct((IDX_HEADS, n), F32),
                   jax.ShapeDtypeStruct((512 // LANES, n, LANES), BF16),
                   jax.ShapeDtypeStruct((512, n), BF16),
                   jax.ShapeDtypeStruct((n, LANES), BF16),
                   jax.ShapeDtypeStruct((B_HEADS * LANES, n), BF16),
                   jax.ShapeDtypeStruct((n, B_HEADS * LANES), BF16),
                   jax.ShapeDtypeStruct((512, n), BF16),
                   jax.ShapeDtypeStruct((512, n), BF16)],
        compiler_params=_params(1), name="proj",
    )(x2, pos_r, rows, cols, *a_w, *b_w)

    o_shape = jax.ShapeDtypeStruct((n, 512), F32)

    def dsa(bounded):
        return pl.pallas_call(
            functools.partial(_dsa_kernel, seq=s, bounded=bounded),
            grid=(b,),
            in_specs=[pl.BlockSpec((A_HEADS * LANES, s), lambda bi: (0, bi)),
                      pl.BlockSpec((nq, LANES, IDX_HEADS * TQ), lambda bi: (bi, 0, 0)),
                      pl.BlockSpec((IDX_HEADS, s), lambda bi: (0, bi)),
                      pl.BlockSpec((512 // LANES, s, LANES), lambda bi: (0, bi, 0)),
                      pl.BlockSpec((512, s), lambda bi: (0, bi)), pl.BlockSpec((s, LANES), lambda bi: (bi, 0))],
            out_specs=pl.BlockSpec((s, 512), lambda bi: (bi, 0)), out_shape=o_shape,
            scratch_shapes=[pltpu.VMEM((s, TQ), F32), pltpu.VMEM((s, TQ), BF16), pltpu.VMEM((s, TQ), F32),
                            pltpu.VMEM((s, TQ), F32),
                            pltpu.VMEM((2 * s, TQ), BF16), pltpu.VMEM((512, TQ), F32)],
            compiler_params=_params(1), name="dsa" if bounded else "dsa_general")

    def mla(bounded):
        return pl.pallas_call(
            functools.partial(_mla_kernel, seq=s, bounded=bounded),
            grid=(b,),
            in_specs=[pl.BlockSpec((B_HEADS * LANES, s), lambda bi: (0, bi)),
                      pl.BlockSpec((s, B_HEADS * LANES), lambda bi: (bi, 0)),
                      pl.BlockSpec((512, s), lambda bi: (0, bi))],
            out_specs=pl.BlockSpec((s, 512), lambda bi: (bi, 0)), out_shape=o_shape,
            scratch_shapes=[pltpu.VMEM((nq, KC, TQ), F32), pltpu.VMEM((s, TQ), F32),
                            pltpu.VMEM((2, 2 * s, TQ), BF16), pltpu.VMEM((nq, 512, TQ), F32)],
            compiler_params=_params(1), name="mla" if bounded else "mla_general")

    def mem_attn(bounded):
        return pl.pallas_call(
            functools.partial(_mem_attn_kernel, mem_len=m_len, bounded=bounded),
            grid=(b,),
            in_specs=[pl.BlockSpec((512, s), lambda bi: (0, bi)), pl.BlockSpec((m_len, d), lambda bi: (bi, 0)),
                      _full(mem_gains.shape), _full(wmk.shape), _full(wmv_t.shape)],
            out_specs=pl.BlockSpec((s, 512), lambda bi: (bi, 0)), out_shape=o_shape,
            scratch_shapes=[pltpu.VMEM((m_len, 512), BF16), pltpu.VMEM((512, m_len), BF16),
                            pltpu.VMEM((m_len, TQ_MEM), F32), pltpu.VMEM((2, M_HEADS * m_len, TQ_MEM), BF16),
                            pltpu.VMEM((2, 512, TQ_MEM), F32)],
            compiler_params=_params(1), name="mem_attn" if bounded else "mem_attn_general")

    gain_max = jnp.max(jnp.abs(cols), axis=0)

    def score_bound(cq_, ck_, dim):
        return dim ** 0.5 * LOG2E * 1.02 * gain_max[cq_] * gain_max[ck_]

    def attention(bounded):
        def run(*ops):
            return (dsa(bounded)(*ops[:6]), mla(bounded)(*ops[6:9]), mem_attn(bounded)(*ops[9:]))
        return run

    k_mem_gain = jnp.max(jnp.abs(g_kn_m[0]))
    worst = jnp.maximum(jnp.maximum(score_bound(COL_GQA, COL_GKA, A_HEAD_DIM), score_bound(COL_GQB, COL_GKB, B_QK)),
                        M_HEAD_DIM ** 0.5 * LOG2E * 1.02 * gain_max[COL_GQM] * k_mem_gain)
    oa, ob, om = lax.cond(worst <= BOUNDED_SCORE_LIMIT, attention(True), attention(False),
                          qat, qib, wt, ka, vat, ki, qbt, kb, vbt, qmt, mem.reshape(b * m_len, d), mem_gains,
                          wmk, wmv_t)

    out = pl.pallas_call(
        _final_kernel,
        grid=(n // TM,),
        in_specs=[tile(d), tile(512), tile(512), tile(512), _full(rows.shape), _full(wz.shape), _full(wg.shape),
                  _full(wb.shape), _full(wo.shape)],
        out_specs=tile(d), out_shape=jax.ShapeDtypeStruct((n, d), x.dtype),
        compiler_params=_params(1), name="final",
    )(x2, oa, ob, om, rows, wz, wg, wb, wo)
    return out.reshape(b, s, d)
```

```python
import functools

import numpy as np
import jax
import jax.numpy as jnp
from jax import lax
from jax.experimental import pallas as pl
from jax.experimental.pallas import tpu as pltpu

F32 = jnp.float32
BF16 = jnp.bfloat16
I32 = jnp.int32

D_MODEL = 1024
ROPE_THETA = 500000.0
EPS = 1e-6
NEG = -1e30
N_BRANCH = 3
BRANCH_WIDTH = 512
A_HEADS = 8
A_HEAD_DIM = 64
A_ROT = A_HEAD_DIM // 4
IDX_HEADS = 8
IDX_DIM = 64
TOPK_MAX = 256
B_HEADS = 8
B_NOPE = 64
B_ROPE = 32
B_VDIM = 64
B_QK = B_NOPE + B_ROPE
B_Q_RANK = 384
B_KV_RANK = 256
M_HEADS = 4
M_HEAD_DIM = 128

LANES = 128
TM = 512
TQ = 256
TQ_MEM = 512
KC = 256
COUNT_ROWS = 64
VMEM_LIMIT = 56 * 1024 * 1024
INT_MIN = -2 ** 31
LOG2E = 1.4426950408889634
BOUNDED_SCORE_LIMIT = 32.0


def _nt(a, b):
    return lax.dot_general(a, b, (((1,), (1,)), ((), ())), preferred_element_type=F32)


def _mm(a, b):
    return jnp.dot(a, b, preferred_element_type=F32)


def _rms_lanes(xf, g_row, n=None):
    n = xf.shape[-1] if n is None else n
    ms = jnp.sum(xf * xf, axis=-1, keepdims=True) / n
    return xf * lax.rsqrt(ms + EPS) * g_row


def _rms_rows(blk, g_col, n=None):
    n = blk.shape[0] if n is None else n
    ms = jnp.sum(blk * blk, axis=0, keepdims=True) / n
    return blk * lax.rsqrt(ms + EPS) * g_col


def _rope_rows(blk, lo, half, cos_t, sin_t):
    x1 = blk[lo:lo + half]
    x2 = blk[lo + half:lo + 2 * half]
    parts = []
    if lo:
        parts.append(blk[:lo])
    parts += [x1 * cos_t - x2 * sin_t, x2 * cos_t + x1 * sin_t]
    if lo + 2 * half < blk.shape[0]:
        parts.append(blk[lo + 2 * half:])
    return jnp.concatenate(parts, axis=0)


def _token_major(blocks):
    rows = sum(blk.shape[0] for blk in blocks)
    if rows < LANES:
        blocks = list(blocks) + [jnp.zeros((LANES - rows, blocks[0].shape[1]), F32)]
    return jnp.concatenate(blocks, axis=0).T.astype(BF16)


W_COLS = 128


def _weights_kernel(wt_ref, wm_ref, wqa_ref, wka_ref, wva_ref, wqi_ref, wqm_ref, wki_ref, wkr_ref, wwi_ref,
                    wcq_ref, wckv_ref, wz_ref, wg_ref, wmk_ref, wmv_ref, *, off):
    n_mk = wmk_ref.shape[1]
    wmk_ref[...] = wm_ref[:, :n_mk].astype(BF16)
    wmv_ref[...] = wm_ref[:, n_mk:].T.astype(BF16)
    seg = lambda i: wt_ref[off[i]:off[i + 1], :]
    for i, o_ref in ((0, wqa_ref), (1, wka_ref), (2, wva_ref), (3, wqi_ref), (11, wqm_ref), (4, wki_ref),
                     (9, wkr_ref)):
        o_ref[...] = seg(i).astype(BF16)
    head_row = lax.broadcasted_iota(I32, (16, W_COLS), 0)
    wwi_ref[...] = jnp.where(head_row < IDX_HEADS, wt_ref[off[5]:off[5] + 16, :], 0.0).astype(BF16)
    wcq_ref[...] = seg(7).T.astype(BF16)
    wckv_ref[...] = seg(8).T.astype(BF16)
    for n, i in enumerate((6, 10, 12)):
        wz_ref[n] = seg(i).T.astype(BF16)
    wg_ref[...] = seg(13).T.astype(BF16)


COL_GQA, COL_GKA, COL_GQB, COL_GKB, COL_GQM, COL_INVA, COL_INVB, N_COLS = 0, 1, 2, 3, 4, 5, 6, 7
ROW_GN, ROW_GCQ, ROW_GCKV, N_ROWS = 0, 1, 2, 3


def _proj_kernel(x_ref, posr_ref, rows_ref, cols_ref,
                 wqa_ref, wqi_ref, wwi_ref, wka_ref, wva_ref, wki_ref,
                 wcq_ref, wckv_ref, wkr_ref, wqm_ref, wuq_ref, wuk_ref, wuv_ref,
                 qat_ref, qib_ref, wt_ref, ka_ref, vat_ref, ki_ref, qbt_ref, kb_ref, vbt_ref, qmt_ref):
    rows, cols = rows_ref[...], cols_ref[...]
    col = lambda j, n: cols[0:n, j:j + 1]
    h = _rms_lanes(x_ref[...], rows[ROW_GN:ROW_GN + 1, :]).astype(BF16)
    pos = posr_ref[...].astype(F32)
    half_a, half_b = A_ROT // 2, B_ROPE // 2
    ang_a = col(COL_INVA, half_a) * pos
    cos_a, sin_a = jnp.cos(ang_a), jnp.sin(ang_a)
    ang_b = col(COL_INVB, half_b) * pos
    cos_b, sin_b = jnp.cos(ang_b), jnp.sin(ang_b)

    cq = _mm(h, wcq_ref[...])
    ckv = _mm(h, wckv_ref[...])
    qa = _nt(wqa_ref[...], h)
    cq = _rms_lanes(cq, rows[ROW_GCQ:ROW_GCQ + 1, 0:B_Q_RANK]).astype(BF16)
    ckv = _rms_lanes(ckv, rows[ROW_GCKV:ROW_GCKV + 1, 0:B_KV_RANK]).astype(BF16)
    qb = _nt(wuq_ref[...], cq)
    kn = _nt(wuk_ref[...], ckv)
    kr = _nt(wkr_ref[...], h)

    gq = col(COL_GQA, A_HEAD_DIM)
    for hh in range(A_HEADS):
        blk = _rms_rows(qa[hh * A_HEAD_DIM:(hh + 1) * A_HEAD_DIM], gq)
        blk = _rope_rows(blk, 0, half_a, cos_a, sin_a) * (A_HEAD_DIM ** -0.5 * LOG2E)
        own = hh * LANES + (hh % 2) * A_HEAD_DIM
        other = hh * LANES + (1 - hh % 2) * A_HEAD_DIM
        qat_ref[own:own + A_HEAD_DIM, :] = blk.astype(BF16)
        qat_ref[other:other + A_HEAD_DIM, :] = jnp.zeros((A_HEAD_DIM, TM), BF16)

    ka = _nt(wka_ref[...], h)
    qm = _nt(wqm_ref[...], h)

    gq = col(COL_GQB, LANES)
    for hh in range(B_HEADS):
        blk = _rms_rows(qb[hh * LANES:(hh + 1) * LANES], gq, n=B_QK)
        blk = _rope_rows(blk, B_NOPE, half_b, cos_b, sin_b) * (B_QK ** -0.5 * LOG2E)
        qbt_ref[hh * LANES:(hh + 1) * LANES, :] = blk.astype(BF16)
    gk = col(COL_GKB, LANES)
    pad = jnp.zeros((LANES - B_QK, TM), F32)
    for hh in range(B_HEADS):
        blk = jnp.concatenate([kn[hh * B_NOPE:(hh + 1) * B_NOPE], kr, pad], axis=0)
        blk = _rope_rows(_rms_rows(blk, gk, n=B_QK), B_NOPE, half_b, cos_b, sin_b)
        kb_ref[:, hh * LANES:(hh + 1) * LANES] = _token_major([blk])

    qi = _nt(wqi_ref[...], h)
    ki = _nt(wki_ref[...], h)
    wt_ref[...] = _nt(wwi_ref[...], h)[0:IDX_HEADS] * (IDX_HEADS ** -0.5)

    gk = col(COL_GKA, A_HEAD_DIM)
    for c in range(A_HEADS // 2):
        pair = [_rope_rows(_rms_rows(ka[hh * A_HEAD_DIM:(hh + 1) * A_HEAD_DIM], gk), 0, half_a, cos_a, sin_a)
                for hh in (2 * c, 2 * c + 1)]
        ka_ref[c, :, :] = _token_major(pair)
    gm = col(COL_GQM, M_HEAD_DIM)
    for hh in range(M_HEADS):
        blk = _rms_rows(qm[hh * M_HEAD_DIM:(hh + 1) * M_HEAD_DIM], gm) * (M_HEAD_DIM ** -0.5 * LOG2E)
        qmt_ref[hh * M_HEAD_DIM:(hh + 1) * M_HEAD_DIM, :] = blk.astype(BF16)

    vbt_ref[...] = _nt(wuv_ref[...], ckv).astype(BF16)
    vat_ref[...] = _nt(wva_ref[...], h).astype(BF16)

    for hh in range(IDX_HEADS):
        blk = _rope_rows(qi[hh * IDX_DIM:(hh + 1) * IDX_DIM], 0, half_a, cos_a, sin_a)
        blk = (blk * (IDX_DIM ** -0.5)).astype(BF16)
        for j in range(TM // TQ):
            qib_ref[j, 0:IDX_DIM, hh * TQ:(hh + 1) * TQ] = blk[:, j * TQ:(j + 1) * TQ]
    qib_ref[:, IDX_DIM:, :] = jnp.zeros((TM // TQ, LANES - IDX_DIM, IDX_HEADS * TQ), BF16)
    ki_ref[...] = _token_major([_rope_rows(ki, 0, half_a, cos_a, sin_a)])


def _mem_kv(mem_ref, gains_ref, wk_ref, wvt_ref, km_ref, vmt_ref):
    hm = _rms_lanes(mem_ref[...], gains_ref[0:1, :]).astype(BF16)
    k = _mm(hm, wk_ref[...])
    gk = gains_ref[1:2, 0:M_HEAD_DIM]
    for hh in range(M_HEADS):
        kc = _rms_lanes(k[:, hh * M_HEAD_DIM:(hh + 1) * M_HEAD_DIM], gk)
        km_ref[:, hh * M_HEAD_DIM:(hh + 1) * M_HEAD_DIM] = kc.astype(BF16)
    vmt_ref[...] = _nt(wvt_ref[...], hm).astype(BF16)


def _attend(nk, n_heads, dv, q_of, k_of, v_of, bias_of, bounded, s_ref, p_ref, ot_ref):
    nq = ot_ref.shape[1]
    chunks = [slice(c * KC, (c + 1) * KC) for c in range(nk // KC)]

    def scores(hh, q, c):
        s = _mm(k_of(hh, chunks[c]), q)
        b = bias_of(c)
        return s if b is None else s + b

    depth = max(1, min(n_heads - 1, 8 // len(chunks)))
    slots = depth + 1

    def p_rows(hh, sl):
        base = (hh % slots) * nk
        return slice(base + sl.start, base + sl.stop)

    def probabilities(hh):
        q = q_of(hh)
        if bounded:
            l8 = jnp.zeros((8, nq), F32)
            for c in range(len(chunks)):
                p = jnp.exp2(scores(hh, q, c))
                l8 = l8 + p.reshape(KC // 8, 8, nq).sum(axis=0)
                p_ref[p_rows(hh, chunks[c]), :] = p.astype(BF16)
            return jnp.sum(l8, axis=0, keepdims=True)
        m = jnp.full((1, nq), -jnp.inf, F32)
        for c in range(len(chunks)):
            s = scores(hh, q, c)
            s_ref[chunks[c], :] = s
            m = jnp.maximum(m, jnp.max(s, axis=0, keepdims=True))
        l = jnp.zeros((1, nq), F32)
        for c in range(len(chunks)):
            p = jnp.exp2(s_ref[chunks[c], :] - m)
            l = l + jnp.sum(p, axis=0, keepdims=True)
            p_ref[p_rows(hh, chunks[c]), :] = p.astype(BF16)
        return l

    def weighted_values(hh, l):
        o = _mm(v_of(hh, slice(0, nk)), p_ref[p_rows(hh, slice(0, nk)), :])
        ot_ref[hh * dv:(hh + 1) * dv, :] = o / l

    sums = {}
    for step in range(n_heads + depth):
        if step < n_heads:
            sums[step] = probabilities(step)
        if step >= depth:
            weighted_values(step - depth, sums.pop(step - depth))


def _count(score_ref, nk, pred):
    cnt = jnp.zeros((COUNT_ROWS, TQ), I32)
    for r in range(0, nk, COUNT_ROWS):
        cnt = jnp.where(pred(score_ref[r:r + COUNT_ROWS, :]), cnt + 1, cnt)
    return jnp.sum(cnt, axis=0, keepdims=True)


def _ordered_to_bits(u, magnitude_mask):
    k = u ^ INT_MIN
    return k ^ ((k >> 31) & magnitude_mask)


def _ordered_pattern_to_float(u):
    return pltpu.bitcast(_ordered_to_bits(u, 0x7FFFFFFF), F32)


def _count_rounded(round_ref, nk, cand):
    assert nk // COUNT_ROWS <= 256
    one, zero = jnp.ones((), BF16), jnp.zeros((), BF16)
    cnt = jnp.zeros((COUNT_ROWS, TQ), BF16)
    for r in range(0, nk, COUNT_ROWS):
        cnt = cnt + jnp.where(round_ref[r:r + COUNT_ROWS, :] >= cand, one, zero)
    return jnp.sum(cnt.astype(F32), axis=0, keepdims=True)


def _select_topk(nk, q_pos, row, chunks, qib_ref, wt_ref, ki_ref, score_ref, round_ref, emit):
    for c, sl in enumerate(chunks):
        ki_c = ki_ref[sl, :]
        acc = jnp.zeros((KC, TQ), F32)
        for hh in range(IDX_HEADS):
            d = _mm(ki_c, qib_ref[0, :, hh * TQ:(hh + 1) * TQ])
            acc = acc + jnp.maximum(d, 0.0) * wt_ref[hh:hh + 1, :]
        score = jnp.where(row + c * KC <= q_pos, acc, NEG)
        score_ref[sl, :] = score
        round_ref[sl, :] = score.astype(BF16)

    def coarse(i, c_u):
        cand_u = c_u | jnp.left_shift(jnp.int32(1), 31 - i)
        cand = pltpu.bitcast(_ordered_to_bits(cand_u, 0x7FFF0000), F32).astype(BF16)
        return jnp.where(_count_rounded(round_ref, nk, cand) >= TOPK_MAX, cand_u, c_u)

    c_u = lax.fori_loop(0, 16, coarse, jnp.zeros((1, TQ), I32))
    pred_bits = _ordered_to_bits(c_u - (1 << 16), 0x7FFF0000)
    base_u = (pred_bits ^ ((pred_bits >> 31) & 0x7FFFFFFF)) ^ INT_MIN

    def fine(i, carry):
        off, cnt_t = carry
        cand_off = off | jnp.left_shift(jnp.int32(1), 16 - i)
        cand = _ordered_pattern_to_float(base_u + cand_off)
        cnt = _count(score_ref, nk, lambda x: x >= cand)
        ok = cnt >= TOPK_MAX
        return jnp.where(ok, cand_off, off), jnp.where(ok, cnt, cnt_t)

    off, cnt_t = lax.fori_loop(0, 17, fine, (jnp.zeros((1, TQ), I32), jnp.full((1, TQ), nk, I32)))
    thr = _ordered_pattern_to_float(base_u + off)
    split_ties = jnp.max(jnp.where(cnt_t > TOPK_MAX, 1, 0)) > 0

    @pl.when(jnp.logical_not(split_ties))
    def _():
        for c, sl in enumerate(chunks):
            emit(sl, (score_ref[sl, :] >= thr) & (row + c * KC <= q_pos))

    @pl.when(split_ties)
    def _():
        room = (TOPK_MAX - _count(score_ref, nk, lambda x: x > thr)).astype(F32)
        tri = lax.broadcasted_iota(I32, (KC, KC), 0) >= lax.broadcasted_iota(I32, (KC, KC), 1)
        tri = jnp.where(tri, 1.0, 0.0).astype(BF16)
        running = jnp.zeros((1, TQ), F32)
        for c, sl in enumerate(chunks):
            x = score_ref[sl, :]
            tie = x == thr
            rank = _mm(tri, jnp.where(tie, 1.0, 0.0).astype(BF16)) + running
            running = rank[KC - 1:KC, :]
            emit(sl, ((x > thr) | (tie & (rank <= room))) & (row + c * KC <= q_pos))


def _dsa_body(nk, start, bounded, qat_ref, qib_ref, wt_ref, ka_ref, vat_ref, ki_ref, oa_ref,
              score_ref, round_ref, bias_ref, s_ref, p_ref, ot_ref):
    q_pos = start + lax.broadcasted_iota(I32, (1, TQ), 1)
    row = lax.broadcasted_iota(I32, (KC, TQ), 0)
    chunks = [slice(c * KC, (c + 1) * KC) for c in range(nk // KC)]

    def emit_bias(sl, keep):
        bias_ref[sl, :] = jnp.where(keep, 0.0, NEG)

    if nk <= TOPK_MAX:
        for c, sl in enumerate(chunks):
            emit_bias(sl, row + c * KC <= q_pos)
    else:
        _select_topk(nk, q_pos, row, chunks, qib_ref, wt_ref, ki_ref, score_ref, round_ref, emit_bias)

    def q_of(hh):
        return qat_ref[hh * LANES:(hh + 1) * LANES, :]

    def k_of(hh, sl):
        return ka_ref[hh // 2, sl, :]

    def v_of(hh, sl):
        return vat_ref[hh * A_HEAD_DIM:(hh + 1) * A_HEAD_DIM, sl]

    _attend(nk, A_HEADS, A_HEAD_DIM, q_of, k_of, v_of, lambda c: bias_ref[chunks[c], :], bounded,
            s_ref, p_ref, ot_ref)
    oa_ref[...] = ot_ref[...].T


def _dsa_kernel(qat_ref, qib_ref, wt_ref, ka_ref, vat_ref, ki_ref, after_ref, oa_ref,
                score_ref, round_ref, bias_ref, s_ref, p_ref, ot_ref, *, seq, bounded):
    del after_ref
    for cls in range(seq // TQ):
        cols = slice(cls * TQ, (cls + 1) * TQ)
        _dsa_body(TQ * (cls + 1), cls * TQ, bounded, qat_ref.at[:, cols], qib_ref.at[pl.ds(cls, 1)],
                  wt_ref.at[:, cols], ka_ref, vat_ref, ki_ref, oa_ref.at[cols, :],
                  score_ref, round_ref, bias_ref, s_ref, p_ref, ot_ref)


def _mla_body(nk, start, bounded, qbt_ref, kb_ref, vbt_ref, ob_ref, bias_ref, s_ref, p_ref, ot_ref):
    last = nk // KC - 1
    q_pos = start + lax.broadcasted_iota(I32, (1, TQ), 1)
    row = lax.broadcasted_iota(I32, (KC, TQ), 0)
    bias_ref[0:KC, :] = jnp.where(row + last * KC <= q_pos, 0.0, NEG)

    def q_of(hh):
        return qbt_ref[hh * LANES:(hh + 1) * LANES, :]

    def k_of(hh, sl):
        return kb_ref[sl, hh * LANES:(hh + 1) * LANES]

    def v_of(hh, sl):
        return vbt_ref[hh * B_VDIM:(hh + 1) * B_VDIM, sl]

    _attend(nk, B_HEADS, B_VDIM, q_of, k_of, v_of, lambda c: bias_ref[0:KC, :] if c == last else None,
            bounded, s_ref, p_ref, ot_ref)
    ob_ref[...] = ot_ref[...].T


def _mla_kernel(qbt_ref, kb_ref, vbt_ref, after_ref, ob_ref, bias_ref, s_ref, p_ref, ot_ref, *, seq, bounded):
    del after_ref
    for cls in range(seq // TQ):
        cols = slice(cls * TQ, (cls + 1) * TQ)
        _mla_body(TQ * (cls + 1), cls * TQ, bounded, qbt_ref.at[:, cols], kb_ref, vbt_ref, ob_ref.at[cols, :],
                  bias_ref.at[cls], s_ref, p_ref.at[cls % 2], ot_ref.at[cls])


def _mem_attn_kernel(qmt_ref, mem_ref, gains_ref, wk_ref, wvt_ref, om_ref, km_ref, vmt_ref, s_ref, p_ref, ot_ref,
                     *, mem_len, bounded):
    _mem_kv(mem_ref, gains_ref, wk_ref, wvt_ref, km_ref, vmt_ref)

    def k_of(hh, sl):
        return km_ref[sl, hh * M_HEAD_DIM:(hh + 1) * M_HEAD_DIM]

    def v_of(hh, sl):
        return vmt_ref[hh * M_HEAD_DIM:(hh + 1) * M_HEAD_DIM, sl]

    for blk in range(qmt_ref.shape[1] // TQ_MEM):
        cols = slice(blk * TQ_MEM, (blk + 1) * TQ_MEM)

        def q_of(hh):
            return qmt_ref[hh * M_HEAD_DIM:(hh + 1) * M_HEAD_DIM, cols]

        _attend(mem_len, M_HEADS, M_HEAD_DIM, q_of, k_of, v_of, lambda c: None, bounded, s_ref,
                p_ref.at[blk % 2], ot_ref.at[blk % 2])
        om_ref[cols, :] = ot_ref[blk % 2].T


def _final_kernel(x_ref, oa_ref, ob_ref, om_ref, rows_ref, wz_ref, wg_ref, wb_ref, wo_ref, out_ref):
    x = x_ref[...]
    h = _rms_lanes(x, rows_ref[ROW_GN:ROW_GN + 1, :]).astype(BF16)
    zs = [_mm(h, wz_ref[n]) for n in range(N_BRANCH)]
    gate_logits = [_mm(h, wg_ref[:, n * D_MODEL:(n + 1) * D_MODEL]) for n in range(N_BRANCH)]
    merged = jnp.zeros((TM, D_MODEL), F32)
    for n, o_ref in enumerate((oa_ref, ob_ref, om_ref)):
        y = (o_ref[...] * (zs[n] * jax.nn.sigmoid(zs[n]))).astype(BF16)
        branch = _mm(y, wb_ref[n])
        merged = merged + jax.nn.sigmoid(gate_logits[n]) * branch
    out_ref[...] = x + _mm(merged.astype(BF16), wo_ref[...])


def _full(shape):
    return pl.BlockSpec(shape, lambda *_: (0,) * len(shape), pipeline_mode=pl.Buffered(1))


def _params(n_axes):
    return pltpu.CompilerParams(dimension_semantics=("arbitrary",) * n_axes,
                                vmem_limit_bytes=VMEM_LIMIT)


def kernel(x, mem, positions, g_norm, w_in, g_qn_a, g_kn_a, g_cq, g_ckv, w_uq, w_ukv, g_qn_b, g_kn_b,
           g_mem, w_mem_kv, g_qn_m, g_kn_m, w_branch, w_out):
    b, s, d = x.shape
    m_len = mem.shape[1]
    n = b * s
    nq = s // TQ
    assert d == D_MODEL and s % TQ == 0 and TQ == KC and TM % TQ == 0 and n % TM == 0 and m_len % KC == 0
    assert g_norm.shape[0] == 1, "single-layer block"

    w = w_in[0]
    off = np.cumsum([0, 512, 512, 512, 512, IDX_DIM, IDX_HEADS, BRANCH_WIDTH, B_Q_RANK, B_KV_RANK, B_ROPE,
                     BRANCH_WIDTH, M_HEADS * M_HEAD_DIM, BRANCH_WIDTH, N_BRANCH * D_MODEL])
    off = tuple(int(o) for o in off)
    assert w.shape == (d, off[14]) and all(o % 16 == 0 for o in off[:5]) and all(o % 8 == 0 for o in off)
    assert IDX_HEADS <= 16 and off[5] + 16 <= off[14] and d % W_COLS == 0
    bf = lambda a: a.astype(BF16)
    t_block = lambda rows: pl.BlockSpec((rows, W_COLS), lambda i: (0, i))
    r_block = lambda width: pl.BlockSpec((W_COLS, width), lambda i: (i, 0))
    t_shape = lambda rows: jax.ShapeDtypeStruct((rows, d), BF16)
    r_shape = lambda width: jax.ShapeDtypeStruct((d, width), BF16)
    n_mk = M_HEADS * M_HEAD_DIM
    assert w_mem_kv.shape == (1, d, 2 * n_mk)
    (wqa_t, wka_t, wva_t, wqi_t, wqm_t, wki_t, wkr_t, wwi_t, wcq, wckv, wz, wg, wmk, wmv_t) = pl.pallas_call(
        functools.partial(_weights_kernel, off=off),
        grid=(d // W_COLS,),
        in_specs=[pl.BlockSpec((off[14], W_COLS), lambda i: (0, i)), r_block(2 * n_mk)],
        out_specs=[t_block(512), t_block(512), t_block(512), t_block(512), t_block(n_mk),
                   t_block(IDX_DIM), t_block(B_ROPE), t_block(16), r_block(B_Q_RANK), r_block(B_KV_RANK),
                   pl.BlockSpec((N_BRANCH, W_COLS, BRANCH_WIDTH), lambda i: (0, i, 0)), r_block(N_BRANCH * D_MODEL),
                   r_block(n_mk), t_block(n_mk)],
        out_shape=[t_shape(512), t_shape(512), t_shape(512), t_shape(512), t_shape(n_mk),
                   t_shape(IDX_DIM), t_shape(B_ROPE), t_shape(16), r_shape(B_Q_RANK), r_shape(B_KV_RANK),
                   jax.ShapeDtypeStruct((N_BRANCH, d, BRANCH_WIDTH), BF16), r_shape(N_BRANCH * D_MODEL),
                   r_shape(n_mk), t_shape(n_mk)],
        compiler_params=_params(1), name="weights",
    )(w.T, w_mem_kv[0])
    wuq_t = bf(jnp.pad(w_uq[0].reshape(B_Q_RANK, B_HEADS, B_QK), ((0, 0), (0, 0), (0, LANES - B_QK)))
               .reshape(B_Q_RANK, B_HEADS * LANES).T)
    ukv = w_ukv[0].reshape(B_KV_RANK, B_HEADS, B_NOPE + B_VDIM)
    wuk_t = bf(ukv[:, :, :B_NOPE].reshape(B_KV_RANK, B_HEADS * B_NOPE).T)
    wuv_t = bf(ukv[:, :, B_NOPE:].reshape(B_KV_RANK, B_HEADS * B_VDIM).T)
    wb = bf(w_branch[0])
    wo = bf(w_out[0])

    pad_to = lambda v, size: jnp.pad(v, (0, size - v.shape[0]))
    inv_a = ROPE_THETA ** (-(jnp.arange(0, A_ROT, 2, dtype=F32) / A_ROT))
    inv_b = ROPE_THETA ** (-(jnp.arange(0, B_ROPE, 2, dtype=F32) / B_ROPE))
    col_vectors = [None] * N_COLS
    col_vectors[COL_GQA], col_vectors[COL_GKA] = g_qn_a[0], g_kn_a[0]
    col_vectors[COL_GQB], col_vectors[COL_GKB] = g_qn_b[0], g_kn_b[0]
    col_vectors[COL_GQM], col_vectors[COL_INVA], col_vectors[COL_INVB] = g_qn_m[0], inv_a, inv_b
    cols = jnp.stack([pad_to(v, LANES) for v in col_vectors], axis=1)
    row_vectors = [None] * N_ROWS
    row_vectors[ROW_GN], row_vectors[ROW_GCQ], row_vectors[ROW_GCKV] = g_norm[0], g_cq[0], g_ckv[0]
    rows = jnp.stack([pad_to(v, D_MODEL) for v in row_vectors])
    mem_gains = jnp.stack([g_mem[0], pad_to(g_kn_m[0], D_MODEL)])

    x2 = x.reshape(n, d)
    pos_r = positions.reshape(1, n)
    tile = lambda width: pl.BlockSpec((TM, width), lambda i: (i, 0))
    tile_t = lambda rows: pl.BlockSpec((rows, TM), lambda i: (0, i))
    pos_spec = pl.BlockSpec((1, TM), lambda i: (0, i))

    a_w = [wqa_t, wqi_t, wwi_t, wka_t, wva_t, wki_t]
    b_w = [wcq, wckv, wkr_t, wqm_t, wuq_t, wuk_t, wuv_t]
    qat, qib, wt, ka, vat, ki, qbt, kb, vbt, qmt = pl.pallas_call(
        _proj_kernel,
        grid=(n // TM,),
        in_specs=[tile(d), pos_spec, _full(rows.shape), _full(cols.shape)] + [_full(a.shape) for a in a_w + b_w],
        out_specs=[tile_t(A_HEADS * LANES), pl.BlockSpec((TM // TQ, LANES, IDX_HEADS * TQ), lambda i: (i, 0, 0)),
                   tile_t(IDX_HEADS), pl.BlockSpec((512 // LANES, TM, LANES), lambda i: (0, i, 0)),
                   tile_t(512), tile(LANES),
                   tile_t(B_HEADS * LANES), tile(B_HEADS * LANES), tile_t(512), tile_t(512)],
        out_shape=[jax.ShapeDtypeStruct((A_HEADS * LANES, n), BF16),
                   jax.ShapeDtypeStruct((n // TQ, LANES, IDX_HEADS * TQ), BF16),
                   jax.ShapeDtypeStruct((IDX_HEADS, n), F32),
                   jax.ShapeDtypeStruct((512 // LANES, n, LANES), BF16),
                   jax.ShapeDtypeStruct((512, n), BF16),
                   jax.ShapeDtypeStruct((n, LANES), BF16),
                   jax.ShapeDtypeStruct((B_HEADS * LANES, n), BF16),
                   jax.ShapeDtypeStruct((n, B_HEADS * LANES), BF16),
                   jax.ShapeDtypeStruct((512, n), BF16),
                   jax.ShapeDtypeStruct((512, n), BF16)],
        compiler_params=_params(1), name="proj",
    )(x2, pos_r, rows, cols, *a_w, *b_w)

    o_shape = jax.ShapeDtypeStruct((n, 512), F32)

    def dsa(bounded):
        return pl.pallas_call(
            functools.partial(_dsa_kernel, seq=s, bounded=bounded),
            grid=(b,),
            in_specs=[pl.BlockSpec((A_HEADS * LANES, s), lambda bi: (0, bi)),
                      pl.BlockSpec((nq, LANES, IDX_HEADS * TQ), lambda bi: (bi, 0, 0)),
                      pl.BlockSpec((IDX_HEADS, s), lambda bi: (0, bi)),
                      pl.BlockSpec((512 // LANES, s, LANES), lambda bi: (0, bi, 0)),
                      pl.BlockSpec((512, s), lambda bi: (0, bi)), pl.BlockSpec((s, LANES), lambda bi: (bi, 0)),
                      pl.BlockSpec(memory_space=pl.ANY)],
            out_specs=pl.BlockSpec((s, 512), lambda bi: (bi, 0)), out_shape=o_shape,
            scratch_shapes=[pltpu.VMEM((s, TQ), F32), pltpu.VMEM((s, TQ), BF16), pltpu.VMEM((s, TQ), F32),
                            pltpu.VMEM((s, TQ), F32),
                            pltpu.VMEM((2 * s, TQ), BF16), pltpu.VMEM((512, TQ), F32)],
            compiler_params=_params(1), name="dsa" if bounded else "dsa_general")

    def mla(bounded):
        return pl.pallas_call(
            functools.partial(_mla_kernel, seq=s, bounded=bounded),
            grid=(b,),
            in_specs=[pl.BlockSpec((B_HEADS * LANES, s), lambda bi: (0, bi)),
                      pl.BlockSpec((s, B_HEADS * LANES), lambda bi: (bi, 0)),
                      pl.BlockSpec((512, s), lambda bi: (0, bi)), pl.BlockSpec(memory_space=pl.ANY)],
            out_specs=pl.BlockSpec((s, 512), lambda bi: (bi, 0)), out_shape=o_shape,
            scratch_shapes=[pltpu.VMEM((nq, KC, TQ), F32), pltpu.VMEM((s, TQ), F32),
                            pltpu.VMEM((2, 2 * s, TQ), BF16), pltpu.VMEM((nq, 512, TQ), F32)],
            compiler_params=_params(1), name="mla" if bounded else "mla_general")

    def mem_attn(bounded):
        return pl.pallas_call(
            functools.partial(_mem_attn_kernel, mem_len=m_len, bounded=bounded),
            grid=(b,),
            in_specs=[pl.BlockSpec((512, s), lambda bi: (0, bi)), pl.BlockSpec((m_len, d), lambda bi: (bi, 0)),
                      _full(mem_gains.shape), _full(wmk.shape), _full(wmv_t.shape)],
            out_specs=pl.BlockSpec((s, 512), lambda bi: (bi, 0)), out_shape=o_shape,
            scratch_shapes=[pltpu.VMEM((m_len, 512), BF16), pltpu.VMEM((512, m_len), BF16),
                            pltpu.VMEM((m_len, TQ_MEM), F32), pltpu.VMEM((2, M_HEADS * m_len, TQ_MEM), BF16),
                            pltpu.VMEM((2, 512, TQ_MEM), F32)],
            compiler_params=_params(1), name="mem_attn" if bounded else "mem_attn_general")

    gain_max = jnp.max(jnp.abs(cols), axis=0)

    def score_bound(cq_, ck_, dim):
        return dim ** 0.5 * LOG2E * 1.02 * gain_max[cq_] * gain_max[ck_]

    def attention(bounded):
        def run(*ops):
            om = mem_attn(bounded)(*ops[9:])
            ob = mla(bounded)(*ops[6:9], om)
            return (dsa(bounded)(*ops[:6], ob), ob, om)
        return run

    k_mem_gain = jnp.max(jnp.abs(g_kn_m[0]))
    worst = jnp.maximum(jnp.maximum(score_bound(COL_GQA, COL_GKA, A_HEAD_DIM), score_bound(COL_GQB, COL_GKB, B_QK)),
                        M_HEAD_DIM ** 0.5 * LOG2E * 1.02 * gain_max[COL_GQM] * k_mem_gain)
    oa, ob, om = lax.cond(worst <= BOUNDED_SCORE_LIMIT, attention(True), attention(False),
                          qat, qib, wt, ka, vat, ki, qbt, kb, vbt, qmt, mem.reshape(b * m_len, d), mem_gains,
                          wmk, wmv_t)

    out = pl.pallas_call(
        _final_kernel,
        grid=(n // TM,),
        in_specs=[tile(d), tile(512), tile(512), tile(512), _full(rows.shape), _full(wz.shape), _full(wg.shape),
                  _full(wb.shape), _full(wo.shape)],
        out_specs=tile(d), out_shape=jax.ShapeDtypeStruct((n, d), x.dtype),
        compiler_params=_params(1), name="final",
    )(x2, oa, ob, om, rows, wz, wg, wb, wo)
    return out.reshape(b, s, d)
```

```python
import functools

import numpy as np
import jax
import jax.numpy as jnp
from jax import lax
from jax.experimental import pallas as pl
from jax.experimental.pallas import tpu as pltpu

F32 = jnp.float32
BF16 = jnp.bfloat16
I32 = jnp.int32

D_MODEL = 1024
ROPE_THETA = 500000.0
EPS = 1e-6
NEG = -1e30
N_BRANCH = 3
BRANCH_WIDTH = 512
A_HEADS = 8
A_HEAD_DIM = 64
A_ROT = A_HEAD_DIM // 4
IDX_HEADS = 8
IDX_DIM = 64
TOPK_MAX = 256
B_HEADS = 8
B_NOPE = 64
B_ROPE = 32
B_VDIM = 64
B_QK = B_NOPE + B_ROPE
B_Q_RANK = 384
B_KV_RANK = 256
M_HEADS = 4
M_HEAD_DIM = 128

LANES = 128
TM = 512
TQ = 256
TQ_MEM = 512
KC = 256
COUNT_ROWS = 64
VMEM_LIMIT = 56 * 1024 * 1024
INT_MIN = -2 ** 31
LOG2E = 1.4426950408889634
BOUNDED_SCORE_LIMIT = 32.0


def _nt(a, b):
    return lax.dot_general(a, b, (((1,), (1,)), ((), ())), preferred_element_type=F32)


def _mm(a, b):
    return jnp.dot(a, b, preferred_element_type=F32)


def _rms_lanes(xf, g_row, n=None):
    n = xf.shape[-1] if n is None else n
    ms = jnp.sum(xf * xf, axis=-1, keepdims=True) / n
    return xf * lax.rsqrt(ms + EPS) * g_row


def _rms_rows(blk, g_col, n=None):
    n = blk.shape[0] if n is None else n
    ms = jnp.sum(blk * blk, axis=0, keepdims=True) / n
    return blk * lax.rsqrt(ms + EPS) * g_col


def _rope_rows(blk, lo, half, cos_t, sin_t):
    x1 = blk[lo:lo + half]
    x2 = blk[lo + half:lo + 2 * half]
    parts = []
    if lo:
        parts.append(blk[:lo])
    parts += [x1 * cos_t - x2 * sin_t, x2 * cos_t + x1 * sin_t]
    if lo + 2 * half < blk.shape[0]:
        parts.append(blk[lo + 2 * half:])
    return jnp.concatenate(parts, axis=0)


def _token_major(blocks):
    rows = sum(blk.shape[0] for blk in blocks)
    if rows < LANES:
        blocks = list(blocks) + [jnp.zeros((LANES - rows, blocks[0].shape[1]), F32)]
    return jnp.concatenate(blocks, axis=0).T.astype(BF16)


W_COLS = 256


def _weights_kernel(wt_ref, wm_ref, wqa_ref, wka_ref, wva_ref, wqi_ref, wqm_ref, wki_ref, wkr_ref, wwi_ref,
                    wcq_ref, wckv_ref, wz_ref, wg_ref, wmk_ref, wmv_ref, *, off):
    n_mk = wmk_ref.shape[1]
    wmk_ref[...] = wm_ref[:, :n_mk].astype(BF16)
    wmv_ref[...] = wm_ref[:, n_mk:].T.astype(BF16)
    seg = lambda i: wt_ref[off[i]:off[i + 1], :]
    for i, o_ref in ((0, wqa_ref), (1, wka_ref), (2, wva_ref), (3, wqi_ref), (11, wqm_ref), (4, wki_ref),
                     (9, wkr_ref)):
        o_ref[...] = seg(i).astype(BF16)
    head_row = lax.broadcasted_iota(I32, (16, W_COLS), 0)
    wwi_ref[...] = jnp.where(head_row < IDX_HEADS, wt_ref[off[5]:off[5] + 16, :], 0.0).astype(BF16)
    wcq_ref[...] = seg(7).T.astype(BF16)
    wckv_ref[...] = seg(8).T.astype(BF16)
    for n, i in enumerate((6, 10, 12)):
        wz_ref[n] = seg(i).T.astype(BF16)
    wg_ref[...] = seg(13).T.astype(BF16)


COL_GQA, COL_GKA, COL_GQB, COL_GKB, COL_GQM, COL_INVA, COL_INVB, N_COLS = 0, 1, 2, 3, 4, 5, 6, 7
ROW_GN, ROW_GCQ, ROW_GCKV, N_ROWS = 0, 1, 2, 3


def _proj_kernel(x_ref, posr_ref, rows_ref, cols_ref,
                 wqa_ref, wqi_ref, wwi_ref, wka_ref, wva_ref, wki_ref,
                 wcq_ref, wckv_ref, wkr_ref, wqm_ref, wuq_ref, wuk_ref, wuv_ref,
                 qat_ref, qib_ref, wt_ref, ka_ref, vat_ref, ki_ref, qbt_ref, kb_ref, vbt_ref, qmt_ref):
    rows, cols = rows_ref[...], cols_ref[...]
    col = lambda j, n: cols[0:n, j:j + 1]
    h = _rms_lanes(x_ref[...], rows[ROW_GN:ROW_GN + 1, :]).astype(BF16)
    pos = posr_ref[...].astype(F32)
    half_a, half_b = A_ROT // 2, B_ROPE // 2
    ang_a = col(COL_INVA, half_a) * pos
    cos_a, sin_a = jnp.cos(ang_a), jnp.sin(ang_a)
    ang_b = col(COL_INVB, half_b) * pos
    cos_b, sin_b = jnp.cos(ang_b), jnp.sin(ang_b)

    cq = _mm(h, wcq_ref[...])
    ckv = _mm(h, wckv_ref[...])
    qa = _nt(wqa_ref[...], h)
    cq = _rms_lanes(cq, rows[ROW_GCQ:ROW_GCQ + 1, 0:B_Q_RANK]).astype(BF16)
    ckv = _rms_lanes(ckv, rows[ROW_GCKV:ROW_GCKV + 1, 0:B_KV_RANK]).astype(BF16)
    qb = _nt(wuq_ref[...], cq)
    kn = _nt(wuk_ref[...], ckv)
    kr = _nt(wkr_ref[...], h)

    gq = col(COL_GQA, A_HEAD_DIM)
    for hh in range(A_HEADS):
        blk = _rms_rows(qa[hh * A_HEAD_DIM:(hh + 1) * A_HEAD_DIM], gq)
        blk = _rope_rows(blk, 0, half_a, cos_a, sin_a) * (A_HEAD_DIM ** -0.5 * LOG2E)
        own = hh * LANES + (hh % 2) * A_HEAD_DIM
        other = hh * LANES + (1 - hh % 2) * A_HEAD_DIM
        qat_ref[own:own + A_HEAD_DIM, :] = blk.astype(BF16)
        qat_ref[other:other + A_HEAD_DIM, :] = jnp.zeros((A_HEAD_DIM, TM), BF16)

    ka = _nt(wka_ref[...], h)
    qm = _nt(wqm_ref[...], h)

    gq = col(COL_GQB, LANES)
    for hh in range(B_HEADS):
        blk = _rms_rows(qb[hh * LANES:(hh + 1) * LANES], gq, n=B_QK)
        blk = _rope_rows(blk, B_NOPE, half_b, cos_b, sin_b) * (B_QK ** -0.5 * LOG2E)
        qbt_ref[hh * LANES:(hh + 1) * LANES, :] = blk.astype(BF16)
    gk = col(COL_GKB, LANES)
    pad = jnp.zeros((LANES - B_QK, TM), F32)
    for hh in range(B_HEADS):
        blk = jnp.concatenate([kn[hh * B_NOPE:(hh + 1) * B_NOPE], kr, pad], axis=0)
        blk = _rope_rows(_rms_rows(blk, gk, n=B_QK), B_NOPE, half_b, cos_b, sin_b)
        kb_ref[:, hh * LANES:(hh + 1) * LANES] = _token_major([blk])

    qi = _nt(wqi_ref[...], h)
    ki = _nt(wki_ref[...], h)
    wt_ref[...] = _nt(wwi_ref[...], h)[0:IDX_HEADS] * (IDX_HEADS ** -0.5)

    gk = col(COL_GKA, A_HEAD_DIM)
    for c in range(A_HEADS // 2):
        pair = [_rope_rows(_rms_rows(ka[hh * A_HEAD_DIM:(hh + 1) * A_HEAD_DIM], gk), 0, half_a, cos_a, sin_a)
                for hh in (2 * c, 2 * c + 1)]
        ka_ref[c, :, :] = _token_major(pair)
    gm = col(COL_GQM, M_HEAD_DIM)
    for hh in range(M_HEADS):
        blk = _rms_rows(qm[hh * M_HEAD_DIM:(hh + 1) * M_HEAD_DIM], gm) * (M_HEAD_DIM ** -0.5 * LOG2E)
        qmt_ref[hh * M_HEAD_DIM:(hh + 1) * M_HEAD_DIM, :] = blk.astype(BF16)

    vbt_ref[...] = _nt(wuv_ref[...], ckv).astype(BF16)
    vat_ref[...] = _nt(wva_ref[...], h).astype(BF16)

    for hh in range(IDX_HEADS):
        blk = _rope_rows(qi[hh * IDX_DIM:(hh + 1) * IDX_DIM], 0, half_a, cos_a, sin_a)
        blk = (blk * (IDX_DIM ** -0.5)).astype(BF16)
        for j in range(TM // TQ):
            qib_ref[j, 0:IDX_DIM, hh * TQ:(hh + 1) * TQ] = blk[:, j * TQ:(j + 1) * TQ]
    qib_ref[:, IDX_DIM:, :] = jnp.zeros((TM // TQ, LANES - IDX_DIM, IDX_HEADS * TQ), BF16)
    ki_ref[...] = _token_major([_rope_rows(ki, 0, half_a, cos_a, sin_a)])


def _mem_kv(mem_ref, gains_ref, wk_ref, wvt_ref, km_ref, vmt_ref):
    hm = _rms_lanes(mem_ref[...], gains_ref[0:1, :]).astype(BF16)
    k = _mm(hm, wk_ref[...])
    gk = gains_ref[1:2, 0:M_HEAD_DIM]
    for hh in range(M_HEADS):
        kc = _rms_lanes(k[:, hh * M_HEAD_DIM:(hh + 1) * M_HEAD_DIM], gk)
        km_ref[:, hh * M_HEAD_DIM:(hh + 1) * M_HEAD_DIM] = kc.astype(BF16)
    vmt_ref[...] = _nt(wvt_ref[...], hm).astype(BF16)


def _attend(nk, n_heads, dv, q_of, k_of, v_of, bias_of, bounded, s_ref, p_ref, ot_ref):
    nq = ot_ref.shape[1]
    chunks = [slice(c * KC, (c + 1) * KC) for c in range(nk // KC)]

    def scores(hh, q, c):
        s = _mm(k_of(hh, chunks[c]), q)
        b = bias_of(c)
        return s if b is None else s + b

    depth = max(1, min(n_heads - 1, 8 // len(chunks)))
    slots = depth + 1

    def p_rows(hh, sl):
        base = (hh % slots) * nk
        return slice(base + sl.start, base + sl.stop)

    def probabilities(hh):
        q = q_of(hh)
        if bounded:
            l8 = jnp.zeros((8, nq), F32)
            for c in range(len(chunks)):
                p = jnp.exp2(scores(hh, q, c))
                l8 = l8 + p.reshape(KC // 8, 8, nq).sum(axis=0)
                p_ref[p_rows(hh, chunks[c]), :] = p.astype(BF16)
            return jnp.sum(l8, axis=0, keepdims=True)
        m = jnp.full((1, nq), -jnp.inf, F32)
        for c in range(len(chunks)):
            s = scores(hh, q, c)
            s_ref[chunks[c], :] = s
            m = jnp.maximum(m, jnp.max(s, axis=0, keepdims=True))
        l = jnp.zeros((1, nq), F32)
        for c in range(len(chunks)):
            p = jnp.exp2(s_ref[chunks[c], :] - m)
            l = l + jnp.sum(p, axis=0, keepdims=True)
            p_ref[p_rows(hh, chunks[c]), :] = p.astype(BF16)
        return l

    def weighted_values(hh, l):
        o = _mm(v_of(hh, slice(0, nk)), p_ref[p_rows(hh, slice(0, nk)), :])
        ot_ref[hh * dv:(hh + 1) * dv, :] = o / l

    sums = {}
    for step in range(n_heads + depth):
        if step < n_heads:
            sums[step] = probabilities(step)
        if step >= depth:
            weighted_values(step - depth, sums.pop(step - depth))


def _count(score_ref, nk, pred):
    cnt = jnp.zeros((COUNT_ROWS, TQ), I32)
    for r in range(0, nk, COUNT_ROWS):
        cnt = jnp.where(pred(score_ref[r:r + COUNT_ROWS, :]), cnt + 1, cnt)
    return jnp.sum(cnt, axis=0, keepdims=True)


def _ordered_to_bits(u, magnitude_mask):
    k = u ^ INT_MIN
    return k ^ ((k >> 31) & magnitude_mask)


def _ordered_pattern_to_float(u):
    return pltpu.bitcast(_ordered_to_bits(u, 0x7FFFFFFF), F32)


def _count_rounded(round_ref, nk, cand):
    assert nk // COUNT_ROWS <= 256
    one, zero = jnp.ones((), BF16), jnp.zeros((), BF16)
    cnt = jnp.zeros((COUNT_ROWS, TQ), BF16)
    for r in range(0, nk, COUNT_ROWS):
        cnt = cnt + jnp.where(round_ref[r:r + COUNT_ROWS, :] >= cand, one, zero)
    return jnp.sum(cnt.astype(F32), axis=0, keepdims=True)


def _select_topk(nk, q_pos, row, chunks, qib_ref, wt_ref, ki_ref, score_ref, round_ref, emit):
    for c, sl in enumerate(chunks):
        ki_c = ki_ref[sl, :]
        acc = jnp.zeros((KC, TQ), F32)
        for hh in range(IDX_HEADS):
            d = _mm(ki_c, qib_ref[0, :, hh * TQ:(hh + 1) * TQ])
            acc = acc + jnp.maximum(d, 0.0) * wt_ref[hh:hh + 1, :]
        score = jnp.where(row + c * KC <= q_pos, acc, NEG)
        score_ref[sl, :] = score
        round_ref[sl, :] = score.astype(BF16)

    def coarse(i, c_u):
        cand_u = c_u | jnp.left_shift(jnp.int32(1), 31 - i)
        cand = pltpu.bitcast(_ordered_to_bits(cand_u, 0x7FFF0000), F32).astype(BF16)
        return jnp.where(_count_rounded(round_ref, nk, cand) >= TOPK_MAX, cand_u, c_u)

    c_u = lax.fori_loop(0, 16, coarse, jnp.zeros((1, TQ), I32))
    pred_bits = _ordered_to_bits(c_u - (1 << 16), 0x7FFF0000)
    base_u = (pred_bits ^ ((pred_bits >> 31) & 0x7FFFFFFF)) ^ INT_MIN

    def fine(i, carry):
        off, cnt_t = carry
        cand_off = off | jnp.left_shift(jnp.int32(1), 16 - i)
        cand = _ordered_pattern_to_float(base_u + cand_off)
        cnt = _count(score_ref, nk, lambda x: x >= cand)
        ok = cnt >= TOPK_MAX
        return jnp.where(ok, cand_off, off), jnp.where(ok, cnt, cnt_t)

    off, cnt_t = lax.fori_loop(0, 17, fine, (jnp.zeros((1, TQ), I32), jnp.full((1, TQ), nk, I32)))
    thr = _ordered_pattern_to_float(base_u + off)
    split_ties = jnp.max(jnp.where(cnt_t > TOPK_MAX, 1, 0)) > 0

    @pl.when(jnp.logical_not(split_ties))
    def _():
        for c, sl in enumerate(chunks):
            emit(sl, (score_ref[sl, :] >= thr) & (row + c * KC <= q_pos))

    @pl.when(split_ties)
    def _():
        room = (TOPK_MAX - _count(score_ref, nk, lambda x: x > thr)).astype(F32)
        tri = lax.broadcasted_iota(I32, (KC, KC), 0) >= lax.broadcasted_iota(I32, (KC, KC), 1)
        tri = jnp.where(tri, 1.0, 0.0).astype(BF16)
        running = jnp.zeros((1, TQ), F32)
        for c, sl in enumerate(chunks):
            x = score_ref[sl, :]
            tie = x == thr
            rank = _mm(tri, jnp.where(tie, 1.0, 0.0).astype(BF16)) + running
            running = rank[KC - 1:KC, :]
            emit(sl, ((x > thr) | (tie & (rank <= room))) & (row + c * KC <= q_pos))


def _dsa_body(nk, start, bounded, qat_ref, qib_ref, wt_ref, ka_ref, vat_ref, ki_ref, oa_ref,
              score_ref, round_ref, bias_ref, s_ref, p_ref, ot_ref):
    q_pos = start + lax.broadcasted_iota(I32, (1, TQ), 1)
    row = lax.broadcasted_iota(I32, (KC, TQ), 0)
    chunks = [slice(c * KC, (c + 1) * KC) for c in range(nk // KC)]

    def emit_bias(sl, keep):
        bias_ref[sl, :] = jnp.where(keep, 0.0, NEG)

    if nk <= TOPK_MAX:
        for c, sl in enumerate(chunks):
            emit_bias(sl, row + c * KC <= q_pos)
    else:
        _select_topk(nk, q_pos, row, chunks, qib_ref, wt_ref, ki_ref, score_ref, round_ref, emit_bias)

    def q_of(hh):
        return qat_ref[hh * LANES:(hh + 1) * LANES, :]

    def k_of(hh, sl):
        return ka_ref[hh // 2, sl, :]

    def v_of(hh, sl):
        return vat_ref[hh * A_HEAD_DIM:(hh + 1) * A_HEAD_DIM, sl]

    _attend(nk, A_HEADS, A_HEAD_DIM, q_of, k_of, v_of, lambda c: bias_ref[chunks[c], :], bounded,
            s_ref, p_ref, ot_ref)
    oa_ref[...] = ot_ref[...].T


def _dsa_kernel(qat_ref, qib_ref, wt_ref, ka_ref, vat_ref, ki_ref, after_ref, oa_ref,
                score_ref, round_ref, bias_ref, s_ref, p_ref, ot_ref, *, seq, bounded):
    del after_ref
    for cls in range(seq // TQ):
        cols = slice(cls * TQ, (cls + 1) * TQ)
        _dsa_body(TQ * (cls + 1), cls * TQ, bounded, qat_ref.at[:, cols], qib_ref.at[pl.ds(cls, 1)],
                  wt_ref.at[:, cols], ka_ref, vat_ref, ki_ref, oa_ref.at[cols, :],
                  score_ref, round_ref, bias_ref, s_ref, p_ref, ot_ref)


def _mla_body(nk, start, bounded, qbt_ref, kb_ref, vbt_ref, ob_ref, bias_ref, s_ref, p_ref, ot_ref):
    last = nk // KC - 1
    q_pos = start + lax.broadcasted_iota(I32, (1, TQ), 1)
    row = lax.broadcasted_iota(I32, (KC, TQ), 0)
    bias_ref[0:KC, :] = jnp.where(row + last * KC <= q_pos, 0.0, NEG)

    def q_of(hh):
        return qbt_ref[hh * LANES:(hh + 1) * LANES, :]

    def k_of(hh, sl):
        return kb_ref[sl, hh * LANES:(hh + 1) * LANES]

    def v_of(hh, sl):
        return vbt_ref[hh * B_VDIM:(hh + 1) * B_VDIM, sl]

    _attend(nk, B_HEADS, B_VDIM, q_of, k_of, v_of, lambda c: bias_ref[0:KC, :] if c == last else None,
            bounded, s_ref, p_ref, ot_ref)
    ob_ref[...] = ot_ref[...].T


def _mla_kernel(qbt_ref, kb_ref, vbt_ref, after_ref, ob_ref, bias_ref, s_ref, p_ref, ot_ref, *, seq, bounded):
    del after_ref
    for cls in range(seq // TQ):
        cols = slice(cls * TQ, (cls + 1) * TQ)
        _mla_body(TQ * (cls + 1), cls * TQ, bounded, qbt_ref.at[:, cols], kb_ref, vbt_ref, ob_ref.at[cols, :],
                  bias_ref.at[cls], s_ref, p_ref.at[cls % 2], ot_ref.at[cls])


def _mem_attn_kernel(qmt_ref, mem_ref, gains_ref, wk_ref, wvt_ref, om_ref, km_ref, vmt_ref, s_ref, p_ref, ot_ref,
                     *, mem_len, bounded):
    _mem_kv(mem_ref, gains_ref, wk_ref, wvt_ref, km_ref, vmt_ref)

    def k_of(hh, sl):
        return km_ref[sl, hh * M_HEAD_DIM:(hh + 1) * M_HEAD_DIM]

    def v_of(hh, sl):
        return vmt_ref[hh * M_HEAD_DIM:(hh + 1) * M_HEAD_DIM, sl]

    for blk in range(qmt_ref.shape[1] // TQ_MEM):
        cols = slice(blk * TQ_MEM, (blk + 1) * TQ_MEM)

        def q_of(hh):
            return qmt_ref[hh * M_HEAD_DIM:(hh + 1) * M_HEAD_DIM, cols]

        _attend(mem_len, M_HEADS, M_HEAD_DIM, q_of, k_of, v_of, lambda c: None, bounded, s_ref,
                p_ref.at[blk % 2], ot_ref.at[blk % 2])
        om_ref[cols, :] = ot_ref[blk % 2].T


def _final_kernel(x_ref, oa_ref, ob_ref, om_ref, rows_ref, wz_ref, wg_ref, wb_ref, wo_ref, out_ref):
    x = x_ref[...]
    h = _rms_lanes(x, rows_ref[ROW_GN:ROW_GN + 1, :]).astype(BF16)
    zs = [_mm(h, wz_ref[n]) for n in range(N_BRANCH)]
    gate_logits = [_mm(h, wg_ref[:, n * D_MODEL:(n + 1) * D_MODEL]) for n in range(N_BRANCH)]
    merged = jnp.zeros((TM, D_MODEL), F32)
    for n, o_ref in enumerate((oa_ref, ob_ref, om_ref)):
        y = (o_ref[...] * (zs[n] * jax.nn.sigmoid(zs[n]))).astype(BF16)
        branch = _mm(y, wb_ref[n])
        merged = merged + jax.nn.sigmoid(gate_logits[n]) * branch
    out_ref[...] = x + _mm(merged.astype(BF16), wo_ref[...])


def _full(shape):
    return pl.BlockSpec(shape, lambda *_: (0,) * len(shape), pipeline_mode=pl.Buffered(1))


def _params(n_axes):
    return pltpu.CompilerParams(dimension_semantics=("arbitrary",) * n_axes,
                                vmem_limit_bytes=VMEM_LIMIT)


def kernel(x, mem, positions, g_norm, w_in, g_qn_a, g_kn_a, g_cq, g_ckv, w_uq, w_ukv, g_qn_b, g_kn_b,
           g_mem, w_mem_kv, g_qn_m, g_kn_m, w_branch, w_out):
    b, s, d = x.shape
    m_len = mem.shape[1]
    n = b * s
    nq = s // TQ
    assert d == D_MODEL and s % TQ == 0 and TQ == KC and TM % TQ == 0 and n % TM == 0 and m_len % KC == 0
    assert g_norm.shape[0] == 1, "single-layer block"

    w = w_in[0]
    off = np.cumsum([0, 512, 512, 512, 512, IDX_DIM, IDX_HEADS, BRANCH_WIDTH, B_Q_RANK, B_KV_RANK, B_ROPE,
                     BRANCH_WIDTH, M_HEADS * M_HEAD_DIM, BRANCH_WIDTH, N_BRANCH * D_MODEL])
    off = tuple(int(o) for o in off)
    assert w.shape == (d, off[14]) and all(o % 16 == 0 for o in off[:5]) and all(o % 8 == 0 for o in off)
    assert IDX_HEADS <= 16 and off[5] + 16 <= off[14] and d % W_COLS == 0
    bf = lambda a: a.astype(BF16)
    t_block = lambda rows: pl.BlockSpec((rows, W_COLS), lambda i: (0, i))
    r_block = lambda width: pl.BlockSpec((W_COLS, width), lambda i: (i, 0))
    t_shape = lambda rows: jax.ShapeDtypeStruct((rows, d), BF16)
    r_shape = lambda width: jax.ShapeDtypeStruct((d, width), BF16)
    n_mk = M_HEADS * M_HEAD_DIM
    assert w_mem_kv.shape == (1, d, 2 * n_mk)
    (wqa_t, wka_t, wva_t, wqi_t, wqm_t, wki_t, wkr_t, wwi_t, wcq, wckv, wz, wg, wmk, wmv_t) = pl.pallas_call(
        functools.partial(_weights_kernel, off=off),
        grid=(d // W_COLS,),
        in_specs=[pl.BlockSpec((off[14], W_COLS), lambda i: (0, i)), r_block(2 * n_mk)],
        out_specs=[t_block(512), t_block(512), t_block(512), t_block(512), t_block(n_mk),
                   t_block(IDX_DIM), t_block(B_ROPE), t_block(16), r_block(B_Q_RANK), r_block(B_KV_RANK),
                   pl.BlockSpec((N_BRANCH, W_COLS, BRANCH_WIDTH), lambda i: (0, i, 0)), r_block(N_BRANCH * D_MODEL),
                   r_block(n_mk), t_block(n_mk)],
        out_shape=[t_shape(512), t_shape(512), t_shape(512), t_shape(512), t_shape(n_mk),
                   t_shape(IDX_DIM), t_shape(B_ROPE), t_shape(16), r_shape(B_Q_RANK), r_shape(B_KV_RANK),
                   jax.ShapeDtypeStruct((N_BRANCH, d, BRANCH_WIDTH), BF16), r_shape(N_BRANCH * D_MODEL),
                   r_shape(n_mk), t_shape(n_mk)],
        compiler_params=_params(1), name="weights",
    )(w.T, w_mem_kv[0])
    wuq_t = bf(jnp.pad(w_uq[0].reshape(B_Q_RANK, B_HEADS, B_QK), ((0, 0), (0, 0), (0, LANES - B_QK)))
               .reshape(B_Q_RANK, B_HEADS * LANES).T)
    ukv = w_ukv[0].reshape(B_KV_RANK, B_HEADS, B_NOPE + B_VDIM)
    wuk_t = bf(ukv[:, :, :B_NOPE].reshape(B_KV_RANK, B_HEADS * B_NOPE).T)
    wuv_t = bf(ukv[:, :, B_NOPE:].reshape(B_KV_RANK, B_HEADS * B_VDIM).T)
    wb = bf(w_branch[0])
    wo = bf(w_out[0])

    pad_to = lambda v, size: jnp.pad(v, (0, size - v.shape[0]))
    inv_a = ROPE_THETA ** (-(jnp.arange(0, A_ROT, 2, dtype=F32) / A_ROT))
    inv_b = ROPE_THETA ** (-(jnp.arange(0, B_ROPE, 2, dtype=F32) / B_ROPE))
    col_vectors = [None] * N_COLS
    col_vectors[COL_GQA], col_vectors[COL_GKA] = g_qn_a[0], g_kn_a[0]
    col_vectors[COL_GQB], col_vectors[COL_GKB] = g_qn_b[0], g_kn_b[0]
    col_vectors[COL_GQM], col_vectors[COL_INVA], col_vectors[COL_INVB] = g_qn_m[0], inv_a, inv_b
    cols = jnp.stack([pad_to(v, LANES) for v in col_vectors], axis=1)
    row_vectors = [None] * N_ROWS
    row_vectors[ROW_GN], row_vectors[ROW_GCQ], row_vectors[ROW_GCKV] = g_norm[0], g_cq[0], g_ckv[0]
    rows = jnp.stack([pad_to(v, D_MODEL) for v in row_vectors])
    mem_gains = jnp.stack([g_mem[0], pad_to(g_kn_m[0], D_MODEL)])

    x2 = x.reshape(n, d)
    pos_r = positions.reshape(1, n)
    tile = lambda width: pl.BlockSpec((TM, width), lambda i: (i, 0))
    tile_t = lambda rows: pl.BlockSpec((rows, TM), lambda i: (0, i))
    pos_spec = pl.BlockSpec((1, TM), lambda i: (0, i))

    a_w = [wqa_t, wqi_t, wwi_t, wka_t, wva_t, wki_t]
    b_w = [wcq, wckv, wkr_t, wqm_t, wuq_t, wuk_t, wuv_t]
    qat, qib, wt, ka, vat, ki, qbt, kb, vbt, qmt = pl.pallas_call(
        _proj_kernel,
        grid=(n // TM,),
        in_specs=[tile(d), pos_spec, _full(rows.shape), _full(cols.shape)] + [_full(a.shape) for a in a_w + b_w],
        out_specs=[tile_t(A_HEADS * LANES), pl.BlockSpec((TM // TQ, LANES, IDX_HEADS * TQ), lambda i: (i, 0, 0)),
                   tile_t(IDX_HEADS), pl.BlockSpec((512 // LANES, TM, LANES), lambda i: (0, i, 0)),
                   tile_t(512), tile(LANES),
                   tile_t(B_HEADS * LANES), tile(B_HEADS * LANES), tile_t(512), tile_t(512)],
        out_shape=[jax.ShapeDtypeStruct((A_HEADS * LANES, n), BF16),
                   jax.ShapeDtypeStruct((n // TQ, LANES, IDX_HEADS * TQ), BF16),
                   jax.ShapeDtypeStruct((IDX_HEADS, n), F32),
                   jax.ShapeDtypeStruct((512 // LANES, n, LANES), BF16),
                   jax.ShapeDtypeStruct((512, n), BF16),
                   jax.ShapeDtypeStruct((n, LANES), BF16),
                   jax.ShapeDtypeStruct((B_HEADS * LANES, n), BF16),
                   jax.ShapeDtypeStruct((n, B_HEADS * LANES), BF16),
                   jax.ShapeDtypeStruct((512, n), BF16),
                   jax.ShapeDtypeStruct((512, n), BF16)],
        compiler_params=_params(1), name="proj",
    )(x2, pos_r, rows, cols, *a_w, *b_w)

    o_shape = jax.ShapeDtypeStruct((n, 512), F32)

    def dsa(bounded):
        return pl.pallas_call(
            functools.partial(_dsa_kernel, seq=s, bounded=bounded),
            grid=(b,),
            in_specs=[pl.BlockSpec((A_HEADS * LANES, s), lambda bi: (0, bi)),
                      pl.BlockSpec((nq, LANES, IDX_HEADS * TQ), lambda bi: (bi, 0, 0)),
                      pl.BlockSpec((IDX_HEADS, s), lambda bi: (0, bi)),
                      pl.BlockSpec((512 // LANES, s, LANES), lambda bi: (0, bi, 0)),
                      pl.BlockSpec((512, s), lambda bi: (0, bi)), pl.BlockSpec((s, LANES), lambda bi: (bi, 0)),
                      pl.BlockSpec(memory_space=pl.ANY)],
            out_specs=pl.BlockSpec((s, 512), lambda bi: (bi, 0)), out_shape=o_shape,
            scratch_shapes=[pltpu.VMEM((s, TQ), F32), pltpu.VMEM((s, TQ), BF16), pltpu.VMEM((s, TQ), F32),
                            pltpu.VMEM((s, TQ), F32),
                            pltpu.VMEM((2 * s, TQ), BF16), pltpu.VMEM((512, TQ), F32)],
            compiler_params=_params(1), name="dsa" if bounded else "dsa_general")

    def mla(bounded):
        return pl.pallas_call(
            functools.partial(_mla_kernel, seq=s, bounded=bounded),
            grid=(b,),
            in_specs=[pl.BlockSpec((B_HEADS * LANES, s), lambda bi: (0, bi)),
                      pl.BlockSpec((s, B_HEADS * LANES), lambda bi: (bi, 0)),
                      pl.BlockSpec((512, s), lambda bi: (0, bi)), pl.BlockSpec(memory_space=pl.ANY)],
            out_specs=pl.BlockSpec((s, 512), lambda bi: (bi, 0)), out_shape=o_shape,
            scratch_shapes=[pltpu.VMEM((nq, KC, TQ), F32), pltpu.VMEM((s, TQ), F32),
                            pltpu.VMEM((2, 2 * s, TQ), BF16), pltpu.VMEM((nq, 512, TQ), F32)],
            compiler_params=_params(1), name="mla" if bounded else "mla_general")

    def mem_attn(bounded):
        return pl.pallas_call(
            functools.partial(_mem_attn_kernel, mem_len=m_len, bounded=bounded),
            grid=(b,),
            in_specs=[pl.BlockSpec((512, s), lambda bi: (0, bi)), pl.BlockSpec((m_len, d), lambda bi: (bi, 0)),
                      _full(mem_gains.shape), _full(wmk.shape), _full(wmv_t.shape)],
            out_specs=pl.BlockSpec((s, 512), lambda bi: (bi, 0)), out_shape=o_shape,
            scratch_shapes=[pltpu.VMEM((m_len, 512), BF16), pltpu.VMEM((512, m_len), BF16),
                            pltpu.VMEM((m_len, TQ_MEM), F32), pltpu.VMEM((2, M_HEADS * m_len, TQ_MEM), BF16),
                            pltpu.VMEM((2, 512, TQ_MEM), F32)],
            compiler_params=_params(1), name="mem_attn" if bounded else "mem_attn_general")

    gain_max = jnp.max(jnp.abs(cols), axis=0)

    def score_bound(cq_, ck_, dim):
        return dim ** 0.5 * LOG2E * 1.02 * gain_max[cq_] * gain_max[ck_]

    def attention(bounded):
        def run(*ops):
            om = mem_attn(bounded)(*ops[9:])
            ob = mla(bounded)(*ops[6:9], om)
            return (dsa(bounded)(*ops[:6], ob), ob, om)
        return run

    k_mem_gain = jnp.max(jnp.abs(g_kn_m[0]))
    worst = jnp.maximum(jnp.maximum(score_bound(COL_GQA, COL_GKA, A_HEAD_DIM), score_bound(COL_GQB, COL_GKB, B_QK)),
                        M_HEAD_DIM ** 0.5 * LOG2E * 1.02 * gain_max[COL_GQM] * k_mem_gain)
    oa, ob, om = lax.cond(worst <= BOUNDED_SCORE_LIMIT, attention(True), attention(False),
                          qat, qib, wt, ka, vat, ki, qbt, kb, vbt, qmt, mem.reshape(b * m_len, d), mem_gains,
                          wmk, wmv_t)

    out = pl.pallas_call(
        _final_kernel,
        grid=(n // TM,),
        in_specs=[tile(d), tile(512), tile(512), tile(512), _full(rows.shape), _full(wz.shape), _full(wg.shape),
                  _full(wb.shape), _full(wo.shape)],
        out_specs=tile(d), out_shape=jax.ShapeDtypeStruct((n, d), x.dtype),
        compiler_params=_params(1), name="final",
    )(x2, oa, ob, om, rows, wz, wg, wb, wo)
    return out.reshape(b, s, d)
```
